```python
import math
import functools
import jax
import jax.numpy as jnp
from jax import lax
import numpy as np

D_MODEL = 1024
BATCH = 16
SEQ = 2048
DEPTH = 1
DEC_BATCH = 32
DEC_SEQ = 8
PAST_LEN = 16384
PAGE_SIZE = 128

D_MIX = D_MODEL
D_ATTN = D_MIX // 2
D_SSM = D_MIX - D_ATTN
HEAD_DIM = 64
N_HEADS = D_ATTN // HEAD_DIM
N_KV_HEADS = N_HEADS
ATTN_SCALE = HEAD_DIM ** -0.5
Q_BLOCK = 128
FORGET_BIAS_INIT = 3.0
SSM_GROUP = 16
N_SSM_GROUPS = D_SSM // SSM_GROUP
SSM_STATE = 64
DT_MIN = 1e-3
DT_MAX = 1e-1
N_EXPERTS = 64
TOP_K = 8
D_EXPERT = 256
D_SHARED = 256
ROUTE_SCALE = 2.5
ROUTER_BIAS_SCALE = 0.01
MOE_BLOCK = 128
RMS_EPS = 1e-6
IN_SPLITS = (D_ATTN, 2 * D_ATTN, 3 * D_ATTN, 3 * D_ATTN + N_HEADS)
D_IN = 3 * D_ATTN + N_HEADS + D_SSM

kernel_name = 'hymba_fox_s5_moe_step'


def rms_norm(x, w):
    x32 = x.astype(jnp.float32)
    y = x32 * lax.rsqrt(jnp.mean(x32 * x32, axis=-1, keepdims=True) + RMS_EPS)
    return (y * w.astype(jnp.float32)).astype(x.dtype)


def fox_attend(q, c_q, q_pos, k, v, c_k, k_pos):
    s = jnp.einsum('bqhd,bkhd->bhqk', q.astype(jnp.float32), k.astype(jnp.float32)) * ATTN_SCALE
    bias = jnp.transpose(c_q, (0, 2, 1))[:, :, :, None] - jnp.transpose(c_k, (0, 2, 1))[:, :, None, :]
    causal = k_pos[None, :] <= q_pos[:, None]
    s = jnp.where(causal[None, None], s + bias, -jnp.inf)
    p = jax.nn.softmax(s, axis=-1)
    return jnp.einsum('bhqk,bkhd->bqhd', p, v.astype(jnp.float32)).astype(v.dtype)


def fox_prompt(q, k, v, logf):
    bsz, length = q.shape[0], q.shape[1]
    n_blk = length // Q_BLOCK
    c = jnp.cumsum(logf, axis=1)
    qb = q.reshape(bsz, n_blk, Q_BLOCK, N_HEADS, HEAD_DIM).transpose(1, 0, 2, 3, 4)
    cb = c.reshape(bsz, n_blk, Q_BLOCK, N_HEADS).transpose(1, 0, 2, 3)
    k_pos = jnp.arange(length)

    def block(args):
        q_i, c_i, i = args
        q_pos = i * Q_BLOCK + jnp.arange(Q_BLOCK)
        return fox_attend(q_i, c_i, q_pos, k, v, c, k_pos)

    o = lax.map(block, (qb, cb, jnp.arange(n_blk)))
    return o.transpose(1, 0, 2, 3, 4).reshape(bsz, length, N_HEADS * HEAD_DIM)


def fox_sample(q, k_new, v_new, logf_new, cache_k, cache_v, cache_logf, page_table, layer):
    n_new = q.shape[1]
    past = page_table.shape[1] * PAGE_SIZE
    q_pos = past + jnp.arange(n_new)
    k_pos = jnp.arange(past + n_new)

    def one_seq(args):
        pages, q_b, k_b, v_b, lf_b = args
        k_all = jnp.concatenate([cache_k[layer, pages].reshape(past, N_KV_HEADS, HEAD_DIM),
                                 k_b.astype(cache_k.dtype)], axis=0)
        v_all = jnp.concatenate([cache_v[layer, pages].reshape(past, N_KV_HEADS, HEAD_DIM),
                                 v_b.astype(cache_v.dtype)], axis=0)
        lf_all = jnp.concatenate([cache_logf[layer, pages].reshape(past, N_HEADS).astype(jnp.float32),
                                  lf_b], axis=0)
        c = jnp.cumsum(lf_all, axis=0)
        o = fox_attend(q_b[None], c[None, past:], q_pos, k_all[None], v_all[None], c[None], k_pos)
        return o[0]

    o = lax.map(one_seq, (page_table, q, k_new, v_new, logf_new))
    return o.reshape(q.shape[0], n_new, N_HEADS * HEAD_DIM).astype(q.dtype)


def _linear_combine(e1, e2):
    a1, b1 = e1
    a2, b2 = e2
    return a2 * a1, a2 * b1 + b2


def s5_scan(u, h0_re, h0_im, p):
    bsz, length, _ = u.shape
    f32 = jnp.float32
    u32 = u.astype(f32).reshape(bsz, length, N_SSM_GROUPS, SSM_GROUP)
    lam = lax.complex(p['lambda_re'].astype(f32), p['lambda_im'].astype(f32))
    dt = jnp.exp(p['log_dt'].astype(f32))[:, None]
    lam_bar = jnp.exp(lam * dt)
    b_bar = ((lam_bar - 1.0) / lam)[:, :, None] * lax.complex(p['b_re'].astype(f32), p['b_im'].astype(f32))
    c_mat = lax.complex(p['c_re'].astype(f32), p['c_im'].astype(f32))
    bu = jnp.einsum('gpc,blgc->blgp', b_bar, u32.astype(jnp.complex64))
    h0 = lax.complex(h0_re.astype(f32), h0_im.astype(f32))[:, None]
    elems_b = jnp.concatenate([h0, bu], axis=1)
    elems_a = jnp.broadcast_to(lam_bar, (1, length + 1) + lam_bar.shape)
    _, h = lax.associative_scan(_linear_combine, (elems_a, elems_b), axis=1)
    h = h[:, 1:]
    y = jnp.real(jnp.einsum('gcp,blgp->blgc', c_mat, h)) \
        + p['d_skip'].astype(f32).reshape(N_SSM_GROUPS, SSM_GROUP) * u32
    return y.reshape(bsz, length, D_SSM), h[:, -1]


def ssm_branch(u, h0_re, h0_im, p):
    y, h_last = s5_scan(u, h0_re, h0_im, p)
    z = jax.nn.gelu(y)
    out = z * jax.nn.sigmoid(z @ p['w_glu'].astype(jnp.float32) + p['b_glu'].astype(jnp.float32))
    return out.astype(u.dtype), h_last


def moe_ffn(h, p):
    n_tok = h.shape[0]
    score = jax.nn.sigmoid(jnp.dot(h.astype(jnp.float32), p['w_router'].astype(jnp.float32)))
    _, idx = lax.top_k(score + p['router_bias'].astype(jnp.float32), TOP_K)
    g = jnp.take_along_axis(score, idx, axis=-1)
    g = ROUTE_SCALE * g / jnp.sum(g, axis=-1, keepdims=True)
    n_slots = n_tok * TOP_K
    e_flat = idx.reshape(-1)
    order = jnp.argsort(e_flat)
    e_sorted = e_flat[order]
    tok_sorted = (order // TOP_K).astype(jnp.int32)
    g_sorted = g.reshape(-1)[order]
    counts = jnp.bincount(e_flat, length=N_EXPERTS)
    start = jnp.cumsum(counts) - counts
    padded = (counts + MOE_BLOCK - 1) // MOE_BLOCK * MOE_BLOCK
    pend = jnp.cumsum(padded)
    pstart = pend - padded
    dest = pstart[e_sorted] + jnp.arange(n_slots) - start[e_sorted]
    n_blk = -(-n_slots // MOE_BLOCK) + N_EXPERTS
    tok_buf = jnp.zeros((n_blk * MOE_BLOCK,), jnp.int32).at[dest].set(tok_sorted)
    gate_buf = jnp.zeros((n_blk * MOE_BLOCK,), jnp.float32).at[dest].set(g_sorted)
    blk_expert = jnp.minimum(jnp.searchsorted(pend, jnp.arange(n_blk) * MOE_BLOCK, side='right'),
                             N_EXPERTS - 1)
    w_gate_e, w_up_e, w_down_e = p['w_gate_e'], p['w_up_e'], p['w_down_e']

    def body(acc, blk):
        tb, gb, e = blk
        xb = h[tb]
        a = jax.nn.silu(xb @ w_gate_e[e]) * (xb @ w_up_e[e])
        yb = (a @ w_down_e[e]).astype(jnp.float32)
        return acc.at[tb].add(yb * gb[:, None]), None

    acc, _ = lax.scan(body, jnp.zeros((n_tok, D_MODEL), jnp.float32),
                      (tok_buf.reshape(n_blk, MOE_BLOCK), gate_buf.reshape(n_blk, MOE_BLOCK), blk_expert))
    shared = (jax.nn.silu(h @ p['w_gate_s']) * (h @ p['w_up_s'])) @ p['w_down_s']
    return (acc + shared.astype(jnp.float32)).astype(h.dtype)


def trunk_layer(x, attn_fn, h0_re, h0_im, p):
    bsz, length, _ = x.shape
    hn = rms_norm(x, p['norm_mix_w'])
    q, k, v, f_logit, u = jnp.split(hn @ p['w_in'], IN_SPLITS, axis=-1)
    q = rms_norm(q.reshape(bsz, length, N_HEADS, HEAD_DIM), p['q_norm_w'])
    k = rms_norm(k.reshape(bsz, length, N_KV_HEADS, HEAD_DIM), p['k_norm_w'])
    v = v.reshape(bsz, length, N_KV_HEADS, HEAD_DIM)
    logf = jax.nn.log_sigmoid(f_logit.astype(jnp.float32) + p['b_forget'].astype(jnp.float32))
    attn = attn_fn(q, k, v, logf)
    ssm, h_last = ssm_branch(u, h0_re, h0_im, p)
    mix = jnp.concatenate([rms_norm(attn, p['attn_out_norm_w']), rms_norm(ssm, p['ssm_out_norm_w'])], axis=-1)
    x = x + (mix @ p['w_out']).astype(x.dtype)
    ffn = moe_ffn(rms_norm(x, p['norm_ffn_w']).reshape(bsz * length, D_MODEL), p)
    x = x + ffn.reshape(bsz, length, D_MODEL).astype(x.dtype)
    return x, k, v, logf, jnp.real(h_last), jnp.imag(h_last)


def setup_inputs(seed: int = 0) -> dict:
    key = jax.random.key(seed)
    ks = jax.random.split(key, 36)
    f32 = jnp.float32
    n_pages = PAST_LEN // PAGE_SIZE
    n_used = DEC_BATCH * n_pages
    n_pool = n_used + max(1, n_used // 4)

    def nrm(k, shape, scale=1.0):
        return scale * jax.random.normal(k, shape, f32)

    def gain(k, shape):
        return 1.0 + nrm(k, shape, 0.02)

    page_table = jax.random.permutation(ks[5], n_pool)[:n_used].reshape(DEC_BATCH, n_pages).astype(jnp.int32)
    ssm_n = jnp.arange(SSM_STATE, dtype=f32)
    return {
        'x_prompt': nrm(ks[0], (BATCH, SEQ, D_MODEL)),
        'x_sample': nrm(ks[1], (DEC_BATCH, DEC_SEQ, D_MODEL)),
        'cache_k': nrm(ks[2], (DEPTH, n_pool, PAGE_SIZE, N_KV_HEADS, HEAD_DIM)),
        'cache_v': nrm(ks[3], (DEPTH, n_pool, PAGE_SIZE, N_KV_HEADS, HEAD_DIM)),
        'cache_logf': jax.nn.log_sigmoid(FORGET_BIAS_INIT + nrm(ks[4], (DEPTH, n_pool, PAGE_SIZE, N_HEADS))),
        'page_table': page_table,
        'state_ssm_re': nrm(ks[6], (DEPTH, DEC_BATCH, N_SSM_GROUPS, SSM_STATE), 0.5),
        'state_ssm_im': nrm(ks[7], (DEPTH, DEC_BATCH, N_SSM_GROUPS, SSM_STATE), 0.5),
        'norm_mix_w': gain(ks[8], (DEPTH, D_MODEL)),
        'w_in': nrm(ks[9], (DEPTH, D_MODEL, D_IN), D_MODEL ** -0.5),
        'b_forget': FORGET_BIAS_INIT + nrm(ks[10], (DEPTH, N_HEADS), 0.5),
        'q_norm_w': gain(ks[11], (DEPTH, HEAD_DIM)),
        'k_norm_w': gain(ks[12], (DEPTH, HEAD_DIM)),
        'lambda_re': -0.5 + nrm(ks[13], (DEPTH, N_SSM_GROUPS, SSM_STATE), 0.01),
        'lambda_im': math.pi * ssm_n + nrm(ks[14], (DEPTH, N_SSM_GROUPS, SSM_STATE), 0.01),
        'log_dt': jax.random.uniform(ks[15], (DEPTH, N_SSM_GROUPS), f32, math.log(DT_MIN), math.log(DT_MAX)),
        'b_re': nrm(ks[16], (DEPTH, N_SSM_GROUPS, SSM_STATE, SSM_GROUP), (2 * SSM_GROUP) ** -0.5),
        'b_im': nrm(ks[17], (DEPTH, N_SSM_GROUPS, SSM_STATE, SSM_GROUP), (2 * SSM_GROUP) ** -0.5),
        'c_re': nrm(ks[18], (DEPTH, N_SSM_GROUPS, SSM_GROUP, SSM_STATE), SSM_STATE ** -0.5),
        'c_im': nrm(ks[19], (DEPTH, N_SSM_GROUPS, SSM_GROUP, SSM_STATE), SSM_STATE ** -0.5),
        'd_skip': nrm(ks[20], (DEPTH, D_SSM)),
        'w_glu': nrm(ks[21], (DEPTH, D_SSM, D_SSM), D_SSM ** -0.5),
        'b_glu': nrm(ks[22], (DEPTH, D_SSM), 0.01),
        'attn_out_norm_w': gain(ks[23], (DEPTH, D_ATTN)),
        'ssm_out_norm_w': gain(ks[24], (DEPTH, D_SSM)),
        'w_out': nrm(ks[25], (DEPTH, D_MIX, D_MODEL), D_MIX ** -0.5),
        'norm_ffn_w': gain(ks[26], (DEPTH, D_MODEL)),
        'w_router': nrm(ks[27], (DEPTH, D_MODEL, N_EXPERTS), D_MODEL ** -0.5),
        'router_bias': nrm(ks[28], (DEPTH, N_EXPERTS), ROUTER_BIAS_SCALE),
        'w_gate_e': nrm(ks[29], (DEPTH, N_EXPERTS, D_MODEL, D_EXPERT), D_MODEL ** -0.5),
        'w_up_e': nrm(ks[30], (DEPTH, N_EXPERTS, D_MODEL, D_EXPERT), D_MODEL ** -0.5),
        'w_down_e': nrm(ks[31], (DEPTH, N_EXPERTS, D_EXPERT, D_MODEL), D_EXPERT ** -0.5),
        'w_gate_s': nrm(ks[32], (DEPTH, D_MODEL, D_SHARED), D_MODEL ** -0.5),
        'w_up_s': nrm(ks[33], (DEPTH, D_MODEL, D_SHARED), D_MODEL ** -0.5),
        'w_down_s': nrm(ks[34], (DEPTH, D_SHARED, D_MODEL), D_SHARED ** -0.5),
    }


def reference(x_prompt, x_sample, cache_k, cache_v, cache_logf, page_table, state_ssm_re, state_ssm_im,
              norm_mix_w, w_in, b_forget, q_norm_w, k_norm_w, lambda_re, lambda_im, log_dt,
              b_re, b_im, c_re, c_im, d_skip, w_glu, b_glu, attn_out_norm_w, ssm_out_norm_w, w_out,
              norm_ffn_w, w_router, router_bias, w_gate_e, w_up_e, w_down_e, w_gate_s, w_up_s, w_down_s):
    x_p, x_s = x_prompt, x_sample
    k_p, v_p, lf_p, re_p, im_p = [], [], [], [], []
    k_s, v_s, lf_s, re_s, im_s = [], [], [], [], []
    for l in range(DEPTH):
        p = {
            'norm_mix_w': norm_mix_w[l], 'w_in': w_in[l], 'b_forget': b_forget[l],
            'q_norm_w': q_norm_w[l], 'k_norm_w': k_norm_w[l],
            'lambda_re': lambda_re[l], 'lambda_im': lambda_im[l], 'log_dt': log_dt[l],
            'b_re': b_re[l], 'b_im': b_im[l], 'c_re': c_re[l], 'c_im': c_im[l], 'd_skip': d_skip[l],
            'w_glu': w_glu[l], 'b_glu': b_glu[l],
            'attn_out_norm_w': attn_out_norm_w[l], 'ssm_out_norm_w': ssm_out_norm_w[l], 'w_out': w_out[l],
            'norm_ffn_w': norm_ffn_w[l], 'w_router': w_router[l], 'router_bias': router_bias[l],
            'w_gate_e': w_gate_e[l], 'w_up_e': w_up_e[l], 'w_down_e': w_down_e[l],
            'w_gate_s': w_gate_s[l], 'w_up_s': w_up_s[l], 'w_down_s': w_down_s[l],
        }
        h0 = jnp.zeros((x_p.shape[0], N_SSM_GROUPS, SSM_STATE), jnp.float32)
        x_p, kk, vv, lf, hr, hi = trunk_layer(x_p, fox_prompt, h0, h0, p)
        k_p.append(kk); v_p.append(vv); lf_p.append(lf); re_p.append(hr); im_p.append(hi)
        attn_sample = functools.partial(fox_sample, cache_k=cache_k, cache_v=cache_v, cache_logf=cache_logf,
                                        page_table=page_table, layer=l)
        x_s, kk, vv, lf, hr, hi = trunk_layer(x_s, attn_sample, state_ssm_re[l], state_ssm_im[l], p)
        k_s.append(kk); v_s.append(vv); lf_s.append(lf); re_s.append(hr); im_s.append(hi)
    return (x_p, x_s, jnp.stack(k_p), jnp.stack(v_p), jnp.stack(lf_p), jnp.stack(re_p), jnp.stack(im_p),
            jnp.stack(k_s), jnp.stack(v_s), jnp.stack(lf_s), jnp.stack(re_s), jnp.stack(im_s))
```

```python
import functools
import math

import jax
import jax.numpy as jnp
from jax import lax
from jax.experimental import pallas as pl
from jax.experimental.pallas import tpu as pltpu
from jax.experimental.pallas import tpu_sc as plsc

F32 = jnp.float32
BF16 = jnp.bfloat16

D_MODEL = 1024
D_ATTN = 512
D_SSM = 512
HEAD_DIM = 64
N_HEADS = 8
ATTN_SCALE = HEAD_DIM ** -0.5
SSM_GROUP = 16
N_SSM_GROUPS = 32
SSM_STATE = 64
D_STATE = N_SSM_GROUPS * SSM_STATE
N_EXPERTS = 64
TOP_K = 8
D_EXPERT = 256
D_SHARED = 256
ROUTE_SCALE = 2.5
PAGE_SIZE = 128
RMS_EPS = 1e-6

LANES = 128
VMEM_LIMIT = 56 * 1024 * 1024

ROW_TILE = 512
ATTN_TILE = 512
PAGES_PER_STEP = 8
ROUTE_TILE = 256
FFN_BLOCK = 512
COMBINE_TILE = 256
SC_WINDOW = 128


def _params(*sem):
    return pltpu.CompilerParams(dimension_semantics=sem, vmem_limit_bytes=VMEM_LIMIT)


def _rms(x, w):
    return x * lax.rsqrt(jnp.mean(x * x, axis=-1, keepdims=True) + RMS_EPS) * w


def _lane_cumsum(x):
    n = x.shape[1]
    lane = lax.broadcasted_iota(jnp.int32, x.shape, 1)
    s = 1
    while s < n:
        x = x + jnp.where(lane >= s, pltpu.roll(x, s, axis=1), 0.0)
        s *= 2
    return x


def _inproj_body(x_ref, nw_ref, wqkv_ref, wf_ref, wu_ref, bf_ref, qnw_ref, knw_ref, seg_ref,
                 q_ref, k_ref, v_ref, kb_ref, vb_ref, lf_ref, u_ref):
    hn = _rms(x_ref[0], nw_ref[...]).astype(BF16)
    qkv = jnp.dot(hn, wqkv_ref[...], preferred_element_type=F32)
    seg = seg_ref[...]

    def head_norm(t, w):
        t2 = t * t
        hi = t2.astype(BF16)
        lo = (t2 - hi.astype(F32)).astype(BF16)
        ms = jnp.dot(hi, seg, preferred_element_type=F32) + jnp.dot(lo, seg, preferred_element_type=F32)
        return t * lax.rsqrt(ms + RMS_EPS) * w

    q = head_norm(qkv[:, :D_ATTN], qnw_ref[...])
    k = head_norm(qkv[:, D_ATTN:2 * D_ATTN], knw_ref[...])
    v = qkv[:, 2 * D_ATTN:]
    q_ref[...] = (q * ATTN_SCALE).astype(BF16)
    k_ref[...] = k
    v_ref[...] = v
    kb_ref[...] = k.astype(BF16)
    vb_ref[...] = v.astype(BF16)
    z = jnp.dot(hn, wf_ref[...], preferred_element_type=F32) + bf_ref[...]
    lf = jnp.minimum(z, 0.0) - jnp.log(1.0 + jnp.exp(-jnp.abs(z)))
    lf_ref[...] = lf[:, :N_HEADS]
    u_ref[...] = jnp.dot(hn, wu_ref[...], preferred_element_type=F32)


def _inproj(x3, w):
    bsz, length, _ = x3.shape
    tm = min(ROW_TILE, length)
    nt = length // tm
    rows = bsz * length
    row_map = lambda b, t: (b * nt + t, 0)
    const = lambda b, t: (0, 0)
    full = lambda a: pl.BlockSpec(a.shape, const)
    consts = [w['norm_mix_w'], w['wqkv'], w['wf'], w['wu'], w['b_forget'], w['q_norm_w'], w['k_norm_w'], w['seg']]
    return pl.pallas_call(
        _inproj_body,
        grid=(bsz, nt),
        in_specs=[pl.BlockSpec((1, tm, D_MODEL), lambda b, t: (b, t, 0))] + [full(a) for a in consts],
        out_specs=[pl.BlockSpec((tm, D_ATTN), row_map)] * 5
        + [pl.BlockSpec((tm, N_HEADS), row_map), pl.BlockSpec((tm, D_SSM), lambda b, t: (t, b))],
        out_shape=[jax.ShapeDtypeStruct((rows, D_ATTN), BF16),
                   jax.ShapeDtypeStruct((rows, D_ATTN), F32),
                   jax.ShapeDtypeStruct((rows, D_ATTN), F32),
                   jax.ShapeDtypeStruct((rows, D_ATTN), BF16),
                   jax.ShapeDtypeStruct((rows, D_ATTN), BF16),
                   jax.ShapeDtypeStruct((rows, N_HEADS), F32),
                   jax.ShapeDtypeStruct((length, bsz * D_SSM), F32)],
        compiler_params=_params("parallel", "parallel"),
        name="inproj",
    )(x3, *consts)


def _cumsum_body(lf_ref, c_ref):
    c_ref[0] = _lane_cumsum(lf_ref[0])


def _cumsum_lanes(lft):
    bsz, nh, length = lft.shape
    spec = pl.BlockSpec((1, nh, length), lambda b: (b, 0, 0))
    return pl.pallas_call(
        _cumsum_body, grid=(bsz,), in_specs=[spec], out_specs=spec,
        out_shape=jax.ShapeDtypeStruct(lft.shape, F32),
        compiler_params=_params("parallel"), name="logf_cumsum",
    )(lft)


def _attn_prompt_body(q_ref, k_ref, v_ref, c_ref, ct_ref, o_ref, acc_ref, m_ref, l_ref, *, tile):
    i = pl.program_id(1)
    lane = lax.broadcasted_iota(jnp.int32, (1, LANES), 1)
    row = lax.broadcasted_iota(jnp.int32, (tile, tile), 0)
    col = lax.broadcasted_iota(jnp.int32, (tile, tile), 1)
    for hp in range(N_HEADS // 2):
        lanes = slice(LANES * hp, LANES * (hp + 1))
        qp = q_ref[:, lanes]
        pair_out = jnp.zeros((tile, LANES), F32)
        for hh in range(2):
            h = 2 * hp + hh
            hmask = (lane >= HEAD_DIM * hh) & (lane < HEAD_DIM * (hh + 1))
            qh = jnp.where(hmask, qp, jnp.zeros_like(qp))
            cq = c_ref[:, h:h + 1]
            m_ref[...] = jnp.full((tile, 1), -jnp.inf, F32)
            l_ref[...] = jnp.zeros((tile, 1), F32)
            acc_ref[...] = jnp.zeros((tile, LANES), F32)

            def kstep(j, diagonal, qh=qh, cq=cq, hmask=hmask, lanes=lanes, h=h):
                r0 = pl.multiple_of(j * tile, tile)
                kj = k_ref[pl.ds(r0, tile), lanes]
                vj = v_ref[pl.ds(r0, tile), lanes]
                s = lax.dot_general(qh, kj, (((1,), (1,)), ((), ())), preferred_element_type=F32)
                s = s + (cq - ct_ref[0, h:h + 1, pl.ds(r0, tile)])
                if diagonal:
                    s = jnp.where(row >= col, s, -jnp.inf)
                m_old = m_ref[...]
                m_new = jnp.maximum(m_old, jnp.max(s, axis=1, keepdims=True))
                alpha = jnp.exp(m_old - m_new)
                p = jnp.exp(s - m_new)
                l_ref[...] = alpha * l_ref[...] + jnp.sum(p, axis=1, keepdims=True)
                vh = jnp.where(hmask, vj, jnp.zeros_like(vj))
                acc_ref[...] = alpha * acc_ref[...] + jnp.dot(p.astype(BF16), vh, preferred_element_type=F32)
                m_ref[...] = m_new

            def body(j, carry):
                kstep(j, False)
                return carry

            lax.fori_loop(0, i, body, 0)
            kstep(i, True)
            pair_out = pair_out + acc_ref[...] / l_ref[...]
        o_ref[:, lanes] = pair_out


def _attn_prompt(q, kb, vb, c, ct, bsz, length):
    tile = min(ATTN_TILE, length)
    nq = length // tile
    q_map = lambda b, i: (b * nq + i, 0)
    seq_map = lambda b, i: (b, 0)
    return pl.pallas_call(
        functools.partial(_attn_prompt_body, tile=tile),
        grid=(bsz, nq),
        in_specs=[pl.BlockSpec((tile, D_ATTN), q_map),
                  pl.BlockSpec((length, D_ATTN), seq_map),
                  pl.BlockSpec((length, D_ATTN), seq_map),
                  pl.BlockSpec((tile, N_HEADS), q_map),
                  pl.BlockSpec((1, N_HEADS, length), lambda b, i: (b, 0, 0))],
        out_specs=pl.BlockSpec((tile, D_ATTN), q_map),
        out_shape=jax.ShapeDtypeStruct((bsz * length, D_ATTN), F32),
        scratch_shapes=[pltpu.VMEM((tile, LANES), F32), pltpu.VMEM((tile, 1), F32), pltpu.VMEM((tile, 1), F32)],
        compiler_params=_params("parallel", "parallel"),
        name="attn_prompt",
    )(q, kb, vb, c, ct)


def _attn_sample_body(pt_ref, q_ref, *refs, pages, n_new):
    del pt_ref
    k_refs, v_refs, lf_refs = refs[:pages], refs[pages:2 * pages], refs[2 * pages:3 * pages]
    kn_ref, vn_ref, lfn_ref, o_ref, m_ref, l_ref, acc_ref, c_ref = refs[3 * pages:]
    j = pl.program_id(1)
    rows = N_HEADS * n_new

    @pl.when(j == 0)
    def _():
        m_ref[...] = jnp.full(m_ref.shape, -jnp.inf, F32)
        l_ref[...] = jnp.zeros(l_ref.shape, F32)
        acc_ref[...] = jnp.zeros(acc_ref.shape, F32)
        c_ref[...] = jnp.zeros(c_ref.shape, F32)

    q = q_ref[0]

    def chunk(kc, vc, lft, valid):
        n = kc.shape[0]
        s = lax.dot_general(q, kc, (((1,), (1,)), ((), ())), preferred_element_type=F32)
        ck = _lane_cumsum(lft) + c_ref[:, 0:1]
        c_ref[...] = jnp.broadcast_to(ck[:, n - 1:n], c_ref.shape)
        s = s - jnp.concatenate([jnp.broadcast_to(ck[h:h + 1], (n_new, n)) for h in range(N_HEADS)], axis=0)
        if valid is not None:
            s = jnp.where(valid, s, -jnp.inf)
        m_old = m_ref[...]
        m_new = jnp.maximum(m_old, jnp.max(s, axis=1, keepdims=True))
        alpha = jnp.exp(m_old - m_new)
        p = jnp.exp(s - m_new)
        l_ref[...] = alpha * l_ref[...] + jnp.sum(p, axis=1, keepdims=True)
        acc_ref[...] = alpha * acc_ref[...] + jnp.dot(p.astype(BF16), vc, preferred_element_type=F32)
        m_ref[...] = m_new

    chunk(jnp.concatenate([r[0] for r in k_refs], axis=0).astype(BF16),
          jnp.concatenate([r[0] for r in v_refs], axis=0).astype(BF16),
          jnp.concatenate([r[0] for r in lf_refs], axis=1), None)

    @pl.when(j == pl.num_programs(1) - 1)
    def _():
        qi = lax.broadcasted_iota(jnp.int32, (rows, PAGE_SIZE), 0) % n_new
        kj = lax.broadcasted_iota(jnp.int32, (rows, PAGE_SIZE), 1)
        chunk(kn_ref[0].astype(BF16), vn_ref[0].astype(BF16), lfn_ref[0], kj <= qi)
        o_full = acc_ref[...] / l_ref[...]
        lane = lax.broadcasted_iota(jnp.int32, (1, D_ATTN), 1)
        out = jnp.zeros((n_new, D_ATTN), F32)
        for h in range(N_HEADS):
            hmask = (lane >= HEAD_DIM * h) & (lane < HEAD_DIM * (h + 1))
            out = out + jnp.where(hmask, o_full[n_new * h:n_new * (h + 1)], 0.0)
        o_ref[0] = out


def _attn_sample(qbd, cache_k, cache_v, cache_lft, page_table, kn, vn, lfn, n_new):
    bsz, n_pages = page_table.shape
    pages = min(PAGES_PER_STEP, n_pages)
    nj = n_pages // pages
    rows = N_HEADS * n_new

    def page_map(p):
        return lambda b, j, pt: (pt[b * n_pages + j * pages + p], 0, 0)

    seq_map = lambda b, j, pt: (b, 0, 0)
    in_specs = [pl.BlockSpec((1, rows, D_ATTN), seq_map)]
    in_specs += [pl.BlockSpec((1, PAGE_SIZE, D_ATTN), page_map(p)) for p in range(pages)]
    in_specs += [pl.BlockSpec((1, PAGE_SIZE, D_ATTN), page_map(p)) for p in range(pages)]
    in_specs += [pl.BlockSpec((1, N_HEADS, PAGE_SIZE), page_map(p)) for p in range(pages)]
    in_specs += [pl.BlockSpec((1, PAGE_SIZE, D_ATTN), seq_map)] * 2 + [pl.BlockSpec((1, N_HEADS, PAGE_SIZE), seq_map)]
    return pl.pallas_call(
        functools.partial(_attn_sample_body, pages=pages, n_new=n_new),
        grid_spec=pltpu.PrefetchScalarGridSpec(
            num_scalar_prefetch=1, grid=(bsz, nj), in_specs=in_specs,
            out_specs=pl.BlockSpec((1, n_new, D_ATTN), seq_map),
            scratch_shapes=[pltpu.VMEM((rows, 1), F32), pltpu.VMEM((rows, 1), F32),
                            pltpu.VMEM((rows, D_ATTN), F32), pltpu.VMEM((N_HEADS, LANES), F32)]),
        out_shape=jax.ShapeDtypeStruct((bsz, n_new, D_ATTN), F32),
        compiler_params=_params("parallel", "arbitrary"),
        name="attn_sample",
    )(page_table.reshape(-1), qbd, *([cache_k] * pages), *([cache_v] * pages), *([cache_lft] * pages), kn, vn, lfn)


SCAN_LANES = 512


def _ssm_body(u_ref, h0_ref, bw_ref, cre_ref, cim_ref, ar_ref, ai_ref, dsk_ref, wglu_ref, bglu_ref,
              out_ref, hlast_ref, hbuf_ref, state_ref, ubuf_ref, *, steps, bsz):
    g = pl.program_id(0)

    @pl.when(g == 0)
    def _():
        state_ref[...] = h0_ref[...]

    n_tiles = D_SSM // LANES
    for b in range(bsz):
        for t in range(n_tiles):
            col = D_SSM * b + LANES * t
            ubuf_ref[t, pl.ds(b, steps, stride=bsz), :] = u_ref[:, col:col + LANES]
    u = jnp.concatenate([ubuf_ref[t] for t in range(n_tiles)], axis=1)
    ub = u.astype(BF16)
    for p in range(N_SSM_GROUPS // 2):
        t = p // 4
        bu = jnp.dot(ub[:, LANES * t:LANES * (t + 1)], bw_ref[p], preferred_element_type=F32)
        hbuf_ref[:, LANES * p:LANES * (p + 1)] = bu[:, :LANES]
        hbuf_ref[:, D_STATE + LANES * p:D_STATE + LANES * (p + 1)] = bu[:, LANES:]

    for c in range(D_STATE // SCAN_LANES):
        re_l = slice(SCAN_LANES * c, SCAN_LANES * (c + 1))
        im_l = slice(D_STATE + SCAN_LANES * c, D_STATE + SCAN_LANES * (c + 1))
        ar = jnp.broadcast_to(ar_ref[:, re_l], (bsz, SCAN_LANES))
        ai = jnp.broadcast_to(ai_ref[:, re_l], (bsz, SCAN_LANES))

        def step(t, carry, re_l=re_l, im_l=im_l, ar=ar, ai=ai):
            re, im = carry
            r0 = pl.multiple_of(t * bsz, bsz)
            nre = ar * re - ai * im + hbuf_ref[pl.ds(r0, bsz), re_l]
            nim = ar * im + ai * re + hbuf_ref[pl.ds(r0, bsz), im_l]
            hbuf_ref[pl.ds(r0, bsz), re_l] = nre
            hbuf_ref[pl.ds(r0, bsz), im_l] = nim
            return nre, nim

        re, im = lax.fori_loop(0, steps, step, (state_ref[:, re_l], state_ref[:, im_l]))
        state_ref[:, re_l] = re
        state_ref[:, im_l] = im

    hlast_ref[...] = state_ref[...]

    ys = []
    for t in range(D_SSM // LANES):
        w = 4 * LANES
        hre = hbuf_ref[:, w * t:w * (t + 1)].astype(BF16)
        him = hbuf_ref[:, D_STATE + w * t:D_STATE + w * (t + 1)].astype(BF16)
        ys.append(jnp.dot(hre, cre_ref[t], preferred_element_type=F32)
                  + jnp.dot(him, cim_ref[t], preferred_element_type=F32))
    y = jnp.concatenate(ys, axis=1) + dsk_ref[...] * u
    z = 0.5 * y * (1.0 + jnp.tanh(math.sqrt(2.0 / math.pi) * (y + 0.044715 * (y * y * y))))
    gate = jnp.dot(z.astype(BF16), wglu_ref[...], preferred_element_type=F32) + bglu_ref[...]
    out = z / (1.0 + jnp.exp(-gate))
    for t in range(n_tiles):
        ubuf_ref[t] = out[:, LANES * t:LANES * (t + 1)]
    for b in range(bsz):
        for t in range(n_tiles):
            col = D_SSM * b + LANES * t
            out_ref[:, col:col + LANES] = ubuf_ref[t, pl.ds(b, steps, stride=bsz), :]


def _ssm(u_lb, h0, w, bsz, length):
    steps = max(1, min(length, ROW_TILE // bsz))
    rows = steps * bsz
    const2 = lambda g: (0, 0)
    const3 = lambda g: (0, 0, 0)
    full = lambda a: pl.BlockSpec(a.shape, const2 if a.ndim == 2 else const3)
    consts = [h0, w['ssm_bw'], w['ssm_cre'], w['ssm_cim'], w['ssm_ar'], w['ssm_ai'], w['d_skip'], w['w_glu'], w['b_glu']]
    return pl.pallas_call(
        functools.partial(_ssm_body, steps=steps, bsz=bsz),
        grid=(length // steps,),
        in_specs=[pl.BlockSpec((steps, bsz * D_SSM), lambda g: (g, 0))] + [full(a) for a in consts],
        out_specs=[pl.BlockSpec((steps, bsz * D_SSM), lambda g: (g, 0)), pl.BlockSpec((bsz, 2 * D_STATE), const2)],
        out_shape=[jax.ShapeDtypeStruct((length, bsz * D_SSM), F32), jax.ShapeDtypeStruct((bsz, 2 * D_STATE), F32)],
        scratch_shapes=[pltpu.VMEM((rows, 2 * D_STATE), F32), pltpu.VMEM((bsz, 2 * D_STATE), F32),
                        pltpu.VMEM((D_SSM // LANES, rows, LANES), F32)],
        compiler_params=_params("arbitrary"),
        name="ssm",
    )(u_lb, *consts)


QUARTER = D_MODEL // 4


def _pack_rows(x):
    bits = lax.bitcast_convert_type(x.astype(BF16).astype(F32), jnp.uint32)
    q = [bits[:, QUARTER * i:QUARTER * (i + 1)] for i in range(4)]
    pack = lambda lo, hi: lax.bitcast_convert_type(hi | (lo >> 16), jnp.int32)
    return pack(q[0], q[1]), pack(q[2], q[3])


def _unpack_words(w):
    bits = lax.bitcast_convert_type(w, jnp.uint32)
    lo = lax.bitcast_convert_type(bits << 16, F32)
    hi = lax.bitcast_convert_type(bits & jnp.uint32(0xFFFF0000), F32)
    return lo, hi


def _outproj_body(attn_ref, ssm_ref, x_ref, aw_ref, sw_ref, woa_ref, wos_ref, nfw_ref, wrh_ref, wrl_ref,
                  wgus_ref, wds_ref, xres_ref, hna_ref, hnb_ref, logit_ref):
    an = _rms(attn_ref[...], aw_ref[...]).astype(BF16)
    sn = _rms(ssm_ref[...], sw_ref[...]).astype(BF16)
    x1 = x_ref[0] + (jnp.dot(an, woa_ref[...], preferred_element_type=F32)
                     + jnp.dot(sn, wos_ref[...], preferred_element_type=F32))
    h2 = _rms(x1, nfw_ref[...])
    hb = h2.astype(BF16)
    hlo = (h2 - hb.astype(F32)).astype(BF16)
    hna_ref[...], hnb_ref[...] = _pack_rows(h2)
    logit_ref[...] = (jnp.dot(hb, wrh_ref[...], preferred_element_type=F32)
                      + (jnp.dot(hb, wrl_ref[...], preferred_element_type=F32)
                         + jnp.dot(hlo, wrh_ref[...], preferred_element_type=F32)))
    gu = jnp.dot(hb, wgus_ref[...], preferred_element_type=F32)
    gs = gu[:, :D_SHARED]
    act = (gs / (1.0 + jnp.exp(-gs))) * gu[:, D_SHARED:]
    xres_ref[...] = x1 + jnp.dot(act.astype(BF16), wds_ref[...], preferred_element_type=F32)


def _outproj(attn, ssm_tb, x3, w):
    bsz, length, _ = x3.shape
    tm = min(ROW_TILE, length)
    nt = length // tm
    rows = bsz * length
    row_map = lambda b, t: (b * nt + t, 0)
    const = lambda b, t: (0, 0)
    full = lambda a: pl.BlockSpec(a.shape, const)
    consts = [w['attn_out_norm_w'], w['ssm_out_norm_w'], w['wo_a'], w['wo_s'], w['norm_ffn_w'],
              w['wr_hi'], w['wr_lo'], w['wgu_s'], w['wd_s']]
    return pl.pallas_call(
        _outproj_body,
        grid=(bsz, nt),
        in_specs=[pl.BlockSpec((tm, D_ATTN), row_map), pl.BlockSpec((tm, D_SSM), lambda b, t: (t, b)),
                  pl.BlockSpec((1, tm, D_MODEL), lambda b, t: (b, t, 0))] + [full(a) for a in consts],
        out_specs=[pl.BlockSpec((tm, D_MODEL), row_map), pl.BlockSpec((tm, QUARTER), row_map),
                   pl.BlockSpec((tm, QUARTER), row_map), pl.BlockSpec((tm, LANES), row_map)],
        out_shape=[jax.ShapeDtypeStruct((rows, D_MODEL), F32), jax.ShapeDtypeStruct((rows, QUARTER), jnp.int32),
                   jax.ShapeDtypeStruct((rows, QUARTER), jnp.int32), jax.ShapeDtypeStruct((rows, LANES), F32)],
        compiler_params=_params("parallel", "parallel"),
        name="outproj",
    )(attn, ssm_tb, x3, *consts)


def _route_body(logit_ref, bias_ref, tri_ref, idx_ref, gate_ref, rank_ref, cnt_ref, carry_ref):
    @pl.when(pl.program_id(0) == 0)
    def _():
        carry_ref[...] = jnp.zeros(carry_ref.shape, F32)

    lg = logit_ref[...]
    tr = lg.shape[0]
    lane = lax.broadcasted_iota(jnp.int32, (tr, LANES), 1).astype(F32)
    score = 1.0 / (1.0 + jnp.exp(-lg))
    sel = jnp.where(lane < N_EXPERTS, score + bias_ref[...], -jnp.inf)
    picked = []
    gates = jnp.zeros((tr, LANES), F32)
    idxs = jnp.zeros((tr, LANES), F32)
    member = jnp.zeros((tr, LANES), F32)
    for k in range(TOP_K):
        best = jnp.max(sel, axis=1, keepdims=True)
        e = jnp.min(jnp.where(sel == best, lane, float(LANES)), axis=1, keepdims=True)
        hit = lane == e
        picked.append(hit)
        gates = jnp.where(lane == k, jnp.sum(jnp.where(hit, score, 0.0), axis=1, keepdims=True), gates)
        idxs = jnp.where(lane == k, e, idxs)
        member = jnp.where(hit, 1.0, member)
        sel = jnp.where(hit, -jnp.inf, sel)
    gate_ref[...] = ROUTE_SCALE * gates / jnp.sum(gates, axis=1, keepdims=True)
    idx_ref[...] = idxs.astype(jnp.int32)
    before = jnp.dot(tri_ref[...], member.astype(BF16), preferred_element_type=F32) + carry_ref[...]
    ranks = jnp.zeros((tr, LANES), F32)
    for k in range(TOP_K):
        ranks = jnp.where(lane == k, jnp.sum(jnp.where(picked[k], before, 0.0), axis=1, keepdims=True), ranks)
    rank_ref[...] = ranks.astype(jnp.int32)
    carry_ref[...] = carry_ref[...] + jnp.sum(member, axis=0, keepdims=True)
    cnt_ref[...] = carry_ref[...]


def _route(logits, bias, tri):
    n_tok = logits.shape[0]
    tr = tri.shape[0]
    tok_map = lambda i: (i, 0)
    const = lambda i: (0, 0)
    return pl.pallas_call(
        _route_body,
        grid=(n_tok // tr,),
        in_specs=[pl.BlockSpec((tr, LANES), tok_map), pl.BlockSpec((1, LANES), const), pl.BlockSpec((tr, tr), const)],
        out_specs=[pl.BlockSpec((tr, LANES), tok_map)] * 3 + [pl.BlockSpec((1, LANES), const)],
        out_shape=[jax.ShapeDtypeStruct((n_tok, LANES), jnp.int32), jax.ShapeDtypeStruct((n_tok, LANES), F32),
                   jax.ShapeDtypeStruct((n_tok, LANES), jnp.int32), jax.ShapeDtypeStruct((1, LANES), F32)],
        scratch_shapes=[pltpu.VMEM((1, LANES), F32)],
        compiler_params=_params("arbitrary"),
        name="route",
    )(logits, bias, tri)


def _gather_rows(table, indices):
    n_idx = indices.shape[0]
    width = table.shape[1]
    mesh = plsc.VectorSubcoreMesh(core_axis_name="core", subcore_axis_name="subcore")

    @pl.kernel(out_type=jax.ShapeDtypeStruct((n_idx, width), table.dtype), mesh=mesh)
    def gather(table_hbm, idx_hbm, out_hbm):
        def body(idx_vmem, out_vmem):
            pltpu.sync_copy(table_hbm.at[idx_vmem.at[0]], out_vmem)

        pltpu.emit_pipeline(
            body,
            grid=(n_idx // SC_WINDOW,),
            in_specs=[pl.BlockSpec((1, SC_WINDOW), lambda i: (0, i))],
            out_specs=[pl.BlockSpec((SC_WINDOW, width), lambda i: (i, 0))],
            core_axis_name=("core", "subcore"),
            dimension_semantics=(pltpu.PARALLEL,),
        )(idx_hbm, out_hbm)

    return gather(table, indices.reshape(1, n_idx))


def _ffn_body(be_ref, nb_ref, xa_ref, xb_ref, wgu_ref, wd_ref, ya_ref, yb_ref):
    del be_ref
    r = pl.program_id(0)

    @pl.when(r < nb_ref[0])
    def _():
        x = jnp.concatenate(_unpack_words(xa_ref[...]) + _unpack_words(xb_ref[...]), axis=1).astype(BF16)
        gu = jnp.dot(x, wgu_ref[0], preferred_element_type=F32)
        gs = gu[:, :D_EXPERT]
        act = (gs / (1.0 + jnp.exp(-gs))) * gu[:, D_EXPERT:]
        ya_ref[...], yb_ref[...] = _pack_rows(jnp.dot(act.astype(BF16), wd_ref[0], preferred_element_type=F32))

    @pl.when(r >= nb_ref[0])
    def _():
        ya_ref[...] = jnp.zeros(ya_ref.shape, jnp.int32)
        yb_ref[...] = jnp.zeros(yb_ref.shape, jnp.int32)


def _expert_ffn(xsa, xsb, blk_expert, n_used, wgu, wd):
    n_slots = xsa.shape[0]
    nb = n_slots // FFN_BLOCK
    slot_spec = pl.BlockSpec((FFN_BLOCK, QUARTER), lambda r, be, n: (r, 0))
    return pl.pallas_call(
        _ffn_body,
        grid_spec=pltpu.PrefetchScalarGridSpec(
            num_scalar_prefetch=2, grid=(nb,),
            in_specs=[slot_spec, slot_spec,
                      pl.BlockSpec((1, D_MODEL, 2 * D_EXPERT), lambda r, be, n: (be[r], 0, 0)),
                      pl.BlockSpec((1, D_EXPERT, D_MODEL), lambda r, be, n: (be[r], 0, 0))],
            out_specs=[slot_spec, slot_spec]),
        out_shape=[jax.ShapeDtypeStruct((n_slots, QUARTER), jnp.int32)] * 2,
        compiler_params=_params("arbitrary"),
        name="expert_ffn",
    )(blk_expert, n_used, xsa, xsb, wgu, wd)


def _combine_body(xres_ref, gate_ref, *refs):
    ya_refs, yb_refs, o_ref = refs[:TOP_K], refs[TOP_K:2 * TOP_K], refs[2 * TOP_K]
    g = gate_ref[...]
    acc = [xres_ref[:, QUARTER * i:QUARTER * (i + 1)] for i in range(4)]
    for k in range(TOP_K):
        gk = g[:, k:k + 1]
        parts = _unpack_words(ya_refs[k][...]) + _unpack_words(yb_refs[k][...])
        acc = [a + gk * p for a, p in zip(acc, parts)]
    o_ref[...] = jnp.concatenate(acc, axis=1)


def _combine(xres, gates, ysga, ysgb, row_offset, n_tok):
    rows = xres.shape[0]
    tc = min(COMBINE_TILE, rows)
    off = row_offset // tc
    per_k = n_tok // tc
    y_specs = [pl.BlockSpec((tc, QUARTER), lambda i, k=k: (k * per_k + off + i, 0)) for k in range(TOP_K)]
    return pl.pallas_call(
        _combine_body,
        grid=(rows // tc,),
        in_specs=[pl.BlockSpec((tc, D_MODEL), lambda i: (i, 0)),
                  pl.BlockSpec((tc, LANES), lambda i: (i + off, 0))] + y_specs + y_specs,
        out_specs=pl.BlockSpec((tc, D_MODEL), lambda i: (i, 0)),
        out_shape=jax.ShapeDtypeStruct((rows, D_MODEL), F32),
        compiler_params=_params("parallel"),
        name="combine",
    )(xres, gates, *([ysga] * TOP_K), *([ysgb] * TOP_K))


def _prepare_weights(norm_mix_w, w_in, b_forget, q_norm_w, k_norm_w, lambda_re, lambda_im, log_dt, b_re, b_im,
                     c_re, c_im, d_skip, w_glu, b_glu, attn_out_norm_w, ssm_out_norm_w, w_out, norm_ffn_w,
                     w_router, router_bias, w_gate_e, w_up_e, w_down_e, w_gate_s, w_up_s, w_down_s):
    w = {}
    row = lambda a: a.reshape(1, -1).astype(F32)
    w['norm_mix_w'] = row(norm_mix_w)
    w['wqkv'] = w_in[:, :3 * D_ATTN].astype(BF16)
    w['wf'] = jnp.pad(w_in[:, 3 * D_ATTN:3 * D_ATTN + N_HEADS], ((0, 0), (0, LANES - N_HEADS))).astype(BF16)
    w['wu'] = w_in[:, 3 * D_ATTN + N_HEADS:].astype(BF16)
    w['b_forget'] = jnp.pad(row(b_forget), ((0, 0), (0, LANES - N_HEADS)))
    w['q_norm_w'] = jnp.tile(row(q_norm_w), (1, N_HEADS))
    w['k_norm_w'] = jnp.tile(row(k_norm_w), (1, N_HEADS))
    head = jnp.arange(D_ATTN) // HEAD_DIM
    w['seg'] = jnp.where(head[:, None] == head[None, :], 1.0 / HEAD_DIM, 0.0).astype(BF16)

    dt = jnp.exp(log_dt.astype(F32))[:, None]
    lre, lim = lambda_re.astype(F32), lambda_im.astype(F32)
    a, b = lre * dt, lim * dt
    ea = jnp.exp(a)
    bar_re, bar_im = ea * jnp.cos(b), ea * jnp.sin(b)
    num_re = jnp.expm1(a) * jnp.cos(b) - 2.0 * jnp.sin(0.5 * b) ** 2
    num_im = bar_im
    den = lre * lre + lim * lim
    coef_re = (num_re * lre + num_im * lim) / den
    coef_im = (num_im * lre - num_re * lim) / den
    bb_re = coef_re[:, :, None] * b_re - coef_im[:, :, None] * b_im
    bb_im = coef_re[:, :, None] * b_im + coef_im[:, :, None] * b_re
    eye = jnp.eye(N_SSM_GROUPS, dtype=F32)

    def in_block_diag(m):
        return (m.transpose(0, 2, 1)[:, :, None, :] * eye[:, None, :, None]).reshape(D_SSM, D_STATE)

    def out_block_diag(m):
        return (m.transpose(0, 2, 1)[:, :, None, :] * eye[:, None, :, None]).reshape(D_STATE, D_SSM)

    pairs = jnp.arange(N_SSM_GROUPS // 2)

    def pair_blocks(m):
        return m.reshape(4, LANES, N_SSM_GROUPS // 2, LANES).transpose(2, 0, 1, 3)[pairs, pairs // 4]

    w['ssm_bw'] = jnp.concatenate([pair_blocks(in_block_diag(bb_re)), pair_blocks(in_block_diag(bb_im))],
                                  axis=2).astype(BF16)
    tiles = jnp.arange(D_SSM // LANES)

    def tile_blocks(m):
        return m.reshape(4, 4 * LANES, 4, LANES).transpose(0, 2, 1, 3)[tiles, tiles]

    w['ssm_cre'] = tile_blocks(out_block_diag(c_re.astype(F32))).astype(BF16)
    w['ssm_cim'] = tile_blocks(out_block_diag(-c_im.astype(F32))).astype(BF16)
    w['ssm_ar'] = bar_re.reshape(1, D_STATE)
    w['ssm_ai'] = bar_im.reshape(1, D_STATE)
    w['d_skip'] = row(d_skip)
    w['w_glu'] = w_glu.astype(BF16)
    w['b_glu'] = row(b_glu)

    w['attn_out_norm_w'] = row(attn_out_norm_w)
    w['ssm_out_norm_w'] = row(ssm_out_norm_w)
    w['wo_a'] = w_out[:D_ATTN].astype(BF16)
    w['wo_s'] = w_out[D_ATTN:].astype(BF16)
    w['norm_ffn_w'] = row(norm_ffn_w)
    wr = jnp.pad(w_router.astype(F32), ((0, 0), (0, LANES - N_EXPERTS)))
    w['wr_hi'] = wr.astype(BF16)
    w['wr_lo'] = (wr - w['wr_hi'].astype(F32)).astype(BF16)
    w['router_bias'] = jnp.pad(row(router_bias), ((0, 0), (0, LANES - N_EXPERTS)))
    w['wgu_s'] = jnp.concatenate([w_gate_s, w_up_s], axis=1).astype(BF16)
    w['wd_s'] = w_down_s.astype(BF16)
    w['wgu_e'] = jnp.concatenate([w_gate_e, w_up_e], axis=2).astype(BF16)
    w['wd_e'] = w_down_e.astype(BF16)
    return w


def _mix_prompt(x, w):
    bsz, length, _ = x.shape
    q, k, v, kb, vb, lf, u_tb = _inproj(x, w)
    ct = _cumsum_lanes(lf.reshape(bsz, length, N_HEADS).transpose(0, 2, 1))
    c = ct.transpose(0, 2, 1).reshape(bsz * length, N_HEADS)
    attn = _attn_prompt(q, kb, vb, c, ct, bsz, length)
    h0 = jnp.zeros((bsz, 2 * D_STATE), F32)
    ssm_lb, hlast = _ssm(u_tb, h0, w, bsz, length)
    xres, hna, hnb, logits = _outproj(attn, ssm_lb, x, w)
    caches = (k.reshape(1, bsz, length, N_HEADS, HEAD_DIM), v.reshape(1, bsz, length, N_HEADS, HEAD_DIM),
              lf.reshape(1, bsz, length, N_HEADS),
              hlast[:, :D_STATE].reshape(1, bsz, N_SSM_GROUPS, SSM_STATE),
              hlast[:, D_STATE:].reshape(1, bsz, N_SSM_GROUPS, SSM_STATE))
    return xres, hna, hnb, logits, caches


def _mix_sample(x, cache_k, cache_v, cache_logf, page_table, h0_re, h0_im, w):
    bsz, n_new, _ = x.shape
    rows = n_new * bsz
    xt = x.transpose(1, 0, 2).reshape(1, rows, D_MODEL)
    q, k, v, _, _, lf, u_tb = _inproj(xt, w)
    to_bm = lambda a: a.reshape(n_new, bsz, -1).transpose(1, 0, 2)
    q_b, k_b, v_b, lf_b = to_bm(q), to_bm(k), to_bm(v), to_bm(lf)
    head = jnp.arange(D_ATTN) // HEAD_DIM
    hmask = (head[None, :] == jnp.arange(N_HEADS)[:, None]).astype(BF16)
    qbd = (q_b[:, None, :, :] * hmask[None, :, None, :]).reshape(bsz, N_HEADS * n_new, D_ATTN)
    pad_keys = ((0, 0), (0, PAGE_SIZE - n_new), (0, 0))
    n_pool = cache_k.shape[1]
    attn = _attn_sample(
        qbd, cache_k[0].reshape(n_pool, PAGE_SIZE, D_ATTN), cache_v[0].reshape(n_pool, PAGE_SIZE, D_ATTN),
        cache_logf[0].astype(F32).transpose(0, 2, 1), page_table.astype(jnp.int32),
        jnp.pad(k_b, pad_keys), jnp.pad(v_b, pad_keys),
        jnp.pad(lf_b.transpose(0, 2, 1), ((0, 0), (0, 0), (0, PAGE_SIZE - n_new))), n_new)
    attn_tm = attn.transpose(1, 0, 2).reshape(rows, D_ATTN)
    h0 = jnp.concatenate([h0_re.reshape(bsz, D_STATE), h0_im.reshape(bsz, D_STATE)], axis=1).astype(F32)
    ssm_lb, hlast = _ssm(u_tb.reshape(n_new, bsz * D_SSM), h0, w, bsz, n_new)
    xres, hna, hnb, logits = _outproj(attn_tm, ssm_lb.reshape(rows, D_SSM), xt, w)
    caches = (k_b.reshape(1, bsz, n_new, N_HEADS, HEAD_DIM), v_b.reshape(1, bsz, n_new, N_HEADS, HEAD_DIM),
              lf_b.reshape(1, bsz, n_new, N_HEADS),
              hlast[:, :D_STATE].reshape(1, bsz, N_SSM_GROUPS, SSM_STATE),
              hlast[:, D_STATE:].reshape(1, bsz, N_SSM_GROUPS, SSM_STATE))
    return xres, hna, hnb, logits, caches


def _moe(hna, hnb, logits, w):
    n_tok = hna.shape[0]
    tri = (jnp.arange(ROUTE_TILE)[:, None] > jnp.arange(ROUTE_TILE)[None, :]).astype(BF16)
    idx, gates, rank, counts = _route(logits, w['router_bias'], tri)
    counts = counts[0, :N_EXPERTS].astype(jnp.int32)
    padded = (counts + FFN_BLOCK - 1) // FFN_BLOCK * FFN_BLOCK
    pend = jnp.cumsum(padded)
    pstart = pend - padded
    n_blk = -(-(n_tok * TOP_K) // FFN_BLOCK) + N_EXPERTS
    n_slots = n_blk * FFN_BLOCK
    pos = (pstart[idx[:, :TOP_K]] + rank[:, :TOP_K]).T.reshape(-1)
    tok_of_slot = jnp.zeros((n_slots,), jnp.int32).at[pos].set(jnp.arange(n_tok * TOP_K, dtype=jnp.int32) % n_tok)
    blk_expert = jnp.minimum(jnp.searchsorted(pend, jnp.arange(n_blk) * FFN_BLOCK, side='right'),
                             N_EXPERTS - 1).astype(jnp.int32)
    n_used = (pend[-1:] // FFN_BLOCK).astype(jnp.int32)
    ysa, ysb = _expert_ffn(_gather_rows(hna, tok_of_slot), _gather_rows(hnb, tok_of_slot), blk_expert, n_used,
                           w['wgu_e'], w['wd_e'])
    return gates, _gather_rows(ysa, pos), _gather_rows(ysb, pos)


def kernel(x_prompt, x_sample, cache_k, cache_v, cache_logf, page_table, state_ssm_re, state_ssm_im, norm_mix_w, w_in, b_forget, q_norm_w, k_norm_w, lambda_re, lambda_im, log_dt, b_re, b_im, c_re, c_im, d_skip, w_glu, b_glu, attn_out_norm_w, ssm_out_norm_w, w_out, norm_ffn_w, w_router, router_bias, w_gate_e, w_up_e, w_down_e, w_gate_s, w_up_s, w_down_s):
    assert norm_mix_w.shape[0] == 1, "single-layer trunk"
    w = _prepare_weights(norm_mix_w[0], w_in[0], b_forget[0], q_norm_w[0], k_norm_w[0], lambda_re[0], lambda_im[0],
                         log_dt[0], b_re[0], b_im[0], c_re[0], c_im[0], d_skip[0], w_glu[0], b_glu[0],
                         attn_out_norm_w[0], ssm_out_norm_w[0], w_out[0], norm_ffn_w[0], w_router[0], router_bias[0],
                         w_gate_e[0], w_up_e[0], w_down_e[0], w_gate_s[0], w_up_s[0], w_down_s[0])
    bp, lp, _ = x_prompt.shape
    bs, ls, _ = x_sample.shape
    xres_p, hna_p, hnb_p, lg_p, caches_p = _mix_prompt(x_prompt, w)
    xres_s, hna_s, hnb_s, lg_s, caches_s = _mix_sample(x_sample, cache_k, cache_v, cache_logf, page_table,
                                                       state_ssm_re[0], state_ssm_im[0], w)
    n_p = bp * lp
    n_tok = n_p + bs * ls
    cat = lambda a, b: jnp.concatenate([a, b], axis=0)
    gates, ysga, ysgb = _moe(cat(hna_p, hna_s), cat(hnb_p, hnb_s), cat(lg_p, lg_s), w)
    y_p = _combine(xres_p, gates, ysga, ysgb, 0, n_tok).reshape(bp, lp, D_MODEL)
    y_s = _combine(xres_s, gates, ysga, ysgb, n_p, n_tok).reshape(ls, bs, D_MODEL).transpose(1, 0, 2)
    return (y_p, y_s) + caches_p + caches_s
```

```python
import functools
import math

import jax
import jax.numpy as jnp
from jax import lax
from jax.experimental import pallas as pl
from jax.experimental.pallas import tpu as pltpu
from jax.experimental.pallas import tpu_sc as plsc

F32 = jnp.float32
BF16 = jnp.bfloat16

D_MODEL = 1024
D_ATTN = 512
D_SSM = 512
HEAD_DIM = 64
N_HEADS = 8
ATTN_SCALE = HEAD_DIM ** -0.5
LOG2E = math.log2(math.e)
SSM_GROUP = 16
N_SSM_GROUPS = 32
SSM_STATE = 64
D_STATE = N_SSM_GROUPS * SSM_STATE
N_EXPERTS = 64
TOP_K = 8
D_EXPERT = 256
D_SHARED = 256
ROUTE_SCALE = 2.5
PAGE_SIZE = 128
RMS_EPS = 1e-6

LANES = 128
VMEM_LIMIT = 56 * 1024 * 1024

ROW_TILE = 512
ATTN_TILE = 512
PAGES_PER_STEP = 8
ROUTE_TILE = 256
FFN_BLOCK = 512
COMBINE_TILE = 256
SC_WINDOW = 128


def _params(*sem):
    return pltpu.CompilerParams(dimension_semantics=sem, vmem_limit_bytes=VMEM_LIMIT)


def _rms(x, w):
    return x * lax.rsqrt(jnp.mean(x * x, axis=-1, keepdims=True) + RMS_EPS) * w


def _lane_cumsum(x):
    n = x.shape[1]
    lane = lax.broadcasted_iota(jnp.int32, x.shape, 1)
    s = 1
    while s < n:
        x = x + jnp.where(lane >= s, pltpu.roll(x, s, axis=1), 0.0)
        s *= 2
    return x


def _inproj_body(x_ref, nw_ref, wqkv_ref, wf_ref, wu_ref, bf_ref, qnw_ref, knw_ref, seg_ref,
                 q_ref, k_ref, v_ref, kb_ref, vb_ref, lf_ref, u_ref):
    hn = _rms(x_ref[0], nw_ref[...]).astype(BF16)
    qkv = jnp.dot(hn, wqkv_ref[...], preferred_element_type=F32)
    seg = seg_ref[...]

    def head_norm(t, w):
        t2 = t * t
        hi = t2.astype(BF16)
        lo = (t2 - hi.astype(F32)).astype(BF16)
        ms = jnp.dot(hi, seg, preferred_element_type=F32) + jnp.dot(lo, seg, preferred_element_type=F32)
        return t * lax.rsqrt(ms + RMS_EPS) * w

    q = head_norm(qkv[:, :D_ATTN], qnw_ref[...])
    k = head_norm(qkv[:, D_ATTN:2 * D_ATTN], knw_ref[...])
    v = qkv[:, 2 * D_ATTN:]
    q_ref[...] = (q * (ATTN_SCALE * LOG2E)).astype(BF16)
    k_ref[...] = k
    v_ref[...] = v
    kb_ref[...] = k.astype(BF16)
    vb_ref[...] = v.astype(BF16)
    z = jnp.dot(hn, wf_ref[...], preferred_element_type=F32) + bf_ref[...]
    lf = jnp.minimum(z, 0.0) - jnp.log(1.0 + jnp.exp(-jnp.abs(z)))
    lf_ref[...] = lf[:, :N_HEADS]
    u_ref[...] = jnp.dot(hn, wu_ref[...], preferred_element_type=F32)


def _inproj(x3, w):
    bsz, length, _ = x3.shape
    tm = min(ROW_TILE, length)
    nt = length // tm
    rows = bsz * length
    row_map = lambda b, t: (b * nt + t, 0)
    const = lambda b, t: (0, 0)
    full = lambda a: pl.BlockSpec(a.shape, const)
    consts = [w['norm_mix_w'], w['wqkv'], w['wf'], w['wu'], w['b_forget'], w['q_norm_w'], w['k_norm_w'], w['seg']]
    return pl.pallas_call(
        _inproj_body,
        grid=(bsz, nt),
        in_specs=[pl.BlockSpec((1, tm, D_MODEL), lambda b, t: (b, t, 0))] + [full(a) for a in consts],
        out_specs=[pl.BlockSpec((tm, D_ATTN), row_map)] * 5
        + [pl.BlockSpec((tm, N_HEADS), row_map), pl.BlockSpec((tm, D_SSM), lambda b, t: (t, b))],
        out_shape=[jax.ShapeDtypeStruct((rows, D_ATTN), BF16),
                   jax.ShapeDtypeStruct((rows, D_ATTN), F32),
                   jax.ShapeDtypeStruct((rows, D_ATTN), F32),
                   jax.ShapeDtypeStruct((rows, D_ATTN), BF16),
                   jax.ShapeDtypeStruct((rows, D_ATTN), BF16),
                   jax.ShapeDtypeStruct((rows, N_HEADS), F32),
                   jax.ShapeDtypeStruct((length, bsz * D_SSM), F32)],
        compiler_params=_params("parallel", "parallel"),
        name="inproj",
    )(x3, *consts)


def _cumsum_body(lf_ref, c_ref):
    c_ref[0] = _lane_cumsum(lf_ref[0])


def _cumsum_lanes(lft):
    bsz, nh, length = lft.shape
    spec = pl.BlockSpec((1, nh, length), lambda b: (b, 0, 0))
    return pl.pallas_call(
        _cumsum_body, grid=(bsz,), in_specs=[spec], out_specs=spec,
        out_shape=jax.ShapeDtypeStruct(lft.shape, F32),
        compiler_params=_params("parallel"), name="logf_cumsum",
    )(lft)


def _attn_prompt_body(q_ref, k_ref, v_ref, c_ref, ct_ref, o_ref, acc_ref, m_ref, l_ref, *, tile):
    i = pl.program_id(1)
    lane = lax.broadcasted_iota(jnp.int32, (1, LANES), 1)
    row = lax.broadcasted_iota(jnp.int32, (tile, tile), 0)
    col = lax.broadcasted_iota(jnp.int32, (tile, tile), 1)
    hmasks = (lane < HEAD_DIM, lane >= HEAD_DIM)
    for hp in range(N_HEADS // 2):
        lanes = slice(LANES * hp, LANES * (hp + 1))
        qp = q_ref[:, lanes]
        qh = [jnp.where(hm, qp, jnp.zeros_like(qp)) for hm in hmasks]
        cq = [c_ref[:, 2 * hp + hh:2 * hp + hh + 1] * LOG2E for hh in range(2)]
        m_ref[...] = jnp.full(m_ref.shape, -jnp.inf, F32)
        l_ref[...] = jnp.zeros(l_ref.shape, F32)
        acc_ref[...] = jnp.zeros(acc_ref.shape, F32)

        def kstep(j, diagonal, qh=qh, cq=cq, lanes=lanes, hp=hp):
            r0 = pl.multiple_of(j * tile, tile)
            kj = k_ref[pl.ds(r0, tile), lanes]
            vj = v_ref[pl.ds(r0, tile), lanes]
            for hh in range(2):
                s = lax.dot_general(qh[hh], kj, (((1,), (1,)), ((), ())), preferred_element_type=F32)
                s = s + (cq[hh] - ct_ref[0, 2 * hp + hh:2 * hp + hh + 1, pl.ds(r0, tile)] * LOG2E)
                if diagonal:
                    s = jnp.where(row >= col, s, -jnp.inf)
                m_old = m_ref[hh]
                m_new = jnp.maximum(m_old, jnp.max(s, axis=1, keepdims=True))
                alpha = jnp.exp2(m_old - m_new)
                p = jnp.exp2(s - m_new)
                l_ref[hh] = alpha * l_ref[hh] + jnp.sum(p, axis=1, keepdims=True)
                vh = jnp.where(hmasks[hh], vj, jnp.zeros_like(vj))
                acc_ref[hh] = alpha * acc_ref[hh] + jnp.dot(p.astype(BF16), vh, preferred_element_type=F32)
                m_ref[hh] = m_new

        def body(j, carry):
            kstep(j, False)
            return carry

        lax.fori_loop(0, i, body, 0)
        kstep(i, True)
        o_ref[:, lanes] = acc_ref[0] / l_ref[0] + acc_ref[1] / l_ref[1]


def _attn_prompt(q, kb, vb, c, ct, bsz, length):
    tile = min(ATTN_TILE, length)
    nq = length // tile
    q_map = lambda b, i: (b * nq + i, 0)
    seq_map = lambda b, i: (b, 0)
    return pl.pallas_call(
        functools.partial(_attn_prompt_body, tile=tile),
        grid=(bsz, nq),
        in_specs=[pl.BlockSpec((tile, D_ATTN), q_map),
                  pl.BlockSpec((length, D_ATTN), seq_map),
                  pl.BlockSpec((length, D_ATTN), seq_map),
                  pl.BlockSpec((tile, N_HEADS), q_map),
                  pl.BlockSpec((1, N_HEADS, length), lambda b, i: (b, 0, 0))],
        out_specs=pl.BlockSpec((tile, D_ATTN), q_map),
        out_shape=jax.ShapeDtypeStruct((bsz * length, D_ATTN), F32),
        scratch_shapes=[pltpu.VMEM((2, tile, LANES), F32), pltpu.VMEM((2, tile, 1), F32),
                        pltpu.VMEM((2, tile, 1), F32)],
        compiler_params=_params("parallel", "parallel"),
        name="attn_prompt",
    )(q, kb, vb, c, ct)


def _attn_sample_body(pt_ref, q_ref, *refs, pages, n_new):
    del pt_ref
    k_refs, v_refs, lf_refs = refs[:pages], refs[pages:2 * pages], refs[2 * pages:3 * pages]
    kn_ref, vn_ref, lfn_ref, o_ref, m_ref, l_ref, acc_ref, c_ref = refs[3 * pages:]
    j = pl.program_id(1)
    rows = N_HEADS * n_new

    @pl.when(j == 0)
    def _():
        m_ref[...] = jnp.full(m_ref.shape, -jnp.inf, F32)
        l_ref[...] = jnp.zeros(l_ref.shape, F32)
        acc_ref[...] = jnp.zeros(acc_ref.shape, F32)
        c_ref[...] = jnp.zeros(c_ref.shape, F32)

    q = q_ref[0]

    def chunk(kc, vc, lft, valid):
        n = kc.shape[0]
        s = lax.dot_general(q, kc, (((1,), (1,)), ((), ())), preferred_element_type=F32)
        ck = _lane_cumsum(lft) + c_ref[:, 0:1]
        c_ref[...] = jnp.broadcast_to(ck[:, n - 1:n], c_ref.shape)
        ck2 = ck * LOG2E
        s = s - jnp.concatenate([jnp.broadcast_to(ck2[h:h + 1], (n_new, n)) for h in range(N_HEADS)], axis=0)
        if valid is not None:
            s = jnp.where(valid, s, -jnp.inf)
        m_old = m_ref[...]
        m_new = jnp.maximum(m_old, jnp.max(s, axis=1, keepdims=True))
        alpha = jnp.exp2(m_old - m_new)
        p = jnp.exp2(s - m_new)
        l_ref[...] = alpha * l_ref[...] + jnp.sum(p, axis=1, keepdims=True)
        acc_ref[...] = alpha * acc_ref[...] + jnp.dot(p.astype(BF16), vc, preferred_element_type=F32)
        m_ref[...] = m_new

    chunk(jnp.concatenate([r[0] for r in k_refs], axis=0).astype(BF16),
          jnp.concatenate([r[0] for r in v_refs], axis=0).astype(BF16),
          jnp.concatenate([r[0] for r in lf_refs], axis=1), None)

    @pl.when(j == pl.num_programs(1) - 1)
    def _():
        qi = lax.broadcasted_iota(jnp.int32, (rows, PAGE_SIZE), 0) % n_new
        kj = lax.broadcasted_iota(jnp.int32, (rows, PAGE_SIZE), 1)
        chunk(kn_ref[0].astype(BF16), vn_ref[0].astype(BF16), lfn_ref[0], kj <= qi)
        o_full = acc_ref[...] / l_ref[...]
        lane = lax.broadcasted_iota(jnp.int32, (1, D_ATTN), 1)
        out = jnp.zeros((n_new, D_ATTN), F32)
        for h in range(N_HEADS):
            hmask = (lane >= HEAD_DIM * h) & (lane < HEAD_DIM * (h + 1))
            out = out + jnp.where(hmask, o_full[n_new * h:n_new * (h + 1)], 0.0)
        o_ref[0] = out


def _attn_sample(qbd, cache_k, cache_v, cache_lft, page_table, kn, vn, lfn, n_new):
    bsz, n_pages = page_table.shape
    pages = min(PAGES_PER_STEP, n_pages)
    nj = n_pages // pages
    rows = N_HEADS * n_new

    def page_map(p):
        return lambda b, j, pt: (pt[b * n_pages + j * pages + p], 0, 0)

    seq_map = lambda b, j, pt: (b, 0, 0)
    in_specs = [pl.BlockSpec((1, rows, D_ATTN), seq_map)]
    in_specs += [pl.BlockSpec((1, PAGE_SIZE, D_ATTN), page_map(p)) for p in range(pages)]
    in_specs += [pl.BlockSpec((1, PAGE_SIZE, D_ATTN), page_map(p)) for p in range(pages)]
    in_specs += [pl.BlockSpec((1, N_HEADS, PAGE_SIZE), page_map(p)) for p in range(pages)]
    in_specs += [pl.BlockSpec((1, PAGE_SIZE, D_ATTN), seq_map)] * 2 + [pl.BlockSpec((1, N_HEADS, PAGE_SIZE), seq_map)]
    return pl.pallas_call(
        functools.partial(_attn_sample_body, pages=pages, n_new=n_new),
        grid_spec=pltpu.PrefetchScalarGridSpec(
            num_scalar_prefetch=1, grid=(bsz, nj), in_specs=in_specs,
            out_specs=pl.BlockSpec((1, n_new, D_ATTN), seq_map),
            scratch_shapes=[pltpu.VMEM((rows, 1), F32), pltpu.VMEM((rows, 1), F32),
                            pltpu.VMEM((rows, D_ATTN), F32), pltpu.VMEM((N_HEADS, LANES), F32)]),
        out_shape=jax.ShapeDtypeStruct((bsz, n_new, D_ATTN), F32),
        compiler_params=_params("parallel", "arbitrary"),
        name="attn_sample",
    )(page_table.reshape(-1), qbd, *([cache_k] * pages), *([cache_v] * pages), *([cache_lft] * pages), kn, vn, lfn)


SCAN_LANES = 512


def _ssm_body(u_ref, h0_ref, bw_ref, cre_ref, cim_ref, ar_ref, ai_ref, dsk_ref, wglu_ref, bglu_ref,
              out_ref, hlast_ref, hbuf_ref, state_ref, ubuf_ref, *, steps, bsz):
    g = pl.program_id(0)

    @pl.when(g == 0)
    def _():
        state_ref[...] = h0_ref[...]

    n_tiles = D_SSM // LANES
    for b in range(bsz):
        for t in range(n_tiles):
            col = D_SSM * b + LANES * t
            ubuf_ref[t, pl.ds(b, steps, stride=bsz), :] = u_ref[:, col:col + LANES]
    u = jnp.concatenate([ubuf_ref[t] for t in range(n_tiles)], axis=1)
    ub = u.astype(BF16)
    for p in range(N_SSM_GROUPS // 2):
        t = p // 4
        bu = jnp.dot(ub[:, LANES * t:LANES * (t + 1)], bw_ref[p], preferred_element_type=F32)
        hbuf_ref[:, LANES * p:LANES * (p + 1)] = bu[:, :LANES]
        hbuf_ref[:, D_STATE + LANES * p:D_STATE + LANES * (p + 1)] = bu[:, LANES:]

    for c in range(D_STATE // SCAN_LANES):
        re_l = slice(SCAN_LANES * c, SCAN_LANES * (c + 1))
        im_l = slice(D_STATE + SCAN_LANES * c, D_STATE + SCAN_LANES * (c + 1))
        ar = jnp.broadcast_to(ar_ref[:, re_l], (bsz, SCAN_LANES))
        ai = jnp.broadcast_to(ai_ref[:, re_l], (bsz, SCAN_LANES))

        def step(t, carry, re_l=re_l, im_l=im_l, ar=ar, ai=ai):
            re, im = carry
            r0 = pl.multiple_of(t * bsz, bsz)
            nre = ar * re - ai * im + hbuf_ref[pl.ds(r0, bsz), re_l]
            nim = ar * im + ai * re + hbuf_ref[pl.ds(r0, bsz), im_l]
            hbuf_ref[pl.ds(r0, bsz), re_l] = nre
            hbuf_ref[pl.ds(r0, bsz), im_l] = nim
            return nre, nim

        re, im = lax.fori_loop(0, steps, step, (state_ref[:, re_l], state_ref[:, im_l]))
        state_ref[:, re_l] = re
        state_ref[:, im_l] = im

    hlast_ref[...] = state_ref[...]

    ys = []
    for t in range(D_SSM // LANES):
        w = 4 * LANES
        hre = hbuf_ref[:, w * t:w * (t + 1)].astype(BF16)
        him = hbuf_ref[:, D_STATE + w * t:D_STATE + w * (t + 1)].astype(BF16)
        ys.append(jnp.dot(hre, cre_ref[t], preferred_element_type=F32)
                  + jnp.dot(him, cim_ref[t], preferred_element_type=F32))
    y = jnp.concatenate(ys, axis=1) + dsk_ref[...] * u
    z = 0.5 * y * (1.0 + jnp.tanh(math.sqrt(2.0 / math.pi) * (y + 0.044715 * (y * y * y))))
    gate = jnp.dot(z.astype(BF16), wglu_ref[...], preferred_element_type=F32) + bglu_ref[...]
    out = z / (1.0 + jnp.exp(-gate))
    for t in range(n_tiles):
        ubuf_ref[t] = out[:, LANES * t:LANES * (t + 1)]
    for b in range(bsz):
        for t in range(n_tiles):
            col = D_SSM * b + LANES * t
            out_ref[:, col:col + LANES] = ubuf_ref[t, pl.ds(b, steps, stride=bsz), :]


def _ssm(u_lb, h0, w, bsz, length):
    steps = max(1, min(length, ROW_TILE // bsz))
    rows = steps * bsz
    const2 = lambda g: (0, 0)
    const3 = lambda g: (0, 0, 0)
    full = lambda a: pl.BlockSpec(a.shape, const2 if a.ndim == 2 else const3)
    consts = [h0, w['ssm_bw'], w['ssm_cre'], w['ssm_cim'], w['ssm_ar'], w['ssm_ai'], w['d_skip'], w['w_glu'], w['b_glu']]
    return pl.pallas_call(
        functools.partial(_ssm_body, steps=steps, bsz=bsz),
        grid=(length // steps,),
        in_specs=[pl.BlockSpec((steps, bsz * D_SSM), lambda g: (g, 0))] + [full(a) for a in consts],
        out_specs=[pl.BlockSpec((steps, bsz * D_SSM), lambda g: (g, 0)), pl.BlockSpec((bsz, 2 * D_STATE), const2)],
        out_shape=[jax.ShapeDtypeStruct((length, bsz * D_SSM), F32), jax.ShapeDtypeStruct((bsz, 2 * D_STATE), F32)],
        scratch_shapes=[pltpu.VMEM((rows, 2 * D_STATE), F32), pltpu.VMEM((bsz, 2 * D_STATE), F32),
                        pltpu.VMEM((D_SSM // LANES, rows, LANES), F32)],
        compiler_params=_params("arbitrary"),
        name="ssm",
    )(u_lb, *consts)


N_PLANES = D_MODEL // (2 * LANES)


def _pack_rows(x, out_ref):
    bits = lax.bitcast_convert_type(x.astype(BF16).astype(F32), jnp.uint32)
    for c in range(N_PLANES):
        lo = bits[:, 2 * LANES * c:2 * LANES * c + LANES]
        hi = bits[:, 2 * LANES * c + LANES:2 * LANES * (c + 1)]
        out_ref[c] = lax.bitcast_convert_type(hi | (lo >> 16), jnp.int32)


def _unpack_rows(ref):
    parts = []
    for c in range(N_PLANES):
        bits = lax.bitcast_convert_type(ref[c], jnp.uint32)
        parts.append(lax.bitcast_convert_type(bits << 16, F32))
        parts.append(lax.bitcast_convert_type(bits & jnp.uint32(0xFFFF0000), F32))
    return parts


def _outproj_body(attn_ref, ssm_ref, x_ref, aw_ref, sw_ref, woa_ref, wos_ref, nfw_ref, wrh_ref, wrl_ref,
                  wgus_ref, wds_ref, xres_ref, hn_ref, logit_ref):
    an = _rms(attn_ref[...], aw_ref[...]).astype(BF16)
    sn = _rms(ssm_ref[...], sw_ref[...]).astype(BF16)
    x1 = x_ref[0] + (jnp.dot(an, woa_ref[...], preferred_element_type=F32)
                     + jnp.dot(sn, wos_ref[...], preferred_element_type=F32))
    h2 = _rms(x1, nfw_ref[...])
    hb = h2.astype(BF16)
    hlo = (h2 - hb.astype(F32)).astype(BF16)
    _pack_rows(h2, hn_ref)
    logit_ref[...] = (jnp.dot(hb, wrh_ref[...], preferred_element_type=F32)
                      + (jnp.dot(hb, wrl_ref[...], preferred_element_type=F32)
                         + jnp.dot(hlo, wrh_ref[...], preferred_element_type=F32)))
    gu = jnp.dot(hb, wgus_ref[...], preferred_element_type=F32)
    gs = gu[:, :D_SHARED]
    act = (gs / (1.0 + jnp.exp(-gs))) * gu[:, D_SHARED:]
    xres_ref[...] = x1 + jnp.dot(act.astype(BF16), wds_ref[...], preferred_element_type=F32)


def _outproj(attn, ssm_tb, x3, w):
    bsz, length, _ = x3.shape
    tm = min(ROW_TILE, length)
    nt = length // tm
    rows = bsz * length
    row_map = lambda b, t: (b * nt + t, 0)
    const = lambda b, t: (0, 0)
    full = lambda a: pl.BlockSpec(a.shape, const)
    consts = [w['attn_out_norm_w'], w['ssm_out_norm_w'], w['wo_a'], w['wo_s'], w['norm_ffn_w'],
              w['wr_hi'], w['wr_lo'], w['wgu_s'], w['wd_s']]
    return pl.pallas_call(
        _outproj_body,
        grid=(bsz, nt),
        in_specs=[pl.BlockSpec((tm, D_ATTN), row_map), pl.BlockSpec((tm, D_SSM), lambda b, t: (t, b)),
                  pl.BlockSpec((1, tm, D_MODEL), lambda b, t: (b, t, 0))] + [full(a) for a in consts],
        out_specs=[pl.BlockSpec((tm, D_MODEL), row_map),
                   pl.BlockSpec((N_PLANES, tm, LANES), lambda b, t: (0, b * nt + t, 0)),
                   pl.BlockSpec((tm, LANES), row_map)],
        out_shape=[jax.ShapeDtypeStruct((rows, D_MODEL), F32),
                   jax.ShapeDtypeStruct((N_PLANES, rows, LANES), jnp.int32),
                   jax.ShapeDtypeStruct((rows, LANES), F32)],
        compiler_params=_params("parallel", "parallel"),
        name="outproj",
    )(attn, ssm_tb, x3, *consts)


def _route_body(logit_ref, bias_ref, tri_ref, idx_ref, gate_ref, rank_ref, cnt_ref, carry_ref):
    @pl.when(pl.program_id(0) == 0)
    def _():
        carry_ref[...] = jnp.zeros(carry_ref.shape, F32)

    lg = logit_ref[...]
    tr = lg.shape[0]
    lane = lax.broadcasted_iota(jnp.int32, (tr, LANES), 1).astype(F32)
    score = 1.0 / (1.0 + jnp.exp(-lg))
    sel = jnp.where(lane < N_EXPERTS, score + bias_ref[...], -jnp.inf)
    picked = []
    gates = jnp.zeros((tr, LANES), F32)
    idxs = jnp.zeros((tr, LANES), F32)
    member = jnp.zeros((tr, LANES), F32)
    for k in range(TOP_K):
        best = jnp.max(sel, axis=1, keepdims=True)
        e = jnp.min(jnp.where(sel == best, lane, float(LANES)), axis=1, keepdims=True)
        hit = lane == e
        picked.append(hit)
        gates = jnp.where(lane == k, jnp.sum(jnp.where(hit, score, 0.0), axis=1, keepdims=True), gates)
        idxs = jnp.where(lane == k, e, idxs)
        member = jnp.where(hit, 1.0, member)
        sel = jnp.where(hit, -jnp.inf, sel)
    gate_ref[...] = ROUTE_SCALE * gates / jnp.sum(gates, axis=1, keepdims=True)
    idx_ref[...] = idxs.astype(jnp.int32)
    before = jnp.dot(tri_ref[...], member.astype(BF16), preferred_element_type=F32) + carry_ref[...]
    ranks = jnp.zeros((tr, LANES), F32)
    for k in range(TOP_K):
        ranks = jnp.where(lane == k, jnp.sum(jnp.where(picked[k], before, 0.0), axis=1, keepdims=True), ranks)
    rank_ref[...] = ranks.astype(jnp.int32)
    carry_ref[...] = carry_ref[...] + jnp.sum(member, axis=0, keepdims=True)
    cnt_ref[...] = carry_ref[...]


def _route(logits, bias, tri):
    n_tok = logits.shape[0]
    tr = tri.shape[0]
    tok_map = lambda i: (i, 0)
    const = lambda i: (0, 0)
    return pl.pallas_call(
        _route_body,
        grid=(n_tok // tr,),
        in_specs=[pl.BlockSpec((tr, LANES), tok_map), pl.BlockSpec((1, LANES), const), pl.BlockSpec((tr, tr), const)],
        out_specs=[pl.BlockSpec((tr, LANES), tok_map)] * 3 + [pl.BlockSpec((1, LANES), const)],
        out_shape=[jax.ShapeDtypeStruct((n_tok, LANES), jnp.int32), jax.ShapeDtypeStruct((n_tok, LANES), F32),
                   jax.ShapeDtypeStruct((n_tok, LANES), jnp.int32), jax.ShapeDtypeStruct((1, LANES), F32)],
        scratch_shapes=[pltpu.VMEM((1, LANES), F32)],
        compiler_params=_params("arbitrary"),
        name="route",
    )(logits, bias, tri)


def _gather_rows(table, indices):
    n_idx = indices.shape[0]
    width = table.shape[1]
    mesh = plsc.VectorSubcoreMesh(core_axis_name="core", subcore_axis_name="subcore")

    @pl.kernel(out_type=jax.ShapeDtypeStruct((n_idx, width), table.dtype), mesh=mesh)
    def gather(table_hbm, idx_hbm, out_hbm):
        def body(idx_vmem, out_vmem):
            pltpu.sync_copy(table_hbm.at[idx_vmem.at[0]], out_vmem)

        pltpu.emit_pipeline(
            body,
            grid=(n_idx // SC_WINDOW,),
            in_specs=[pl.BlockSpec((1, SC_WINDOW), lambda i: (0, i))],
            out_specs=[pl.BlockSpec((SC_WINDOW, width), lambda i: (i, 0))],
            core_axis_name=("core", "subcore"),
            dimension_semantics=(pltpu.PARALLEL,),
        )(idx_hbm, out_hbm)

    return gather(table, indices.reshape(1, n_idx))


def _scatter_rows(table, dest, n_out, n_rows):
    n_idx = dest.shape[0]
    width = table.shape[1]
    win_per_plane = n_rows // SC_WINDOW
    win_per_rep_plane = n_idx // SC_WINDOW // (table.shape[0] // n_rows)
    mesh = plsc.VectorSubcoreMesh(core_axis_name="core", subcore_axis_name="subcore")

    @pl.kernel(out_type=jax.ShapeDtypeStruct((n_out, width), table.dtype), mesh=mesh)
    def scatter(table_hbm, idx_hbm, out_hbm):
        def body(rows_vmem, idx_vmem):
            pltpu.sync_copy(rows_vmem, out_hbm.at[idx_vmem.at[0]])

        pltpu.emit_pipeline(
            body,
            grid=(n_idx // SC_WINDOW,),
            in_specs=[pl.BlockSpec((SC_WINDOW, width),
                                   lambda i: ((i // win_per_rep_plane) * win_per_plane + i % win_per_plane, 0)),
                      pl.BlockSpec((1, SC_WINDOW), lambda i: (0, i))],
            out_specs=[],
            core_axis_name=("core", "subcore"),
            dimension_semantics=(pltpu.PARALLEL,),
        )(table_hbm, idx_hbm)

    return scatter(table, dest.reshape(1, n_idx))


def _ffn_body(be_ref, nv_ref, x_ref, wg_ref, wu_ref, wd_ref, y_ref, wgu_s, wd_s):
    r = pl.program_id(0)

    @pl.when((r == 0) | (be_ref[r] != be_ref[jnp.maximum(r - 1, 0)]))
    def _():
        wgu_s[:, :D_EXPERT] = wg_ref[0].astype(BF16)
        wgu_s[:, D_EXPERT:] = wu_ref[0].astype(BF16)
        wd_s[...] = wd_ref[0].astype(BF16)

    n_valid = nv_ref[r]

    @pl.when(n_valid > 0)
    def _():
        live = lax.broadcasted_iota(jnp.int32, (FFN_BLOCK, 1), 0) < n_valid
        x = jnp.where(live, jnp.concatenate(_unpack_rows(x_ref), axis=1), 0.0).astype(BF16)
        gu = jnp.dot(x, wgu_s[...], preferred_element_type=F32)
        gs = gu[:, :D_EXPERT]
        act = (gs / (1.0 + jnp.exp(-gs))) * gu[:, D_EXPERT:]
        _pack_rows(jnp.dot(act.astype(BF16), wd_s[...], preferred_element_type=F32), y_ref)

    @pl.when(n_valid == 0)
    def _():
        y_ref[...] = jnp.zeros(y_ref.shape, jnp.int32)


def _expert_ffn(xs, blk_expert, blk_valid, w_gate, w_up, w_down):
    n_slots = xs.shape[1]
    nb = n_slots // FFN_BLOCK
    slot_spec = pl.BlockSpec((N_PLANES, FFN_BLOCK, LANES), lambda r, be, nv: (0, r, 0))
    return pl.pallas_call(
        _ffn_body,
        grid_spec=pltpu.PrefetchScalarGridSpec(
            num_scalar_prefetch=2, grid=(nb,),
            in_specs=[slot_spec,
                      pl.BlockSpec((1, D_MODEL, D_EXPERT), lambda r, be, nv: (be[r], 0, 0)),
                      pl.BlockSpec((1, D_MODEL, D_EXPERT), lambda r, be, nv: (be[r], 0, 0)),
                      pl.BlockSpec((1, D_EXPERT, D_MODEL), lambda r, be, nv: (be[r], 0, 0))],
            out_specs=slot_spec,
            scratch_shapes=[pltpu.VMEM((D_MODEL, 2 * D_EXPERT), BF16), pltpu.VMEM((D_EXPERT, D_MODEL), BF16)]),
        out_shape=jax.ShapeDtypeStruct((N_PLANES, n_slots, LANES), jnp.int32),
        compiler_params=_params("arbitrary"),
        name="expert_ffn",
    )(blk_expert, blk_valid, xs, w_gate, w_up, w_down)


def _combine_body(xres_ref, gate_ref, *refs):
    y_refs, o_ref = refs[:TOP_K], refs[TOP_K]
    g = gate_ref[...]
    acc = [xres_ref[:, LANES * i:LANES * (i + 1)] for i in range(D_MODEL // LANES)]
    for k in range(TOP_K):
        gk = g[:, k:k + 1]
        acc = [a + gk * p for a, p in zip(acc, _unpack_rows(y_refs[k]))]
    o_ref[...] = jnp.concatenate(acc, axis=1)


def _combine(xres, gates, ysg, row_offset, n_tok):
    rows = xres.shape[0]
    tc = min(COMBINE_TILE, rows)
    off = row_offset // tc
    per_k = n_tok // tc
    y_specs = [pl.BlockSpec((N_PLANES, tc, LANES), lambda i, k=k: (0, k * per_k + off + i, 0)) for k in range(TOP_K)]
    return pl.pallas_call(
        _combine_body,
        grid=(rows // tc,),
        in_specs=[pl.BlockSpec((tc, D_MODEL), lambda i: (i, 0)),
                  pl.BlockSpec((tc, LANES), lambda i: (i + off, 0))] + y_specs,
        out_specs=pl.BlockSpec((tc, D_MODEL), lambda i: (i, 0)),
        out_shape=jax.ShapeDtypeStruct((rows, D_MODEL), F32),
        compiler_params=_params("parallel"),
        name="combine",
    )(xres, gates, *([ysg] * TOP_K))


def _prepare_weights(norm_mix_w, w_in, b_forget, q_norm_w, k_norm_w, lambda_re, lambda_im, log_dt, b_re, b_im,
                     c_re, c_im, d_skip, w_glu, b_glu, attn_out_norm_w, ssm_out_norm_w, w_out, norm_ffn_w,
                     w_router, router_bias, w_gate_e, w_up_e, w_down_e, w_gate_s, w_up_s, w_down_s):
    w = {}
    row = lambda a: a.reshape(1, -1).astype(F32)
    w['norm_mix_w'] = row(norm_mix_w)
    w['wqkv'] = w_in[:, :3 * D_ATTN].astype(BF16)
    w['wf'] = jnp.pad(w_in[:, 3 * D_ATTN:3 * D_ATTN + N_HEADS], ((0, 0), (0, LANES - N_HEADS))).astype(BF16)
    w['wu'] = w_in[:, 3 * D_ATTN + N_HEADS:].astype(BF16)
    w['b_forget'] = jnp.pad(row(b_forget), ((0, 0), (0, LANES - N_HEADS)))
    w['q_norm_w'] = jnp.tile(row(q_norm_w), (1, N_HEADS))
    w['k_norm_w'] = jnp.tile(row(k_norm_w), (1, N_HEADS))
    head = jnp.arange(D_ATTN) // HEAD_DIM
    w['seg'] = jnp.where(head[:, None] == head[None, :], 1.0 / HEAD_DIM, 0.0).astype(BF16)

    dt = jnp.exp(log_dt.astype(F32))[:, None]
    lre, lim = lambda_re.astype(F32), lambda_im.astype(F32)
    a, b = lre * dt, lim * dt
    ea = jnp.exp(a)
    bar_re, bar_im = ea * jnp.cos(b), ea * jnp.sin(b)
    num_re = jnp.expm1(a) * jnp.cos(b) - 2.0 * jnp.sin(0.5 * b) ** 2
    num_im = bar_im
    den = lre * lre + lim * lim
    coef_re = (num_re * lre + num_im * lim) / den
    coef_im = (num_im * lre - num_re * lim) / den
    bb_re = coef_re[:, :, None] * b_re - coef_im[:, :, None] * b_im
    bb_im = coef_re[:, :, None] * b_im + coef_im[:, :, None] * b_re
    eye = jnp.eye(N_SSM_GROUPS, dtype=F32)

    def in_block_diag(m):
        return (m.transpose(0, 2, 1)[:, :, None, :] * eye[:, None, :, None]).reshape(D_SSM, D_STATE)

    def out_block_diag(m):
        return (m.transpose(0, 2, 1)[:, :, None, :] * eye[:, None, :, None]).reshape(D_STATE, D_SSM)

    pairs = jnp.arange(N_SSM_GROUPS // 2)

    def pair_blocks(m):
        return m.reshape(4, LANES, N_SSM_GROUPS // 2, LANES).transpose(2, 0, 1, 3)[pairs, pairs // 4]

    w['ssm_bw'] = jnp.concatenate([pair_blocks(in_block_diag(bb_re)), pair_blocks(in_block_diag(bb_im))],
                                  axis=2).astype(BF16)
    tiles = jnp.arange(D_SSM // LANES)

    def tile_blocks(m):
        return m.reshape(4, 4 * LANES, 4, LANES).transpose(0, 2, 1, 3)[tiles, tiles]

    w['ssm_cre'] = tile_blocks(out_block_diag(c_re.astype(F32))).astype(BF16)
    w['ssm_cim'] = tile_blocks(out_block_diag(-c_im.astype(F32))).astype(BF16)
    w['ssm_ar'] = bar_re.reshape(1, D_STATE)
    w['ssm_ai'] = bar_im.reshape(1, D_STATE)
    w['d_skip'] = row(d_skip)
    w['w_glu'] = w_glu.astype(BF16)
    w['b_glu'] = row(b_glu)

    w['attn_out_norm_w'] = row(attn_out_norm_w)
    w['ssm_out_norm_w'] = row(ssm_out_norm_w)
    w['wo_a'] = w_out[:D_ATTN].astype(BF16)
    w['wo_s'] = w_out[D_ATTN:].astype(BF16)
    w['norm_ffn_w'] = row(norm_ffn_w)
    wr = jnp.pad(w_router.astype(F32), ((0, 0), (0, LANES - N_EXPERTS)))
    w['wr_hi'] = wr.astype(BF16)
    w['wr_lo'] = (wr - w['wr_hi'].astype(F32)).astype(BF16)
    w['router_bias'] = jnp.pad(row(router_bias), ((0, 0), (0, LANES - N_EXPERTS)))
    w['wgu_s'] = jnp.concatenate([w_gate_s, w_up_s], axis=1).astype(BF16)
    w['wd_s'] = w_down_s.astype(BF16)
    w['w_gate_e'], w['w_up_e'], w['w_down_e'] = w_gate_e, w_up_e, w_down_e
    return w


def _mix_prompt(x, w):
    bsz, length, _ = x.shape
    q, k, v, kb, vb, lf, u_tb = _inproj(x, w)
    ct = _cumsum_lanes(lf.reshape(bsz, length, N_HEADS).transpose(0, 2, 1))
    c = ct.transpose(0, 2, 1).reshape(bsz * length, N_HEADS)
    attn = _attn_prompt(q, kb, vb, c, ct, bsz, length)
    h0 = jnp.zeros((bsz, 2 * D_STATE), F32)
    ssm_lb, hlast = _ssm(u_tb, h0, w, bsz, length)
    xres, hn, logits = _outproj(attn, ssm_lb, x, w)
    caches = (k.reshape(1, bsz, length, N_HEADS, HEAD_DIM), v.reshape(1, bsz, length, N_HEADS, HEAD_DIM),
              lf.reshape(1, bsz, length, N_HEADS),
              hlast[:, :D_STATE].reshape(1, bsz, N_SSM_GROUPS, SSM_STATE),
              hlast[:, D_STATE:].reshape(1, bsz, N_SSM_GROUPS, SSM_STATE))
    return xres, hn, logits, caches


def _mix_sample(x, cache_k, cache_v, cache_logf, page_table, h0_re, h0_im, w):
    bsz, n_new, _ = x.shape
    rows = n_new * bsz
    xt = x.transpose(1, 0, 2).reshape(1, rows, D_MODEL)
    q, k, v, _, _, lf, u_tb = _inproj(xt, w)
    to_bm = lambda a: a.reshape(n_new, bsz, -1).transpose(1, 0, 2)
    q_b, k_b, v_b, lf_b = to_bm(q), to_bm(k), to_bm(v), to_bm(lf)
    head = jnp.arange(D_ATTN) // HEAD_DIM
    hmask = (head[None, :] == jnp.arange(N_HEADS)[:, None]).astype(BF16)
    qbd = (q_b[:, None, :, :] * hmask[None, :, None, :]).reshape(bsz, N_HEADS * n_new, D_ATTN)
    pad_keys = ((0, 0), (0, PAGE_SIZE - n_new), (0, 0))
    n_pool = cache_k.shape[1]
    attn = _attn_sample(
        qbd, cache_k[0].reshape(n_pool, PAGE_SIZE, D_ATTN), cache_v[0].reshape(n_pool, PAGE_SIZE, D_ATTN),
        cache_logf[0].astype(F32).transpose(0, 2, 1), page_table.astype(jnp.int32),
        jnp.pad(k_b, pad_keys), jnp.pad(v_b, pad_keys),
        jnp.pad(lf_b.transpose(0, 2, 1), ((0, 0), (0, 0), (0, PAGE_SIZE - n_new))), n_new)
    attn_tm = attn.transpose(1, 0, 2).reshape(rows, D_ATTN)
    h0 = jnp.concatenate([h0_re.reshape(bsz, D_STATE), h0_im.reshape(bsz, D_STATE)], axis=1).astype(F32)
    ssm_lb, hlast = _ssm(u_tb.reshape(n_new, bsz * D_SSM), h0, w, bsz, n_new)
    xres, hn, logits = _outproj(attn_tm, ssm_lb.reshape(rows, D_SSM), xt, w)
    caches = (k_b.reshape(1, bsz, n_new, N_HEADS, HEAD_DIM), v_b.reshape(1, bsz, n_new, N_HEADS, HEAD_DIM),
              lf_b.reshape(1, bsz, n_new, N_HEADS),
              hlast[:, :D_STATE].reshape(1, bsz, N_SSM_GROUPS, SSM_STATE),
              hlast[:, D_STATE:].reshape(1, bsz, N_SSM_GROUPS, SSM_STATE))
    return xres, hn, logits, caches


def _moe(hn, logits, w):
    n_tok = hn.shape[1]
    tri = (jnp.arange(ROUTE_TILE)[:, None] > jnp.arange(ROUTE_TILE)[None, :]).astype(BF16)
    idx, gates, rank, counts = _route(logits, w['router_bias'], tri)
    counts = counts[0, :N_EXPERTS].astype(jnp.int32)
    padded = (counts + FFN_BLOCK - 1) // FFN_BLOCK * FFN_BLOCK
    pend = jnp.cumsum(padded)
    pstart = pend - padded
    n_blk = -(-(n_tok * TOP_K) // FFN_BLOCK) + N_EXPERTS
    n_slots = n_blk * FFN_BLOCK
    pos = (pstart[idx[:, :TOP_K]] + rank[:, :TOP_K]).T.reshape(-1)
    blk_start = jnp.arange(n_blk, dtype=jnp.int32) * FFN_BLOCK
    blk_expert = jnp.minimum(jnp.sum(pend[None, :] <= blk_start[:, None], axis=1), N_EXPERTS - 1).astype(jnp.int32)
    blk_valid = jnp.clip(pstart[blk_expert] + counts[blk_expert] - blk_start, 0, FFN_BLOCK).astype(jnp.int32)
    plane_pos = (pos[None, :] + n_slots * jnp.arange(N_PLANES, dtype=jnp.int32)[:, None]).reshape(-1)
    xs = _scatter_rows(hn.reshape(N_PLANES * n_tok, LANES), plane_pos, N_PLANES * n_slots, n_tok)
    ys = _expert_ffn(xs.reshape(N_PLANES, n_slots, LANES), blk_expert, blk_valid,
                     w['w_gate_e'], w['w_up_e'], w['w_down_e'])
    ysg = _gather_rows(ys.reshape(N_PLANES * n_slots, LANES), plane_pos)
    return gates, ysg.reshape(N_PLANES, TOP_K * n_tok, LANES)


def kernel(x_prompt, x_sample, cache_k, cache_v, cache_logf, page_table, state_ssm_re, state_ssm_im, norm_mix_w, w_in, b_forget, q_norm_w, k_norm_w, lambda_re, lambda_im, log_dt, b_re, b_im, c_re, c_im, d_skip, w_glu, b_glu, attn_out_norm_w, ssm_out_norm_w, w_out, norm_ffn_w, w_router, router_bias, w_gate_e, w_up_e, w_down_e, w_gate_s, w_up_s, w_down_s):
    assert norm_mix_w.shape[0] == 1, "single-layer trunk"
    w = _prepare_weights(norm_mix_w[0], w_in[0], b_forget[0], q_norm_w[0], k_norm_w[0], lambda_re[0], lambda_im[0],
                         log_dt[0], b_re[0], b_im[0], c_re[0], c_im[0], d_skip[0], w_glu[0], b_glu[0],
                         attn_out_norm_w[0], ssm_out_norm_w[0], w_out[0], norm_ffn_w[0], w_router[0], router_bias[0],
                         w_gate_e[0], w_up_e[0], w_down_e[0], w_gate_s[0], w_up_s[0], w_down_s[0])
    bp, lp, _ = x_prompt.shape
    bs, ls, _ = x_sample.shape
    xres_p, hn_p, lg_p, caches_p = _mix_prompt(x_prompt, w)
    xres_s, hn_s, lg_s, caches_s = _mix_sample(x_sample, cache_k, cache_v, cache_logf, page_table,
                                               state_ssm_re[0], state_ssm_im[0], w)
    n_p = bp * lp
    n_tok = n_p + bs * ls
    gates, ysg = _moe(jnp.concatenate([hn_p, hn_s], axis=1), jnp.concatenate([lg_p, lg_s], axis=0), w)
    y_p = _combine(xres_p, gates, ysg, 0, n_tok).reshape(bp, lp, D_MODEL)
    y_s = _combine(xres_s, gates, ysg, n_p, n_tok).reshape(ls, bs, D_MODEL).transpose(1, 0, 2)
    return (y_p, y_s) + caches_p + caches_s
```

```python
import functools
import math

import jax
import jax.numpy as jnp
from jax import lax
from jax.experimental import pallas as pl
from jax.experimental.pallas import tpu as pltpu
from jax.experimental.pallas import tpu_sc as plsc

F32 = jnp.float32
BF16 = jnp.bfloat16

D_MODEL = 1024
D_ATTN = 512
D_SSM = 512
HEAD_DIM = 64
N_HEADS = 8
ATTN_SCALE = HEAD_DIM ** -0.5
LOG2E = math.log2(math.e)
SSM_GROUP = 16
N_SSM_GROUPS = 32
SSM_STATE = 64
D_STATE = N_SSM_GROUPS * SSM_STATE
N_EXPERTS = 64
TOP_K = 8
D_EXPERT = 256
D_SHARED = 256
ROUTE_SCALE = 2.5
PAGE_SIZE = 128
RMS_EPS = 1e-6

LANES = 128
VMEM_LIMIT = 56 * 1024 * 1024

ROW_TILE = 512
ATTN_TILE = 512
PAGES_PER_STEP = 8
ROUTE_TILE = 256
FFN_BLOCK = 512
COMBINE_TILE = 256
SC_WINDOW = 128


def _params(*sem):
    return pltpu.CompilerParams(dimension_semantics=sem, vmem_limit_bytes=VMEM_LIMIT)


def _rms(x, w):
    return x * lax.rsqrt(jnp.mean(x * x, axis=-1, keepdims=True) + RMS_EPS) * w


def _lane_cumsum(x):
    n = x.shape[1]
    lane = lax.broadcasted_iota(jnp.int32, x.shape, 1)
    s = 1
    while s < n:
        x = x + jnp.where(lane >= s, pltpu.roll(x, s, axis=1), 0.0)
        s *= 2
    return x


def _inproj_body(x_ref, nw_ref, wqkv_ref, wf_ref, wu_ref, bf_ref, qnw_ref, knw_ref, seg_ref,
                 q_ref, k_ref, v_ref, kb_ref, vb_ref, lf_ref, u_ref):
    hn = _rms(x_ref[0], nw_ref[...]).astype(BF16)
    qkv = jnp.dot(hn, wqkv_ref[...], preferred_element_type=F32)
    seg = seg_ref[...]

    def head_norm(t, w):
        t2 = t * t
        hi = t2.astype(BF16)
        lo = (t2 - hi.astype(F32)).astype(BF16)
        ms = jnp.dot(hi, seg, preferred_element_type=F32) + jnp.dot(lo, seg, preferred_element_type=F32)
        return t * lax.rsqrt(ms + RMS_EPS) * w

    q = head_norm(qkv[:, :D_ATTN], qnw_ref[...])
    k = head_norm(qkv[:, D_ATTN:2 * D_ATTN], knw_ref[...])
    v = qkv[:, 2 * D_ATTN:]
    q_ref[...] = (q * (ATTN_SCALE * LOG2E)).astype(BF16)
    k_ref[0] = k.T
    v_ref[0] = v.T
    kb_ref[...] = k.astype(BF16)
    vb_ref[...] = v.astype(BF16)
    z = jnp.dot(hn, wf_ref[...], preferred_element_type=F32) + bf_ref[...]
    lf = jnp.minimum(z, 0.0) - jnp.log(1.0 + jnp.exp(-jnp.abs(z)))
    lf_ref[...] = lf[:, :N_HEADS]
    u_ref[...] = jnp.dot(hn, wu_ref[...], preferred_element_type=F32)


def _inproj(x3, w):
    bsz, length, _ = x3.shape
    tm = min(ROW_TILE, length)
    nt = length // tm
    rows = bsz * length
    row_map = lambda b, t: (b * nt + t, 0)
    const = lambda b, t: (0, 0)
    full = lambda a: pl.BlockSpec(a.shape, const)
    consts = [w['norm_mix_w'], w['wqkv'], w['wf'], w['wu'], w['b_forget'], w['q_norm_w'], w['k_norm_w'], w['seg']]
    return pl.pallas_call(
        _inproj_body,
        grid=(bsz, nt),
        in_specs=[pl.BlockSpec((1, tm, D_MODEL), lambda b, t: (b, t, 0))] + [full(a) for a in consts],
        out_specs=[pl.BlockSpec((tm, D_ATTN), row_map)]
        + [pl.BlockSpec((1, D_ATTN, tm), lambda b, t: (b, 0, t))] * 2
        + [pl.BlockSpec((tm, D_ATTN), row_map)] * 2
        + [pl.BlockSpec((tm, N_HEADS), row_map), pl.BlockSpec((tm, D_SSM), lambda b, t: (t, b))],
        out_shape=[jax.ShapeDtypeStruct((rows, D_ATTN), BF16),
                   jax.ShapeDtypeStruct((bsz, D_ATTN, length), F32),
                   jax.ShapeDtypeStruct((bsz, D_ATTN, length), F32),
                   jax.ShapeDtypeStruct((rows, D_ATTN), BF16),
                   jax.ShapeDtypeStruct((rows, D_ATTN), BF16),
                   jax.ShapeDtypeStruct((rows, N_HEADS), F32),
                   jax.ShapeDtypeStruct((length, bsz * D_SSM), F32)],
        compiler_params=_params("parallel", "parallel"),
        name="inproj",
    )(x3, *consts)


def _cumsum_body(lf_ref, c_ref):
    c_ref[0] = _lane_cumsum(lf_ref[0])


def _cumsum_lanes(lft):
    bsz, nh, length = lft.shape
    spec = pl.BlockSpec((1, nh, length), lambda b: (b, 0, 0))
    return pl.pallas_call(
        _cumsum_body, grid=(bsz,), in_specs=[spec], out_specs=spec,
        out_shape=jax.ShapeDtypeStruct(lft.shape, F32),
        compiler_params=_params("parallel"), name="logf_cumsum",
    )(lft)


def _attn_prompt_body(q_ref, k_ref, v_ref, c_ref, ct_ref, o_ref, acc_ref, m_ref, l_ref, *, tile):
    i = pl.program_id(1)
    lane = lax.broadcasted_iota(jnp.int32, (1, LANES), 1)
    row = lax.broadcasted_iota(jnp.int32, (tile, tile), 0)
    col = lax.broadcasted_iota(jnp.int32, (tile, tile), 1)
    hmasks = (lane < HEAD_DIM, lane >= HEAD_DIM)
    for hp in range(N_HEADS // 2):
        lanes = slice(LANES * hp, LANES * (hp + 1))
        qp = q_ref[:, lanes]
        qh = [jnp.where(hm, qp, jnp.zeros_like(qp)) for hm in hmasks]
        cq = [c_ref[:, 2 * hp + hh:2 * hp + hh + 1] * LOG2E for hh in range(2)]
        m_ref[...] = jnp.full(m_ref.shape, -jnp.inf, F32)
        l_ref[...] = jnp.zeros(l_ref.shape, F32)
        acc_ref[...] = jnp.zeros(acc_ref.shape, F32)

        def kstep(j, diagonal, qh=qh, cq=cq, lanes=lanes, hp=hp):
            r0 = pl.multiple_of(j * tile, tile)
            kj = k_ref[pl.ds(r0, tile), lanes]
            vj = v_ref[pl.ds(r0, tile), lanes]
            for hh in range(2):
                s = lax.dot_general(qh[hh], kj, (((1,), (1,)), ((), ())), preferred_element_type=F32)
                s = s + (cq[hh] - ct_ref[0, 2 * hp + hh:2 * hp + hh + 1, pl.ds(r0, tile)] * LOG2E)
                if diagonal:
                    s = jnp.where(row >= col, s, -jnp.inf)
                m_old = m_ref[hh]
                m_new = jnp.maximum(m_old, jnp.max(s, axis=1, keepdims=True))
                alpha = jnp.exp2(m_old - m_new)
                p = jnp.exp2(s - m_new)
                l_ref[hh] = alpha * l_ref[hh] + jnp.sum(p, axis=1, keepdims=True)
                vh = jnp.where(hmasks[hh], vj, jnp.zeros_like(vj))
                acc_ref[hh] = alpha * acc_ref[hh] + jnp.dot(p.astype(BF16), vh, preferred_element_type=F32)
                m_ref[hh] = m_new

        def body(j, carry):
            kstep(j, False)
            return carry

        lax.fori_loop(0, i, body, 0)
        kstep(i, True)
        o_ref[:, lanes] = acc_ref[0] / l_ref[0] + acc_ref[1] / l_ref[1]


def _attn_prompt(q, kb, vb, c, ct, bsz, length):
    tile = min(ATTN_TILE, length)
    nq = length // tile
    q_map = lambda b, i: (b * nq + i, 0)
    seq_map = lambda b, i: (b, 0)
    return pl.pallas_call(
        functools.partial(_attn_prompt_body, tile=tile),
        grid=(bsz, nq),
        in_specs=[pl.BlockSpec((tile, D_ATTN), q_map),
                  pl.BlockSpec((length, D_ATTN), seq_map),
                  pl.BlockSpec((length, D_ATTN), seq_map),
                  pl.BlockSpec((tile, N_HEADS), q_map),
                  pl.BlockSpec((1, N_HEADS, length), lambda b, i: (b, 0, 0))],
        out_specs=pl.BlockSpec((tile, D_ATTN), q_map),
        out_shape=jax.ShapeDtypeStruct((bsz * length, D_ATTN), F32),
        scratch_shapes=[pltpu.VMEM((2, tile, LANES), F32), pltpu.VMEM((2, tile, 1), F32),
                        pltpu.VMEM((2, tile, 1), F32)],
        compiler_params=_params("parallel", "parallel"),
        name="attn_prompt",
    )(q, kb, vb, c, ct)


def _attn_sample_body(pt_ref, q_ref, *refs, pages, n_new):
    del pt_ref
    k_refs, v_refs, lf_refs = refs[:pages], refs[pages:2 * pages], refs[2 * pages:3 * pages]
    kn_ref, vn_ref, lfn_ref, o_ref, m_ref, l_ref, acc_ref, c_ref = refs[3 * pages:]
    j = pl.program_id(1)
    rows = N_HEADS * n_new

    @pl.when(j == 0)
    def _():
        m_ref[...] = jnp.full(m_ref.shape, -jnp.inf, F32)
        l_ref[...] = jnp.zeros(l_ref.shape, F32)
        acc_ref[...] = jnp.zeros(acc_ref.shape, F32)
        c_ref[...] = jnp.zeros(c_ref.shape, F32)

    q = q_ref[0]

    def chunk(kc, vc, lft, valid):
        n = kc.shape[1]
        s = jnp.dot(q, kc, preferred_element_type=F32)
        ck = _lane_cumsum(lft) + c_ref[:, 0:1]
        c_ref[...] = jnp.broadcast_to(ck[:, n - 1:n], c_ref.shape)
        ck2 = ck * LOG2E
        s = s - jnp.concatenate([jnp.broadcast_to(ck2[h:h + 1], (n_new, n)) for h in range(N_HEADS)], axis=0)
        if valid is not None:
            s = jnp.where(valid, s, -jnp.inf)
        m_old = m_ref[...]
        m_new = jnp.maximum(m_old, jnp.max(s, axis=1, keepdims=True))
        alpha = jnp.exp2(m_old - m_new)
        p = jnp.exp2(s - m_new)
        l_ref[...] = alpha * l_ref[...] + jnp.sum(p, axis=1, keepdims=True)
        acc_ref[...] = alpha * acc_ref[...] + lax.dot_general(
            p.astype(BF16), vc, (((1,), (1,)), ((), ())), preferred_element_type=F32)
        m_ref[...] = m_new

    chunk(jnp.concatenate([r[0] for r in k_refs], axis=1).astype(BF16),
          jnp.concatenate([r[0] for r in v_refs], axis=1).astype(BF16),
          jnp.concatenate([r[0] for r in lf_refs], axis=1), None)

    @pl.when(j == pl.num_programs(1) - 1)
    def _():
        qi = lax.broadcasted_iota(jnp.int32, (rows, PAGE_SIZE), 0) % n_new
        kj = lax.broadcasted_iota(jnp.int32, (rows, PAGE_SIZE), 1)
        chunk(kn_ref[0].astype(BF16), vn_ref[0].astype(BF16), lfn_ref[0], kj <= qi)
        o_full = acc_ref[...] / l_ref[...]
        lane = lax.broadcasted_iota(jnp.int32, (1, D_ATTN), 1)
        out = jnp.zeros((n_new, D_ATTN), F32)
        for h in range(N_HEADS):
            hmask = (lane >= HEAD_DIM * h) & (lane < HEAD_DIM * (h + 1))
            out = out + jnp.where(hmask, o_full[n_new * h:n_new * (h + 1)], 0.0)
        o_ref[0] = out


def _attn_sample(qbd, cache_k, cache_v, cache_lft, page_table, kn, vn, lfn, n_new):
    bsz, n_pages = page_table.shape
    pages = min(PAGES_PER_STEP, n_pages)
    nj = n_pages // pages
    rows = N_HEADS * n_new

    def page_map(p):
        return lambda b, j, pt: (pt[b * n_pages + j * pages + p], 0, 0)

    seq_map = lambda b, j, pt: (b, 0, 0)
    in_specs = [pl.BlockSpec((1, rows, D_ATTN), seq_map)]
    in_specs += [pl.BlockSpec((1, D_ATTN, PAGE_SIZE), page_map(p)) for p in range(pages)]
    in_specs += [pl.BlockSpec((1, D_ATTN, PAGE_SIZE), page_map(p)) for p in range(pages)]
    in_specs += [pl.BlockSpec((1, N_HEADS, PAGE_SIZE), page_map(p)) for p in range(pages)]
    in_specs += [pl.BlockSpec((1, D_ATTN, PAGE_SIZE), seq_map)] * 2 + [pl.BlockSpec((1, N_HEADS, PAGE_SIZE), seq_map)]
    return pl.pallas_call(
        functools.partial(_attn_sample_body, pages=pages, n_new=n_new),
        grid_spec=pltpu.PrefetchScalarGridSpec(
            num_scalar_prefetch=1, grid=(bsz, nj), in_specs=in_specs,
            out_specs=pl.BlockSpec((1, n_new, D_ATTN), seq_map),
            scratch_shapes=[pltpu.VMEM((rows, 1), F32), pltpu.VMEM((rows, 1), F32),
                            pltpu.VMEM((rows, D_ATTN), F32), pltpu.VMEM((N_HEADS, LANES), F32)]),
        out_shape=jax.ShapeDtypeStruct((bsz, n_new, D_ATTN), F32),
        compiler_params=_params("parallel", "arbitrary"),
        name="attn_sample",
    )(page_table.reshape(-1), qbd, *([cache_k] * pages), *([cache_v] * pages), *([cache_lft] * pages), kn, vn, lfn)


SCAN_LANES = 512


def _ssm_body(u_ref, h0_ref, bw_ref, cre_ref, cim_ref, ar_ref, ai_ref, dsk_ref, wglu_ref, bglu_ref,
              out_ref, hlast_ref, hbuf_ref, state_ref, ubuf_ref, *, steps, bsz):
    g = pl.program_id(0)

    @pl.when(g == 0)
    def _():
        state_ref[...] = h0_ref[...]

    n_tiles = D_SSM // LANES
    for b in range(bsz):
        for t in range(n_tiles):
            col = D_SSM * b + LANES * t
            ubuf_ref[t, pl.ds(b, steps, stride=bsz), :] = u_ref[:, col:col + LANES]
    u = jnp.concatenate([ubuf_ref[t] for t in range(n_tiles)], axis=1)
    ub = u.astype(BF16)
    for p in range(N_SSM_GROUPS // 2):
        t = p // 4
        bu = jnp.dot(ub[:, LANES * t:LANES * (t + 1)], bw_ref[p], preferred_element_type=F32)
        hbuf_ref[:, LANES * p:LANES * (p + 1)] = bu[:, :LANES]
        hbuf_ref[:, D_STATE + LANES * p:D_STATE + LANES * (p + 1)] = bu[:, LANES:]

    for c in range(D_STATE // SCAN_LANES):
        re_l = slice(SCAN_LANES * c, SCAN_LANES * (c + 1))
        im_l = slice(D_STATE + SCAN_LANES * c, D_STATE + SCAN_LANES * (c + 1))
        ar = jnp.broadcast_to(ar_ref[:, re_l], (bsz, SCAN_LANES))
        ai = jnp.broadcast_to(ai_ref[:, re_l], (bsz, SCAN_LANES))

        def step(t, carry, re_l=re_l, im_l=im_l, ar=ar, ai=ai):
            re, im = carry
            r0 = pl.multiple_of(t * bsz, bsz)
            nre = ar * re - ai * im + hbuf_ref[pl.ds(r0, bsz), re_l]
            nim = ar * im + ai * re + hbuf_ref[pl.ds(r0, bsz), im_l]
            hbuf_ref[pl.ds(r0, bsz), re_l] = nre
            hbuf_ref[pl.ds(r0, bsz), im_l] = nim
            return nre, nim

        re, im = lax.fori_loop(0, steps, step, (state_ref[:, re_l], state_ref[:, im_l]))
        state_ref[:, re_l] = re
        state_ref[:, im_l] = im

    hlast_ref[...] = state_ref[...]

    ys = []
    for t in range(D_SSM // LANES):
        w = 4 * LANES
        hre = hbuf_ref[:, w * t:w * (t + 1)].astype(BF16)
        him = hbuf_ref[:, D_STATE + w * t:D_STATE + w * (t + 1)].astype(BF16)
        ys.append(jnp.dot(hre, cre_ref[t], preferred_element_type=F32)
                  + jnp.dot(him, cim_ref[t], preferred_element_type=F32))
    y = jnp.concatenate(ys, axis=1) + dsk_ref[...] * u
    z = 0.5 * y * (1.0 + jnp.tanh(math.sqrt(2.0 / math.pi) * (y + 0.044715 * (y * y * y))))
    gate = jnp.dot(z.astype(BF16), wglu_ref[...], preferred_element_type=F32) + bglu_ref[...]
    out = z / (1.0 + jnp.exp(-gate))
    for t in range(n_tiles):
        ubuf_ref[t] = out[:, LANES * t:LANES * (t + 1)]
    for b in range(bsz):
        for t in range(n_tiles):
            col = D_SSM * b + LANES * t
            out_ref[:, col:col + LANES] = ubuf_ref[t, pl.ds(b, steps, stride=bsz), :]


def _ssm(u_lb, h0, w, bsz, length):
    steps = max(1, min(length, ROW_TILE // bsz))
    rows = steps * bsz
    const2 = lambda g: (0, 0)
    const3 = lambda g: (0, 0, 0)
    full = lambda a: pl.BlockSpec(a.shape, const2 if a.ndim == 2 else const3)
    consts = [h0, w['ssm_bw'], w['ssm_cre'], w['ssm_cim'], w['ssm_ar'], w['ssm_ai'], w['d_skip'], w['w_glu'], w['b_glu']]
    return pl.pallas_call(
        functools.partial(_ssm_body, steps=steps, bsz=bsz),
        grid=(length // steps,),
        in_specs=[pl.BlockSpec((steps, bsz * D_SSM), lambda g: (g, 0))] + [full(a) for a in consts],
        out_specs=[pl.BlockSpec((steps, bsz * D_SSM), lambda g: (g, 0)), pl.BlockSpec((bsz, 2 * D_STATE), const2)],
        out_shape=[jax.ShapeDtypeStruct((length, bsz * D_SSM), F32), jax.ShapeDtypeStruct((bsz, 2 * D_STATE), F32)],
        scratch_shapes=[pltpu.VMEM((rows, 2 * D_STATE), F32), pltpu.VMEM((bsz, 2 * D_STATE), F32),
                        pltpu.VMEM((D_SSM // LANES, rows, LANES), F32)],
        compiler_params=_params("arbitrary"),
        name="ssm",
    )(u_lb, *consts)


N_PLANES = D_MODEL // (2 * LANES)


def _pack_rows(x, out_ref):
    bits = lax.bitcast_convert_type(x.astype(BF16).astype(F32), jnp.uint32)
    for c in range(N_PLANES):
        lo = bits[:, 2 * LANES * c:2 * LANES * c + LANES]
        hi = bits[:, 2 * LANES * c + LANES:2 * LANES * (c + 1)]
        out_ref[c] = lax.bitcast_convert_type(hi | (lo >> 16), jnp.int32)


def _unpack_rows(ref):
    parts = []
    for c in range(N_PLANES):
        bits = lax.bitcast_convert_type(ref[c], jnp.uint32)
        parts.append(lax.bitcast_convert_type(bits << 16, F32))
        parts.append(lax.bitcast_convert_type(bits & jnp.uint32(0xFFFF0000), F32))
    return parts


def _outproj_body(attn_ref, ssm_ref, x_ref, aw_ref, sw_ref, woa_ref, wos_ref, nfw_ref, wrh_ref, wrl_ref,
                  wgus_ref, wds_ref, xres_ref, hn_ref, logit_ref):
    an = _rms(attn_ref[...], aw_ref[...]).astype(BF16)
    sn = _rms(ssm_ref[...], sw_ref[...]).astype(BF16)
    x1 = x_ref[0] + (jnp.dot(an, woa_ref[...], preferred_element_type=F32)
                     + jnp.dot(sn, wos_ref[...], preferred_element_type=F32))
    h2 = _rms(x1, nfw_ref[...])
    hb = h2.astype(BF16)
    hlo = (h2 - hb.astype(F32)).astype(BF16)
    _pack_rows(h2, hn_ref)
    logit_ref[...] = (jnp.dot(hb, wrh_ref[...], preferred_element_type=F32)
                      + (jnp.dot(hb, wrl_ref[...], preferred_element_type=F32)
                         + jnp.dot(hlo, wrh_ref[...], preferred_element_type=F32)))
    gu = jnp.dot(hb, wgus_ref[...], preferred_element_type=F32)
    gs = gu[:, :D_SHARED]
    act = (gs / (1.0 + jnp.exp(-gs))) * gu[:, D_SHARED:]
    xres_ref[...] = x1 + jnp.dot(act.astype(BF16), wds_ref[...], preferred_element_type=F32)


def _outproj(attn, ssm_tb, x3, w):
    bsz, length, _ = x3.shape
    tm = min(ROW_TILE, length)
    nt = length // tm
    rows = bsz * length
    row_map = lambda b, t: (b * nt + t, 0)
    const = lambda b, t: (0, 0)
    full = lambda a: pl.BlockSpec(a.shape, const)
    consts = [w['attn_out_norm_w'], w['ssm_out_norm_w'], w['wo_a'], w['wo_s'], w['norm_ffn_w'],
              w['wr_hi'], w['wr_lo'], w['wgu_s'], w['wd_s']]
    return pl.pallas_call(
        _outproj_body,
        grid=(bsz, nt),
        in_specs=[pl.BlockSpec((tm, D_ATTN), row_map), pl.BlockSpec((tm, D_SSM), lambda b, t: (t, b)),
                  pl.BlockSpec((1, tm, D_MODEL), lambda b, t: (b, t, 0))] + [full(a) for a in consts],
        out_specs=[pl.BlockSpec((tm, D_MODEL), row_map),
                   pl.BlockSpec((N_PLANES, tm, LANES), lambda b, t: (0, b * nt + t, 0)),
                   pl.BlockSpec((tm, LANES), row_map)],
        out_shape=[jax.ShapeDtypeStruct((rows, D_MODEL), F32),
                   jax.ShapeDtypeStruct((N_PLANES, rows, LANES), jnp.int32),
                   jax.ShapeDtypeStruct((rows, LANES), F32)],
        compiler_params=_params("parallel", "parallel"),
        name="outproj",
    )(attn, ssm_tb, x3, *consts)


def _route_body(logit_ref, bias_ref, tri_ref, idx_ref, gate_ref, rank_ref, cnt_ref, carry_ref):
    @pl.when(pl.program_id(0) == 0)
    def _():
        carry_ref[...] = jnp.zeros(carry_ref.shape, F32)

    lg = logit_ref[...]
    tr = lg.shape[0]
    lane = lax.broadcasted_iota(jnp.int32, (tr, LANES), 1).astype(F32)
    score = 1.0 / (1.0 + jnp.exp(-lg))
    sel = jnp.where(lane < N_EXPERTS, score + bias_ref[...], -jnp.inf)
    picked = []
    gates = jnp.zeros((tr, LANES), F32)
    idxs = jnp.zeros((tr, LANES), F32)
    member = jnp.zeros((tr, LANES), F32)
    for k in range(TOP_K):
        best = jnp.max(sel, axis=1, keepdims=True)
        e = jnp.min(jnp.where(sel == best, lane, float(LANES)), axis=1, keepdims=True)
        hit = lane == e
        picked.append(hit)
        gates = jnp.where(lane == k, jnp.sum(jnp.where(hit, score, 0.0), axis=1, keepdims=True), gates)
        idxs = jnp.where(lane == k, e, idxs)
        member = jnp.where(hit, 1.0, member)
        sel = jnp.where(hit, -jnp.inf, sel)
    gate_ref[...] = ROUTE_SCALE * gates / jnp.sum(gates, axis=1, keepdims=True)
    idx_ref[...] = idxs.astype(jnp.int32)
    before = jnp.dot(tri_ref[...], member.astype(BF16), preferred_element_type=F32) + carry_ref[...]
    ranks = jnp.zeros((tr, LANES), F32)
    for k in range(TOP_K):
        ranks = jnp.where(lane == k, jnp.sum(jnp.where(picked[k], before, 0.0), axis=1, keepdims=True), ranks)
    rank_ref[...] = ranks.astype(jnp.int32)
    carry_ref[...] = carry_ref[...] + jnp.sum(member, axis=0, keepdims=True)
    cnt_ref[...] = carry_ref[...]


def _route(logits, bias, tri):
    n_tok = logits.shape[0]
    tr = tri.shape[0]
    tok_map = lambda i: (i, 0)
    const = lambda i: (0, 0)
    return pl.pallas_call(
        _route_body,
        grid=(n_tok // tr,),
        in_specs=[pl.BlockSpec((tr, LANES), tok_map), pl.BlockSpec((1, LANES), const), pl.BlockSpec((tr, tr), const)],
        out_specs=[pl.BlockSpec((tr, LANES), tok_map)] * 3 + [pl.BlockSpec((1, LANES), const)],
        out_shape=[jax.ShapeDtypeStruct((n_tok, LANES), jnp.int32), jax.ShapeDtypeStruct((n_tok, LANES), F32),
                   jax.ShapeDtypeStruct((n_tok, LANES), jnp.int32), jax.ShapeDtypeStruct((1, LANES), F32)],
        scratch_shapes=[pltpu.VMEM((1, LANES), F32)],
        compiler_params=_params("arbitrary"),
        name="route",
    )(logits, bias, tri)


def _gather_rows(table, indices):
    n_idx = indices.shape[0]
    width = table.shape[1]
    mesh = plsc.VectorSubcoreMesh(core_axis_name="core", subcore_axis_name="subcore")

    @pl.kernel(out_type=jax.ShapeDtypeStruct((n_idx, width), table.dtype), mesh=mesh)
    def gather(table_hbm, idx_hbm, out_hbm):
        def body(idx_vmem, out_vmem):
            pltpu.sync_copy(table_hbm.at[idx_vmem.at[0]], out_vmem)

        pltpu.emit_pipeline(
            body,
            grid=(n_idx // SC_WINDOW,),
            in_specs=[pl.BlockSpec((1, SC_WINDOW), lambda i: (0, i))],
            out_specs=[pl.BlockSpec((SC_WINDOW, width), lambda i: (i, 0))],
            core_axis_name=("core", "subcore"),
            dimension_semantics=(pltpu.PARALLEL,),
        )(idx_hbm, out_hbm)

    return gather(table, indices.reshape(1, n_idx))


def _scatter_rows(table, dest, n_out, n_rows):
    n_idx = dest.shape[0]
    width = table.shape[1]
    win_per_plane = n_rows // SC_WINDOW
    win_per_rep_plane = n_idx // SC_WINDOW // (table.shape[0] // n_rows)
    mesh = plsc.VectorSubcoreMesh(core_axis_name="core", subcore_axis_name="subcore")

    @pl.kernel(out_type=jax.ShapeDtypeStruct((n_out, width), table.dtype), mesh=mesh)
    def scatter(table_hbm, idx_hbm, out_hbm):
        def body(rows_vmem, idx_vmem):
            pltpu.sync_copy(rows_vmem, out_hbm.at[idx_vmem.at[0]])

        pltpu.emit_pipeline(
            body,
            grid=(n_idx // SC_WINDOW,),
            in_specs=[pl.BlockSpec((SC_WINDOW, width),
                                   lambda i: ((i // win_per_rep_plane) * win_per_plane + i % win_per_plane, 0)),
                      pl.BlockSpec((1, SC_WINDOW), lambda i: (0, i))],
            out_specs=[],
            core_axis_name=("core", "subcore"),
            dimension_semantics=(pltpu.PARALLEL,),
        )(table_hbm, idx_hbm)

    return scatter(table, dest.reshape(1, n_idx))


def _ffn_body(be_ref, nv_ref, x_ref, wg_ref, wu_ref, wd_ref, y_ref, wgu_s, wd_s):
    r = pl.program_id(0)

    @pl.when((r == 0) | (be_ref[r] != be_ref[jnp.maximum(r - 1, 0)]))
    def _():
        wgu_s[:, :D_EXPERT] = wg_ref[0].astype(BF16)
        wgu_s[:, D_EXPERT:] = wu_ref[0].astype(BF16)
        wd_s[...] = wd_ref[0].astype(BF16)

    n_valid = nv_ref[r]

    @pl.when(n_valid > 0)
    def _():
        live = lax.broadcasted_iota(jnp.int32, (FFN_BLOCK, 1), 0) < n_valid
        x = jnp.where(live, jnp.concatenate(_unpack_rows(x_ref), axis=1), 0.0).astype(BF16)
        gu = jnp.dot(x, wgu_s[...], preferred_element_type=F32)
        gs = gu[:, :D_EXPERT]
        act = (gs / (1.0 + jnp.exp(-gs))) * gu[:, D_EXPERT:]
        _pack_rows(jnp.dot(act.astype(BF16), wd_s[...], preferred_element_type=F32), y_ref)

    @pl.when(n_valid == 0)
    def _():
        y_ref[...] = jnp.zeros(y_ref.shape, jnp.int32)


def _expert_ffn(xs, blk_expert, blk_valid, w_gate, w_up, w_down):
    n_slots = xs.shape[1]
    nb = n_slots // FFN_BLOCK
    slot_spec = pl.BlockSpec((N_PLANES, FFN_BLOCK, LANES), lambda r, be, nv: (0, r, 0))
    return pl.pallas_call(
        _ffn_body,
        grid_spec=pltpu.PrefetchScalarGridSpec(
            num_scalar_prefetch=2, grid=(nb,),
            in_specs=[slot_spec,
                      pl.BlockSpec((1, D_MODEL, D_EXPERT), lambda r, be, nv: (be[r], 0, 0)),
                      pl.BlockSpec((1, D_MODEL, D_EXPERT), lambda r, be, nv: (be[r], 0, 0)),
                      pl.BlockSpec((1, D_EXPERT, D_MODEL), lambda r, be, nv: (be[r], 0, 0))],
            out_specs=slot_spec,
            scratch_shapes=[pltpu.VMEM((D_MODEL, 2 * D_EXPERT), BF16), pltpu.VMEM((D_EXPERT, D_MODEL), BF16)]),
        out_shape=jax.ShapeDtypeStruct((N_PLANES, n_slots, LANES), jnp.int32),
        compiler_params=_params("arbitrary"),
        name="expert_ffn",
    )(blk_expert, blk_valid, xs, w_gate, w_up, w_down)


def _combine_body(xres_ref, gate_ref, *refs):
    y_refs, o_ref = refs[:TOP_K], refs[TOP_K]
    g = gate_ref[...]
    acc = [xres_ref[:, LANES * i:LANES * (i + 1)] for i in range(D_MODEL // LANES)]
    for k in range(TOP_K):
        gk = g[:, k:k + 1]
        acc = [a + gk * p for a, p in zip(acc, _unpack_rows(y_refs[k]))]
    o_ref[...] = jnp.concatenate(acc, axis=1)


def _combine(xres, gates, ysg, row_offset, n_tok):
    rows = xres.shape[0]
    tc = min(COMBINE_TILE, rows)
    off = row_offset // tc
    per_k = n_tok // tc
    y_specs = [pl.BlockSpec((N_PLANES, tc, LANES), lambda i, k=k: (0, k * per_k + off + i, 0)) for k in range(TOP_K)]
    return pl.pallas_call(
        _combine_body,
        grid=(rows // tc,),
        in_specs=[pl.BlockSpec((tc, D_MODEL), lambda i: (i, 0)),
                  pl.BlockSpec((tc, LANES), lambda i: (i + off, 0))] + y_specs,
        out_specs=pl.BlockSpec((tc, D_MODEL), lambda i: (i, 0)),
        out_shape=jax.ShapeDtypeStruct((rows, D_MODEL), F32),
        compiler_params=_params("parallel"),
        name="combine",
    )(xres, gates, *([ysg] * TOP_K))


def _prepare_weights(norm_mix_w, w_in, b_forget, q_norm_w, k_norm_w, lambda_re, lambda_im, log_dt, b_re, b_im,
                     c_re, c_im, d_skip, w_glu, b_glu, attn_out_norm_w, ssm_out_norm_w, w_out, norm_ffn_w,
                     w_router, router_bias, w_gate_e, w_up_e, w_down_e, w_gate_s, w_up_s, w_down_s):
    w = {}
    row = lambda a: a.reshape(1, -1).astype(F32)
    w['norm_mix_w'] = row(norm_mix_w)
    w['wqkv'] = w_in[:, :3 * D_ATTN].astype(BF16)
    w['wf'] = jnp.pad(w_in[:, 3 * D_ATTN:3 * D_ATTN + N_HEADS], ((0, 0), (0, LANES - N_HEADS))).astype(BF16)
    w['wu'] = w_in[:, 3 * D_ATTN + N_HEADS:].astype(BF16)
    w['b_forget'] = jnp.pad(row(b_forget), ((0, 0), (0, LANES - N_HEADS)))
    w['q_norm_w'] = jnp.tile(row(q_norm_w), (1, N_HEADS))
    w['k_norm_w'] = jnp.tile(row(k_norm_w), (1, N_HEADS))
    head = jnp.arange(D_ATTN) // HEAD_DIM
    w['seg'] = jnp.where(head[:, None] == head[None, :], 1.0 / HEAD_DIM, 0.0).astype(BF16)

    dt = jnp.exp(log_dt.astype(F32))[:, None]
    lre, lim = lambda_re.astype(F32), lambda_im.astype(F32)
    a, b = lre * dt, lim * dt
    ea = jnp.exp(a)
    bar_re, bar_im = ea * jnp.cos(b), ea * jnp.sin(b)
    num_re = jnp.expm1(a) * jnp.cos(b) - 2.0 * jnp.sin(0.5 * b) ** 2
    num_im = bar_im
    den = lre * lre + lim * lim
    coef_re = (num_re * lre + num_im * lim) / den
    coef_im = (num_im * lre - num_re * lim) / den
    bb_re = coef_re[:, :, None] * b_re - coef_im[:, :, None] * b_im
    bb_im = coef_re[:, :, None] * b_im + coef_im[:, :, None] * b_re
    eye = jnp.eye(N_SSM_GROUPS, dtype=F32)

    def in_block_diag(m):
        return (m.transpose(0, 2, 1)[:, :, None, :] * eye[:, None, :, None]).reshape(D_SSM, D_STATE)

    def out_block_diag(m):
        return (m.transpose(0, 2, 1)[:, :, None, :] * eye[:, None, :, None]).reshape(D_STATE, D_SSM)

    pairs = jnp.arange(N_SSM_GROUPS // 2)

    def pair_blocks(m):
        return m.reshape(4, LANES, N_SSM_GROUPS // 2, LANES).transpose(2, 0, 1, 3)[pairs, pairs // 4]

    w['ssm_bw'] = jnp.concatenate([pair_blocks(in_block_diag(bb_re)), pair_blocks(in_block_diag(bb_im))],
                                  axis=2).astype(BF16)
    tiles = jnp.arange(D_SSM // LANES)

    def tile_blocks(m):
        return m.reshape(4, 4 * LANES, 4, LANES).transpose(0, 2, 1, 3)[tiles, tiles]

    w['ssm_cre'] = tile_blocks(out_block_diag(c_re.astype(F32))).astype(BF16)
    w['ssm_cim'] = tile_blocks(out_block_diag(-c_im.astype(F32))).astype(BF16)
    w['ssm_ar'] = bar_re.reshape(1, D_STATE)
    w['ssm_ai'] = bar_im.reshape(1, D_STATE)
    w['d_skip'] = row(d_skip)
    w['w_glu'] = w_glu.astype(BF16)
    w['b_glu'] = row(b_glu)

    w['attn_out_norm_w'] = row(attn_out_norm_w)
    w['ssm_out_norm_w'] = row(ssm_out_norm_w)
    w['wo_a'] = w_out[:D_ATTN].astype(BF16)
    w['wo_s'] = w_out[D_ATTN:].astype(BF16)
    w['norm_ffn_w'] = row(norm_ffn_w)
    wr = jnp.pad(w_router.astype(F32), ((0, 0), (0, LANES - N_EXPERTS)))
    w['wr_hi'] = wr.astype(BF16)
    w['wr_lo'] = (wr - w['wr_hi'].astype(F32)).astype(BF16)
    w['router_bias'] = jnp.pad(row(router_bias), ((0, 0), (0, LANES - N_EXPERTS)))
    w['wgu_s'] = jnp.concatenate([w_gate_s, w_up_s], axis=1).astype(BF16)
    w['wd_s'] = w_down_s.astype(BF16)
    w['w_gate_e'], w['w_up_e'], w['w_down_e'] = w_gate_e, w_up_e, w_down_e
    return w


def _mix_prompt(x, w):
    bsz, length, _ = x.shape
    q, kt, vt, kb, vb, lf, u_tb = _inproj(x, w)
    ct = _cumsum_lanes(lf.reshape(bsz, length, N_HEADS).transpose(0, 2, 1))
    c = ct.transpose(0, 2, 1).reshape(bsz * length, N_HEADS)
    attn = _attn_prompt(q, kb, vb, c, ct, bsz, length)
    h0 = jnp.zeros((bsz, 2 * D_STATE), F32)
    ssm_lb, hlast = _ssm(u_tb, h0, w, bsz, length)
    xres, hn, logits = _outproj(attn, ssm_lb, x, w)
    heads_last = lambda a: a.reshape(1, bsz, N_HEADS, HEAD_DIM, length).transpose(0, 1, 4, 2, 3)
    caches = (heads_last(kt), heads_last(vt),
              lf.reshape(1, bsz, length, N_HEADS),
              hlast[:, :D_STATE].reshape(1, bsz, N_SSM_GROUPS, SSM_STATE),
              hlast[:, D_STATE:].reshape(1, bsz, N_SSM_GROUPS, SSM_STATE))
    return xres, hn, logits, caches


def _mix_sample(x, cache_k, cache_v, cache_logf, page_table, h0_re, h0_im, w):
    bsz, n_new, _ = x.shape
    rows = n_new * bsz
    xt = x.transpose(1, 0, 2).reshape(1, rows, D_MODEL)
    q, kt, vt, _, _, lf, u_tb = _inproj(xt, w)
    k, v = kt[0].T, vt[0].T
    to_bm = lambda a: a.reshape(n_new, bsz, -1).transpose(1, 0, 2)
    q_b, k_b, v_b, lf_b = to_bm(q), to_bm(k), to_bm(v), to_bm(lf)
    head = jnp.arange(D_ATTN) // HEAD_DIM
    hmask = (head[None, :] == jnp.arange(N_HEADS)[:, None]).astype(BF16)
    qbd = (q_b[:, None, :, :] * hmask[None, :, None, :]).reshape(bsz, N_HEADS * n_new, D_ATTN)
    pad_keys = ((0, 0), (0, PAGE_SIZE - n_new), (0, 0))
    n_pool = cache_k.shape[1]
    keys_minor = lambda c: c[0].transpose(0, 2, 3, 1).reshape(n_pool, D_ATTN, PAGE_SIZE)
    attn = _attn_sample(
        qbd, keys_minor(cache_k), keys_minor(cache_v),
        cache_logf[0].astype(F32).transpose(0, 2, 1), page_table.astype(jnp.int32),
        jnp.pad(k_b, pad_keys).transpose(0, 2, 1), jnp.pad(v_b, pad_keys).transpose(0, 2, 1),
        jnp.pad(lf_b.transpose(0, 2, 1), ((0, 0), (0, 0), (0, PAGE_SIZE - n_new))), n_new)
    attn_tm = attn.transpose(1, 0, 2).reshape(rows, D_ATTN)
    h0 = jnp.concatenate([h0_re.reshape(bsz, D_STATE), h0_im.reshape(bsz, D_STATE)], axis=1).astype(F32)
    ssm_lb, hlast = _ssm(u_tb.reshape(n_new, bsz * D_SSM), h0, w, bsz, n_new)
    xres, hn, logits = _outproj(attn_tm, ssm_lb.reshape(rows, D_SSM), xt, w)
    caches = (k_b.reshape(1, bsz, n_new, N_HEADS, HEAD_DIM), v_b.reshape(1, bsz, n_new, N_HEADS, HEAD_DIM),
              lf_b.reshape(1, bsz, n_new, N_HEADS),
              hlast[:, :D_STATE].reshape(1, bsz, N_SSM_GROUPS, SSM_STATE),
              hlast[:, D_STATE:].reshape(1, bsz, N_SSM_GROUPS, SSM_STATE))
    return xres, hn, logits, caches


def _moe(hn, logits, w):
    n_tok = hn.shape[1]
    tri = (jnp.arange(ROUTE_TILE)[:, None] > jnp.arange(ROUTE_TILE)[None, :]).astype(BF16)
    idx, gates, rank, counts = _route(logits, w['router_bias'], tri)
    counts = counts[0, :N_EXPERTS].astype(jnp.int32)
    padded = (counts + FFN_BLOCK - 1) // FFN_BLOCK * FFN_BLOCK
    pend = jnp.cumsum(padded)
    pstart = pend - padded
    n_blk = -(-(n_tok * TOP_K) // FFN_BLOCK) + N_EXPERTS
    n_slots = n_blk * FFN_BLOCK
    pos = (pstart[idx[:, :TOP_K]] + rank[:, :TOP_K]).T.reshape(-1)
    blk_start = jnp.arange(n_blk, dtype=jnp.int32) * FFN_BLOCK
    blk_expert = jnp.minimum(jnp.sum(pend[None, :] <= blk_start[:, None], axis=1), N_EXPERTS - 1).astype(jnp.int32)
    blk_valid = jnp.clip(pstart[blk_expert] + counts[blk_expert] - blk_start, 0, FFN_BLOCK).astype(jnp.int32)
    plane_pos = (pos[None, :] + n_slots * jnp.arange(N_PLANES, dtype=jnp.int32)[:, None]).reshape(-1)
    xs = _scatter_rows(hn.reshape(N_PLANES * n_tok, LANES), plane_pos, N_PLANES * n_slots, n_tok)
    ys = _expert_ffn(xs.reshape(N_PLANES, n_slots, LANES), blk_expert, blk_valid,
                     w['w_gate_e'], w['w_up_e'], w['w_down_e'])
    ysg = _gather_rows(ys.reshape(N_PLANES * n_slots, LANES), plane_pos)
    return gates, ysg.reshape(N_PLANES, TOP_K * n_tok, LANES)


def kernel(x_prompt, x_sample, cache_k, cache_v, cache_logf, page_table, state_ssm_re, state_ssm_im, norm_mix_w, w_in, b_forget, q_norm_w, k_norm_w, lambda_re, lambda_im, log_dt, b_re, b_im, c_re, c_im, d_skip, w_glu, b_glu, attn_out_norm_w, ssm_out_norm_w, w_out, norm_ffn_w, w_router, router_bias, w_gate_e, w_up_e, w_down_e, w_gate_s, w_up_s, w_down_s):
    assert norm_mix_w.shape[0] == 1, "single-layer trunk"
    w = _prepare_weights(norm_mix_w[0], w_in[0], b_forget[0], q_norm_w[0], k_norm_w[0], lambda_re[0], lambda_im[0],
                         log_dt[0], b_re[0], b_im[0], c_re[0], c_im[0], d_skip[0], w_glu[0], b_glu[0],
                         attn_out_norm_w[0], ssm_out_norm_w[0], w_out[0], norm_ffn_w[0], w_router[0], router_bias[0],
                         w_gate_e[0], w_up_e[0], w_down_e[0], w_gate_s[0], w_up_s[0], w_down_s[0])
    bp, lp, _ = x_prompt.shape
    bs, ls, _ = x_sample.shape
    xres_p, hn_p, lg_p, caches_p = _mix_prompt(x_prompt, w)
    xres_s, hn_s, lg_s, caches_s = _mix_sample(x_sample, cache_k, cache_v, cache_logf, page_table,
                                               state_ssm_re[0], state_ssm_im[0], w)
    n_p = bp * lp
    n_tok = n_p + bs * ls
    gates, ysg = _moe(jnp.concatenate([hn_p, hn_s], axis=1), jnp.concatenate([lg_p, lg_s], axis=0), w)
    y_p = _combine(xres_p, gates, ysg, 0, n_tok).reshape(bp, lp, D_MODEL)
    y_s = _combine(xres_s, gates, ysg, n_p, n_tok).reshape(ls, bs, D_MODEL).transpose(1, 0, 2)
    return (y_p, y_s) + caches_p + caches_s
```

```python
import functools
import math

import jax
import jax.numpy as jnp
from jax import lax
from jax.experimental import pallas as pl
from jax.experimental.pallas import tpu as pltpu
from jax.experimental.pallas import tpu_sc as plsc

F32 = jnp.float32
BF16 = jnp.bfloat16

D_MODEL = 1024
D_ATTN = 512
D_SSM = 512
HEAD_DIM = 64
N_HEADS = 8
ATTN_SCALE = HEAD_DIM ** -0.5
LOG2E = math.log2(math.e)
SSM_GROUP = 16
N_SSM_GROUPS = 32
SSM_STATE = 64
D_STATE = N_SSM_GROUPS * SSM_STATE
N_EXPERTS = 64
TOP_K = 8
D_EXPERT = 256
D_SHARED = 256
ROUTE_SCALE = 2.5
PAGE_SIZE = 128
RMS_EPS = 1e-6

LANES = 128
VMEM_LIMIT = 56 * 1024 * 1024

ROW_TILE = 512
ATTN_TILE = 512
PAGES_PER_STEP = 32
PAGES_PER_CHUNK = 32
ROUTE_TILE = 256
FFN_BLOCK = 512
COMBINE_TILE = 256
SC_WINDOW = 128


def _params(*sem):
    return pltpu.CompilerParams(dimension_semantics=sem, vmem_limit_bytes=VMEM_LIMIT)


def _rms(x, w):
    return x * lax.rsqrt(jnp.mean(x * x, axis=-1, keepdims=True) + RMS_EPS) * w


def _lane_cumsum(x):
    n = x.shape[1]
    lane = lax.broadcasted_iota(jnp.int32, x.shape, 1)
    s = 1
    while s < n:
        x = x + jnp.where(lane >= s, pltpu.roll(x, s, axis=1), 0.0)
        s *= 2
    return x


def _inproj_body(x_ref, nw_ref, wqkv_ref, wf_ref, wu_ref, bf_ref, qnw_ref, knw_ref, seg_ref,
                 q_ref, k_ref, v_ref, kb_ref, vb_ref, lf_ref, u_ref):
    hn = _rms(x_ref[0], nw_ref[...]).astype(BF16)
    qkv = jnp.dot(hn, wqkv_ref[...], preferred_element_type=F32)
    seg = seg_ref[...]

    def head_norm(t, w):
        t2 = t * t
        hi = t2.astype(BF16)
        lo = (t2 - hi.astype(F32)).astype(BF16)
        ms = jnp.dot(hi, seg, preferred_element_type=F32) + jnp.dot(lo, seg, preferred_element_type=F32)
        return t * lax.rsqrt(ms + RMS_EPS) * w

    q = head_norm(qkv[:, :D_ATTN], qnw_ref[...])
    k = head_norm(qkv[:, D_ATTN:2 * D_ATTN], knw_ref[...])
    v = qkv[:, 2 * D_ATTN:]
    q_ref[...] = (q * (ATTN_SCALE * LOG2E)).astype(BF16)
    k_ref[0] = k.T
    v_ref[0] = v.T
    kb_ref[...] = k.astype(BF16)
    vb_ref[...] = v.astype(BF16)
    z = jnp.dot(hn, wf_ref[...], preferred_element_type=F32) + bf_ref[...]
    lf = jnp.minimum(z, 0.0) - jnp.log(1.0 + jnp.exp(-jnp.abs(z)))
    lf_ref[...] = lf[:, :N_HEADS]
    u_ref[...] = jnp.dot(hn, wu_ref[...], preferred_element_type=F32)


def _inproj(x3, w):
    bsz, length, _ = x3.shape
    tm = min(ROW_TILE, length)
    nt = length // tm
    rows = bsz * length
    row_map = lambda b, t: (b * nt + t, 0)
    const = lambda b, t: (0, 0)
    full = lambda a: pl.BlockSpec(a.shape, const)
    consts = [w['norm_mix_w'], w['wqkv'], w['wf'], w['wu'], w['b_forget'], w['q_norm_w'], w['k_norm_w'], w['seg']]
    return pl.pallas_call(
        _inproj_body,
        grid=(bsz, nt),
        in_specs=[pl.BlockSpec((1, tm, D_MODEL), lambda b, t: (b, t, 0))] + [full(a) for a in consts],
        out_specs=[pl.BlockSpec((tm, D_ATTN), row_map)]
        + [pl.BlockSpec((1, D_ATTN, tm), lambda b, t: (b, 0, t))] * 2
        + [pl.BlockSpec((tm, D_ATTN), row_map)] * 2
        + [pl.BlockSpec((tm, N_HEADS), row_map), pl.BlockSpec((tm, D_SSM), lambda b, t: (t, b))],
        out_shape=[jax.ShapeDtypeStruct((rows, D_ATTN), BF16),
                   jax.ShapeDtypeStruct((bsz, D_ATTN, length), F32),
                   jax.ShapeDtypeStruct((bsz, D_ATTN, length), F32),
                   jax.ShapeDtypeStruct((rows, D_ATTN), BF16),
                   jax.ShapeDtypeStruct((rows, D_ATTN), BF16),
                   jax.ShapeDtypeStruct((rows, N_HEADS), F32),
                   jax.ShapeDtypeStruct((length, bsz * D_SSM), F32)],
        compiler_params=_params("parallel", "parallel"),
        name="inproj",
    )(x3, *consts)


def _cumsum_body(lf_ref, c_ref):
    c_ref[0] = _lane_cumsum(lf_ref[0])


def _cumsum_lanes(lft):
    bsz, nh, length = lft.shape
    spec = pl.BlockSpec((1, nh, length), lambda b: (b, 0, 0))
    return pl.pallas_call(
        _cumsum_body, grid=(bsz,), in_specs=[spec], out_specs=spec,
        out_shape=jax.ShapeDtypeStruct(lft.shape, F32),
        compiler_params=_params("parallel"), name="logf_cumsum",
    )(lft)


def _attn_prompt_body(q_ref, k_ref, v_ref, ct_ref, o_ref, acc_ref, m_ref, *, tile):
    i = pl.program_id(1)
    lane = lax.broadcasted_iota(jnp.int32, (1, LANES), 1)
    row = lax.broadcasted_iota(jnp.int32, (tile, tile), 0)
    col = lax.broadcasted_iota(jnp.int32, (tile, tile), 1)
    hmasks = (lane < HEAD_DIM, lane >= HEAD_DIM)
    sum_lane = (HEAD_DIM, 0)
    ones_col = [jnp.where(lane == sl, 1.0, 0.0).astype(BF16) for sl in sum_lane]
    for hp in range(N_HEADS // 2):
        lanes = slice(LANES * hp, LANES * (hp + 1))
        qp = q_ref[:, lanes]
        qh = [jnp.where(hm, qp, jnp.zeros_like(qp)) for hm in hmasks]
        m_ref[...] = jnp.full(m_ref.shape, -jnp.inf, F32)
        acc_ref[...] = jnp.zeros(acc_ref.shape, F32)

        def kstep(j, diagonal, qh=qh, lanes=lanes, hp=hp):
            r0 = pl.multiple_of(j * tile, tile)
            kj = k_ref[pl.ds(r0, tile), lanes]
            vj = v_ref[pl.ds(r0, tile), lanes]
            for hh in range(2):
                s = lax.dot_general(qh[hh], kj, (((1,), (1,)), ((), ())), preferred_element_type=F32)
                s = s - ct_ref[0, 2 * hp + hh:2 * hp + hh + 1, pl.ds(r0, tile)] * LOG2E
                if diagonal:
                    s = jnp.where(row >= col, s, -jnp.inf)
                m_old = m_ref[hh]
                m_new = jnp.maximum(m_old, jnp.max(s, axis=1, keepdims=True))
                alpha = jnp.exp2(m_old - m_new)
                p = jnp.exp2(s - m_new).astype(BF16)
                vh = jnp.where(hmasks[hh], vj, ones_col[hh])
                acc_ref[hh] = alpha * acc_ref[hh] + jnp.dot(p, vh, preferred_element_type=F32)
                m_ref[hh] = m_new

        def body(j, carry):
            kstep(j, False)
            return carry

        lax.fori_loop(0, i, body, 0)
        kstep(i, True)
        acc0, acc1 = acc_ref[0], acc_ref[1]
        o_ref[:, lanes] = jnp.where(hmasks[0], acc0 / acc0[:, sum_lane[0]:sum_lane[0] + 1],
                                    acc1 / acc1[:, sum_lane[1]:sum_lane[1] + 1])


def _attn_prompt(q, kb, vb, ct, bsz, length):
    tile = min(ATTN_TILE, length)
    nq = length // tile
    q_map = lambda b, i: (b * nq + i, 0)
    seq_map = lambda b, i: (b, 0)
    return pl.pallas_call(
        functools.partial(_attn_prompt_body, tile=tile),
        grid=(bsz, nq),
        in_specs=[pl.BlockSpec((tile, D_ATTN), q_map),
                  pl.BlockSpec((length, D_ATTN), seq_map),
                  pl.BlockSpec((length, D_ATTN), seq_map),
                  pl.BlockSpec((1, N_HEADS, length), lambda b, i: (b, 0, 0))],
        out_specs=pl.BlockSpec((tile, D_ATTN), q_map),
        out_shape=jax.ShapeDtypeStruct((bsz * length, D_ATTN), F32),
        scratch_shapes=[pltpu.VMEM((2, tile, LANES), F32), pltpu.VMEM((2, tile, 1), F32)],
        compiler_params=_params("parallel", "parallel"),
        name="attn_prompt",
    )(q, kb, vb, ct)


def _attn_sample_body(pt_ref, q_ref, *refs, pages, n_new):
    del pt_ref
    k_refs, v_refs, lf_refs = refs[:pages], refs[pages:2 * pages], refs[2 * pages:3 * pages]
    kn_ref, vn_ref, lfn_ref, o_ref, m_ref, l_ref, acc_ref, c_ref = refs[3 * pages:]
    j = pl.program_id(1)
    rows = N_HEADS * n_new

    @pl.when(j == 0)
    def _():
        m_ref[...] = jnp.full(m_ref.shape, -jnp.inf, F32)
        l_ref[...] = jnp.zeros(l_ref.shape, F32)
        acc_ref[...] = jnp.zeros(acc_ref.shape, F32)
        c_ref[...] = jnp.zeros(c_ref.shape, F32)

    q = q_ref[0]

    def chunk(kc, vc, lft, valid):
        n = kc.shape[1]
        s = jnp.dot(q, kc, preferred_element_type=F32)
        ck = _lane_cumsum(lft) + c_ref[:, 0:1]
        c_ref[...] = jnp.broadcast_to(ck[:, n - 1:n], c_ref.shape)
        ck2 = ck * LOG2E
        s = s - jnp.concatenate([jnp.broadcast_to(ck2[h:h + 1], (n_new, n)) for h in range(N_HEADS)], axis=0)
        if valid is not None:
            s = jnp.where(valid, s, -jnp.inf)
        m_old = m_ref[...]
        m_new = jnp.maximum(m_old, jnp.max(s, axis=1, keepdims=True))
        alpha = jnp.exp2(m_old - m_new)
        p = jnp.exp2(s - m_new)
        l_ref[...] = alpha * l_ref[...] + jnp.sum(p, axis=1, keepdims=True)
        acc_ref[...] = alpha * acc_ref[...] + lax.dot_general(
            p.astype(BF16), vc, (((1,), (1,)), ((), ())), preferred_element_type=F32)
        m_ref[...] = m_new

    for g in range(0, pages, PAGES_PER_CHUNK):
        grp = slice(g, g + PAGES_PER_CHUNK)
        chunk(jnp.concatenate([r[0] for r in k_refs[grp]], axis=1).astype(BF16),
              jnp.concatenate([r[0] for r in v_refs[grp]], axis=1).astype(BF16),
              jnp.concatenate([r[0] for r in lf_refs[grp]], axis=1), None)

    @pl.when(j == pl.num_programs(1) - 1)
    def _():
        qi = lax.broadcasted_iota(jnp.int32, (rows, PAGE_SIZE), 0) % n_new
        kj = lax.broadcasted_iota(jnp.int32, (rows, PAGE_SIZE), 1)
        chunk(kn_ref[0].astype(BF16), vn_ref[0].astype(BF16), lfn_ref[0], kj <= qi)
        o_full = acc_ref[...] / l_ref[...]
        lane = lax.broadcasted_iota(jnp.int32, (1, D_ATTN), 1)
        out = jnp.zeros((n_new, D_ATTN), F32)
        for h in range(N_HEADS):
            hmask = (lane >= HEAD_DIM * h) & (lane < HEAD_DIM * (h + 1))
            out = out + jnp.where(hmask, o_full[n_new * h:n_new * (h + 1)], 0.0)
        o_ref[0] = out


def _attn_sample(qbd, cache_k, cache_v, cache_lft, page_table, kn, vn, lfn, n_new):
    bsz, n_pages = page_table.shape
    pages = min(PAGES_PER_STEP, n_pages)
    nj = n_pages // pages
    rows = N_HEADS * n_new

    def page_map(p):
        return lambda b, j, pt: (pt[b * n_pages + j * pages + p], 0, 0)

    seq_map = lambda b, j, pt: (b, 0, 0)
    in_specs = [pl.BlockSpec((1, rows, D_ATTN), seq_map)]
    in_specs += [pl.BlockSpec((1, D_ATTN, PAGE_SIZE), page_map(p)) for p in range(pages)]
    in_specs += [pl.BlockSpec((1, D_ATTN, PAGE_SIZE), page_map(p)) for p in range(pages)]
    in_specs += [pl.BlockSpec((1, N_HEADS, PAGE_SIZE), page_map(p)) for p in range(pages)]
    in_specs += [pl.BlockSpec((1, D_ATTN, PAGE_SIZE), seq_map)] * 2 + [pl.BlockSpec((1, N_HEADS, PAGE_SIZE), seq_map)]
    return pl.pallas_call(
        functools.partial(_attn_sample_body, pages=pages, n_new=n_new),
        grid_spec=pltpu.PrefetchScalarGridSpec(
            num_scalar_prefetch=1, grid=(bsz, nj), in_specs=in_specs,
            out_specs=pl.BlockSpec((1, n_new, D_ATTN), seq_map),
            scratch_shapes=[pltpu.VMEM((rows, 1), F32), pltpu.VMEM((rows, 1), F32),
                            pltpu.VMEM((rows, D_ATTN), F32), pltpu.VMEM((N_HEADS, LANES), F32)]),
        out_shape=jax.ShapeDtypeStruct((bsz, n_new, D_ATTN), F32),
        compiler_params=_params("parallel", "arbitrary"),
        name="attn_sample",
    )(page_table.reshape(-1), qbd, *([cache_k] * pages), *([cache_v] * pages), *([cache_lft] * pages), kn, vn, lfn)


SCAN_LANES = 512


def _ssm_body(u_ref, h0_ref, bw_ref, cre_ref, cim_ref, ar_ref, ai_ref, dsk_ref, wglu_ref, bglu_ref,
              out_ref, hlast_ref, hbuf_ref, state_ref, ubuf_ref, *, steps, bsz):
    g = pl.program_id(0)

    @pl.when(g == 0)
    def _():
        state_ref[...] = h0_ref[...]

    n_tiles = D_SSM // LANES
    for b in range(bsz):
        for t in range(n_tiles):
            col = D_SSM * b + LANES * t
            ubuf_ref[t, pl.ds(b, steps, stride=bsz), :] = u_ref[:, col:col + LANES]
    u = jnp.concatenate([ubuf_ref[t] for t in range(n_tiles)], axis=1)
    ub = u.astype(BF16)
    for p in range(N_SSM_GROUPS // 2):
        t = p // 4
        bu = jnp.dot(ub[:, LANES * t:LANES * (t + 1)], bw_ref[p], preferred_element_type=F32)
        hbuf_ref[:, LANES * p:LANES * (p + 1)] = bu[:, :LANES]
        hbuf_ref[:, D_STATE + LANES * p:D_STATE + LANES * (p + 1)] = bu[:, LANES:]

    for c in range(D_STATE // SCAN_LANES):
        re_l = slice(SCAN_LANES * c, SCAN_LANES * (c + 1))
        im_l = slice(D_STATE + SCAN_LANES * c, D_STATE + SCAN_LANES * (c + 1))
        ar = jnp.broadcast_to(ar_ref[:, re_l], (bsz, SCAN_LANES))
        ai = jnp.broadcast_to(ai_ref[:, re_l], (bsz, SCAN_LANES))

        def step(t, carry, re_l=re_l, im_l=im_l, ar=ar, ai=ai):
            re, im = carry
            r0 = pl.multiple_of(t * bsz, bsz)
            nre = ar * re - ai * im + hbuf_ref[pl.ds(r0, bsz), re_l]
            nim = ar * im + ai * re + hbuf_ref[pl.ds(r0, bsz), im_l]
            hbuf_ref[pl.ds(r0, bsz), re_l] = nre
            hbuf_ref[pl.ds(r0, bsz), im_l] = nim
            return nre, nim

        re, im = lax.fori_loop(0, steps, step, (state_ref[:, re_l], state_ref[:, im_l]))
        state_ref[:, re_l] = re
        state_ref[:, im_l] = im

    hlast_ref[...] = state_ref[...]

    ys = []
    for t in range(D_SSM // LANES):
        w = 4 * LANES
        hre = hbuf_ref[:, w * t:w * (t + 1)].astype(BF16)
        him = hbuf_ref[:, D_STATE + w * t:D_STATE + w * (t + 1)].astype(BF16)
        ys.append(jnp.dot(hre, cre_ref[t], preferred_element_type=F32)
                  + jnp.dot(him, cim_ref[t], preferred_element_type=F32))
    y = jnp.concatenate(ys, axis=1) + dsk_ref[...] * u
    z = 0.5 * y * (1.0 + jnp.tanh(math.sqrt(2.0 / math.pi) * (y + 0.044715 * (y * y * y))))
    gate = jnp.dot(z.astype(BF16), wglu_ref[...], preferred_element_type=F32) + bglu_ref[...]
    out = z / (1.0 + jnp.exp(-gate))
    for t in range(n_tiles):
        ubuf_ref[t] = out[:, LANES * t:LANES * (t + 1)]
    for b in range(bsz):
        for t in range(n_tiles):
            col = D_SSM * b + LANES * t
            out_ref[:, col:col + LANES] = ubuf_ref[t, pl.ds(b, steps, stride=bsz), :]


def _ssm(u_lb, h0, w, bsz, length):
    steps = max(1, min(length, ROW_TILE // bsz))
    rows = steps * bsz
    const2 = lambda g: (0, 0)
    const3 = lambda g: (0, 0, 0)
    full = lambda a: pl.BlockSpec(a.shape, const2 if a.ndim == 2 else const3)
    consts = [h0, w['ssm_bw'], w['ssm_cre'], w['ssm_cim'], w['ssm_ar'], w['ssm_ai'], w['d_skip'], w['w_glu'], w['b_glu']]
    return pl.pallas_call(
        functools.partial(_ssm_body, steps=steps, bsz=bsz),
        grid=(length // steps,),
        in_specs=[pl.BlockSpec((steps, bsz * D_SSM), lambda g: (g, 0))] + [full(a) for a in consts],
        out_specs=[pl.BlockSpec((steps, bsz * D_SSM), lambda g: (g, 0)), pl.BlockSpec((bsz, 2 * D_STATE), const2)],
        out_shape=[jax.ShapeDtypeStruct((length, bsz * D_SSM), F32), jax.ShapeDtypeStruct((bsz, 2 * D_STATE), F32)],
        scratch_shapes=[pltpu.VMEM((rows, 2 * D_STATE), F32), pltpu.VMEM((bsz, 2 * D_STATE), F32),
                        pltpu.VMEM((D_SSM // LANES, rows, LANES), F32)],
        compiler_params=_params("arbitrary"),
        name="ssm",
    )(u_lb, *consts)


N_PLANES = D_MODEL // (2 * LANES)


def _pack_rows(x, out_ref):
    bits = lax.bitcast_convert_type(x.astype(BF16).astype(F32), jnp.uint32)
    for c in range(N_PLANES):
        lo = bits[:, 2 * LANES * c:2 * LANES * c + LANES]
        hi = bits[:, 2 * LANES * c + LANES:2 * LANES * (c + 1)]
        out_ref[c] = lax.bitcast_convert_type(hi | (lo >> 16), jnp.int32)


def _unpack_rows(ref):
    parts = []
    for c in range(N_PLANES):
        bits = lax.bitcast_convert_type(ref[c], jnp.uint32)
        parts.append(lax.bitcast_convert_type(bits << 16, F32))
        parts.append(lax.bitcast_convert_type(bits & jnp.uint32(0xFFFF0000), F32))
    return parts


def _outproj_body(attn_ref, ssm_ref, x_ref, aw_ref, sw_ref, woa_ref, wos_ref, nfw_ref, wrh_ref, wrl_ref,
                  wgus_ref, wds_ref, xres_ref, hn_ref, logit_ref):
    an = _rms(attn_ref[...], aw_ref[...]).astype(BF16)
    sn = _rms(ssm_ref[...], sw_ref[...]).astype(BF16)
    x1 = x_ref[0] + (jnp.dot(an, woa_ref[...], preferred_element_type=F32)
                     + jnp.dot(sn, wos_ref[...], preferred_element_type=F32))
    h2 = _rms(x1, nfw_ref[...])
    hb = h2.astype(BF16)
    hlo = (h2 - hb.astype(F32)).astype(BF16)
    _pack_rows(h2, hn_ref)
    logit_ref[...] = (jnp.dot(hb, wrh_ref[...], preferred_element_type=F32)
                      + (jnp.dot(hb, wrl_ref[...], preferred_element_type=F32)
                         + jnp.dot(hlo, wrh_ref[...], preferred_element_type=F32)))
    gu = jnp.dot(hb, wgus_ref[...], preferred_element_type=F32)
    gs = gu[:, :D_SHARED]
    act = (gs / (1.0 + jnp.exp(-gs))) * gu[:, D_SHARED:]
    xres_ref[...] = x1 + jnp.dot(act.astype(BF16), wds_ref[...], preferred_element_type=F32)


def _outproj(attn, ssm_tb, x3, w):
    bsz, length, _ = x3.shape
    tm = min(ROW_TILE, length)
    nt = length // tm
    rows = bsz * length
    row_map = lambda b, t: (b * nt + t, 0)
    const = lambda b, t: (0, 0)
    full = lambda a: pl.BlockSpec(a.shape, const)
    consts = [w['attn_out_norm_w'], w['ssm_out_norm_w'], w['wo_a'], w['wo_s'], w['norm_ffn_w'],
              w['wr_hi'], w['wr_lo'], w['wgu_s'], w['wd_s']]
    return pl.pallas_call(
        _outproj_body,
        grid=(bsz, nt),
        in_specs=[pl.BlockSpec((tm, D_ATTN), row_map), pl.BlockSpec((tm, D_SSM), lambda b, t: (t, b)),
                  pl.BlockSpec((1, tm, D_MODEL), lambda b, t: (b, t, 0))] + [full(a) for a in consts],
        out_specs=[pl.BlockSpec((tm, D_MODEL), row_map),
                   pl.BlockSpec((N_PLANES, tm, LANES), lambda b, t: (0, b * nt + t, 0)),
                   pl.BlockSpec((tm, LANES), row_map)],
        out_shape=[jax.ShapeDtypeStruct((rows, D_MODEL), F32),
                   jax.ShapeDtypeStruct((N_PLANES, rows, LANES), jnp.int32),
                   jax.ShapeDtypeStruct((rows, LANES), F32)],
        compiler_params=_params("parallel", "parallel"),
        name="outproj",
    )(attn, ssm_tb, x3, *consts)


def _route_body(logit_ref, bias_ref, tri_ref, idx_ref, gate_ref, rank_ref, cnt_ref, carry_ref):
    @pl.when(pl.program_id(0) == 0)
    def _():
        carry_ref[...] = jnp.zeros(carry_ref.shape, F32)

    lg = logit_ref[...]
    tr = lg.shape[0]
    lane = lax.broadcasted_iota(jnp.int32, (tr, LANES), 1).astype(F32)
    score = 1.0 / (1.0 + jnp.exp(-lg))
    sel = jnp.where(lane < N_EXPERTS, score + bias_ref[...], -jnp.inf)
    picked = []
    gates = jnp.zeros((tr, LANES), F32)
    idxs = jnp.zeros((tr, LANES), F32)
    member = jnp.zeros((tr, LANES), F32)
    for k in range(TOP_K):
        best = jnp.max(sel, axis=1, keepdims=True)
        e = jnp.min(jnp.where(sel == best, lane, float(LANES)), axis=1, keepdims=True)
        hit = lane == e
        picked.append(hit)
        gates = jnp.where(lane == k, jnp.sum(jnp.where(hit, score, 0.0), axis=1, keepdims=True), gates)
        idxs = jnp.where(lane == k, e, idxs)
        member = jnp.where(hit, 1.0, member)
        sel = jnp.where(hit, -jnp.inf, sel)
    gate_ref[...] = ROUTE_SCALE * gates / jnp.sum(gates, axis=1, keepdims=True)
    idx_ref[...] = idxs.astype(jnp.int32)
    before = jnp.dot(tri_ref[...], member.astype(BF16), preferred_element_type=F32) + carry_ref[...]
    ranks = jnp.zeros((tr, LANES), F32)
    for k in range(TOP_K):
        ranks = jnp.where(lane == k, jnp.sum(jnp.where(picked[k], before, 0.0), axis=1, keepdims=True), ranks)
    rank_ref[...] = ranks.astype(jnp.int32)
    carry_ref[...] = carry_ref[...] + jnp.sum(member, axis=0, keepdims=True)
    cnt_ref[...] = carry_ref[...]


def _route(logits, bias, tri):
    n_tok = logits.shape[0]
    tr = tri.shape[0]
    tok_map = lambda i: (i, 0)
    const = lambda i: (0, 0)
    return pl.pallas_call(
        _route_body,
        grid=(n_tok // tr,),
        in_specs=[pl.BlockSpec((tr, LANES), tok_map), pl.BlockSpec((1, LANES), const), pl.BlockSpec((tr, tr), const)],
        out_specs=[pl.BlockSpec((tr, LANES), tok_map)] * 3 + [pl.BlockSpec((1, LANES), const)],
        out_shape=[jax.ShapeDtypeStruct((n_tok, LANES), jnp.int32), jax.ShapeDtypeStruct((n_tok, LANES), F32),
                   jax.ShapeDtypeStruct((n_tok, LANES), jnp.int32), jax.ShapeDtypeStruct((1, LANES), F32)],
        scratch_shapes=[pltpu.VMEM((1, LANES), F32)],
        compiler_params=_params("arbitrary"),
        name="route",
    )(logits, bias, tri)


def _slot_body(idx_ref, rank_ref, pstart_ref, pos_ref):
    idx = idx_ref[...]
    base = jnp.take_along_axis(jnp.broadcast_to(pstart_ref[...], idx.shape), idx, axis=1)
    pos_ref[...] = (base + rank_ref[...]).T[:TOP_K]


def _slot_positions(idx, rank, pstart):
    n_tok = idx.shape[0]
    tr = ROUTE_TILE
    tok_map = lambda i: (i, 0)
    return pl.pallas_call(
        _slot_body,
        grid=(n_tok // tr,),
        in_specs=[pl.BlockSpec((tr, LANES), tok_map), pl.BlockSpec((tr, LANES), tok_map),
                  pl.BlockSpec((1, LANES), lambda i: (0, 0))],
        out_specs=pl.BlockSpec((TOP_K, tr), lambda i: (0, i)),
        out_shape=jax.ShapeDtypeStruct((TOP_K, n_tok), jnp.int32),
        compiler_params=_params("parallel"),
        name="slot_positions",
    )(idx, rank, pstart)


def _gather_rows(table, indices):
    n_idx = indices.shape[0]
    width = table.shape[1]
    mesh = plsc.VectorSubcoreMesh(core_axis_name="core", subcore_axis_name="subcore")

    @pl.kernel(out_type=jax.ShapeDtypeStruct((n_idx, width), table.dtype), mesh=mesh)
    def gather(table_hbm, idx_hbm, out_hbm):
        def body(idx_vmem, out_vmem):
            pltpu.sync_copy(table_hbm.at[idx_vmem.at[0]], out_vmem)

        pltpu.emit_pipeline(
            body,
            grid=(n_idx // SC_WINDOW,),
            in_specs=[pl.BlockSpec((1, SC_WINDOW), lambda i: (0, i))],
            out_specs=[pl.BlockSpec((SC_WINDOW, width), lambda i: (i, 0))],
            core_axis_name=("core", "subcore"),
            dimension_semantics=(pltpu.PARALLEL,),
        )(idx_hbm, out_hbm)

    return gather(table, indices.reshape(1, n_idx))


def _scatter_rows(table, dest, n_out, n_rows):
    n_idx = dest.shape[0]
    width = table.shape[1]
    win_per_plane = n_rows // SC_WINDOW
    win_per_rep_plane = n_idx // SC_WINDOW // (table.shape[0] // n_rows)
    mesh = plsc.VectorSubcoreMesh(core_axis_name="core", subcore_axis_name="subcore")

    @pl.kernel(out_type=jax.ShapeDtypeStruct((n_out, width), table.dtype), mesh=mesh)
    def scatter(table_hbm, idx_hbm, out_hbm):
        def body(rows_vmem, idx_vmem):
            pltpu.sync_copy(rows_vmem, out_hbm.at[idx_vmem.at[0]])

        pltpu.emit_pipeline(
            body,
            grid=(n_idx // SC_WINDOW,),
            in_specs=[pl.BlockSpec((SC_WINDOW, width),
                                   lambda i: ((i // win_per_rep_plane) * win_per_plane + i % win_per_plane, 0)),
                      pl.BlockSpec((1, SC_WINDOW), lambda i: (0, i))],
            out_specs=[],
            core_axis_name=("core", "subcore"),
            dimension_semantics=(pltpu.PARALLEL,),
        )(table_hbm, idx_hbm)

    return scatter(table, dest.reshape(1, n_idx))


def _ffn_body(be_ref, nv_ref, x_ref, wg_ref, wu_ref, wd_ref, y_ref, wgu_s, wd_s):
    r = pl.program_id(0)

    @pl.when((r == 0) | (be_ref[r] != be_ref[jnp.maximum(r - 1, 0)]))
    def _():
        wgu_s[:, :D_EXPERT] = wg_ref[0].astype(BF16)
        wgu_s[:, D_EXPERT:] = wu_ref[0].astype(BF16)
        wd_s[...] = wd_ref[0].astype(BF16)

    n_valid = nv_ref[r]

    def ffn(partial_block):
        x = jnp.concatenate(_unpack_rows(x_ref), axis=1)
        if partial_block:
            x = jnp.where(lax.broadcasted_iota(jnp.int32, (FFN_BLOCK, 1), 0) < n_valid, x, 0.0)
        gu = jnp.dot(x.astype(BF16), wgu_s[...], preferred_element_type=F32)
        gs = gu[:, :D_EXPERT]
        act = (gs / (1.0 + jnp.exp(-gs))) * gu[:, D_EXPERT:]
        _pack_rows(jnp.dot(act.astype(BF16), wd_s[...], preferred_element_type=F32), y_ref)

    pl.when(n_valid == FFN_BLOCK)(functools.partial(ffn, False))
    pl.when((n_valid > 0) & (n_valid < FFN_BLOCK))(functools.partial(ffn, True))

    @pl.when(n_valid == 0)
    def _():
        y_ref[...] = jnp.zeros(y_ref.shape, jnp.int32)


def _expert_ffn(xs, blk_expert, blk_valid, w_gate, w_up, w_down):
    n_slots = xs.shape[1]
    nb = n_slots // FFN_BLOCK
    slot_spec = pl.BlockSpec((N_PLANES, FFN_BLOCK, LANES), lambda r, be, nv: (0, r, 0))
    return pl.pallas_call(
        _ffn_body,
        grid_spec=pltpu.PrefetchScalarGridSpec(
            num_scalar_prefetch=2, grid=(nb,),
            in_specs=[slot_spec,
                      pl.BlockSpec((1, D_MODEL, D_EXPERT), lambda r, be, nv: (be[r], 0, 0)),
                      pl.BlockSpec((1, D_MODEL, D_EXPERT), lambda r, be, nv: (be[r], 0, 0)),
                      pl.BlockSpec((1, D_EXPERT, D_MODEL), lambda r, be, nv: (be[r], 0, 0))],
            out_specs=slot_spec,
            scratch_shapes=[pltpu.VMEM((D_MODEL, 2 * D_EXPERT), BF16), pltpu.VMEM((D_EXPERT, D_MODEL), BF16)]),
        out_shape=jax.ShapeDtypeStruct((N_PLANES, n_slots, LANES), jnp.int32),
        compiler_params=_params("arbitrary"),
        name="expert_ffn",
    )(blk_expert, blk_valid, xs, w_gate, w_up, w_down)


def _combine_body(xres_ref, gate_ref, *refs):
    y_refs, o_ref = refs[:TOP_K], refs[TOP_K]
    g = gate_ref[...]
    acc = [xres_ref[:, LANES * i:LANES * (i + 1)] for i in range(D_MODEL // LANES)]
    for k in range(TOP_K):
        gk = g[:, k:k + 1]
        acc = [a + gk * p for a, p in zip(acc, _unpack_rows(y_refs[k]))]
    o_ref[...] = jnp.concatenate(acc, axis=1)


def _combine(xres, gates, ysg, row_offset, n_tok):
    rows = xres.shape[0]
    tc = min(COMBINE_TILE, rows)
    off = row_offset // tc
    per_k = n_tok // tc
    y_specs = [pl.BlockSpec((N_PLANES, tc, LANES), lambda i, k=k: (0, k * per_k + off + i, 0)) for k in range(TOP_K)]
    return pl.pallas_call(
        _combine_body,
        grid=(rows // tc,),
        in_specs=[pl.BlockSpec((tc, D_MODEL), lambda i: (i, 0)),
                  pl.BlockSpec((tc, LANES), lambda i: (i + off, 0))] + y_specs,
        out_specs=pl.BlockSpec((tc, D_MODEL), lambda i: (i, 0)),
        out_shape=jax.ShapeDtypeStruct((rows, D_MODEL), F32),
        compiler_params=_params("parallel"),
        name="combine",
    )(xres, gates, *([ysg] * TOP_K))


def _prepare_weights(norm_mix_w, w_in, b_forget, q_norm_w, k_norm_w, lambda_re, lambda_im, log_dt, b_re, b_im,
                     c_re, c_im, d_skip, w_glu, b_glu, attn_out_norm_w, ssm_out_norm_w, w_out, norm_ffn_w,
                     w_router, router_bias, w_gate_e, w_up_e, w_down_e, w_gate_s, w_up_s, w_down_s):
    w = {}
    row = lambda a: a.reshape(1, -1).astype(F32)
    w['norm_mix_w'] = row(norm_mix_w)
    w['wqkv'] = w_in[:, :3 * D_ATTN].astype(BF16)
    w['wf'] = jnp.pad(w_in[:, 3 * D_ATTN:3 * D_ATTN + N_HEADS], ((0, 0), (0, LANES - N_HEADS))).astype(BF16)
    w['wu'] = w_in[:, 3 * D_ATTN + N_HEADS:].astype(BF16)
    w['b_forget'] = jnp.pad(row(b_forget), ((0, 0), (0, LANES - N_HEADS)))
    w['q_norm_w'] = jnp.tile(row(q_norm_w), (1, N_HEADS))
    w['k_norm_w'] = jnp.tile(row(k_norm_w), (1, N_HEADS))
    head = jnp.arange(D_ATTN) // HEAD_DIM
    w['seg'] = jnp.where(head[:, None] == head[None, :], 1.0 / HEAD_DIM, 0.0).astype(BF16)

    dt = jnp.exp(log_dt.astype(F32))[:, None]
    lre, lim = lambda_re.astype(F32), lambda_im.astype(F32)
    a, b = lre * dt, lim * dt
    ea = jnp.exp(a)
    bar_re, bar_im = ea * jnp.cos(b), ea * jnp.sin(b)
    num_re = jnp.expm1(a) * jnp.cos(b) - 2.0 * jnp.sin(0.5 * b) ** 2
    num_im = bar_im
    den = lre * lre + lim * lim
    coef_re = (num_re * lre + num_im * lim) / den
    coef_im = (num_im * lre - num_re * lim) / den
    bb_re = coef_re[:, :, None] * b_re - coef_im[:, :, None] * b_im
    bb_im = coef_re[:, :, None] * b_im + coef_im[:, :, None] * b_re
    eye = jnp.eye(N_SSM_GROUPS, dtype=F32)

    def in_block_diag(m):
        return (m.transpose(0, 2, 1)[:, :, None, :] * eye[:, None, :, None]).reshape(D_SSM, D_STATE)

    def out_block_diag(m):
        return (m.transpose(0, 2, 1)[:, :, None, :] * eye[:, None, :, None]).reshape(D_STATE, D_SSM)

    pairs = jnp.arange(N_SSM_GROUPS // 2)

    def pair_blocks(m):
        return m.reshape(4, LANES, N_SSM_GROUPS // 2, LANES).transpose(2, 0, 1, 3)[pairs, pairs // 4]

    w['ssm_bw'] = jnp.concatenate([pair_blocks(in_block_diag(bb_re)), pair_blocks(in_block_diag(bb_im))],
                                  axis=2).astype(BF16)
    tiles = jnp.arange(D_SSM // LANES)

    def tile_blocks(m):
        return m.reshape(4, 4 * LANES, 4, LANES).transpose(0, 2, 1, 3)[tiles, tiles]

    w['ssm_cre'] = tile_blocks(out_block_diag(c_re.astype(F32))).astype(BF16)
    w['ssm_cim'] = tile_blocks(out_block_diag(-c_im.astype(F32))).astype(BF16)
    w['ssm_ar'] = bar_re.reshape(1, D_STATE)
    w['ssm_ai'] = bar_im.reshape(1, D_STATE)
    w['d_skip'] = row(d_skip)
    w['w_glu'] = w_glu.astype(BF16)
    w['b_glu'] = row(b_glu)

    w['attn_out_norm_w'] = row(attn_out_norm_w)
    w['ssm_out_norm_w'] = row(ssm_out_norm_w)
    w['wo_a'] = w_out[:D_ATTN].astype(BF16)
    w['wo_s'] = w_out[D_ATTN:].astype(BF16)
    w['norm_ffn_w'] = row(norm_ffn_w)
    wr = jnp.pad(w_router.astype(F32), ((0, 0), (0, LANES - N_EXPERTS)))
    w['wr_hi'] = wr.astype(BF16)
    w['wr_lo'] = (wr - w['wr_hi'].astype(F32)).astype(BF16)
    w['router_bias'] = jnp.pad(row(router_bias), ((0, 0), (0, LANES - N_EXPERTS)))
    w['wgu_s'] = jnp.concatenate([w_gate_s, w_up_s], axis=1).astype(BF16)
    w['wd_s'] = w_down_s.astype(BF16)
    w['w_gate_e'], w['w_up_e'], w['w_down_e'] = w_gate_e, w_up_e, w_down_e
    return w


def _mix_prompt(x, w):
    bsz, length, _ = x.shape
    q, kt, vt, kb, vb, lf, u_tb = _inproj(x, w)
    ct = _cumsum_lanes(lf.reshape(bsz, length, N_HEADS).transpose(0, 2, 1))
    attn = _attn_prompt(q, kb, vb, ct, bsz, length)
    h0 = jnp.zeros((bsz, 2 * D_STATE), F32)
    ssm_lb, hlast = _ssm(u_tb, h0, w, bsz, length)
    xres, hn, logits = _outproj(attn, ssm_lb, x, w)
    heads_last = lambda a: a.reshape(1, bsz, N_HEADS, HEAD_DIM, length).transpose(0, 1, 4, 2, 3)
    caches = (heads_last(kt), heads_last(vt),
              lf.reshape(1, bsz, length, N_HEADS),
              hlast[:, :D_STATE].reshape(1, bsz, N_SSM_GROUPS, SSM_STATE),
              hlast[:, D_STATE:].reshape(1, bsz, N_SSM_GROUPS, SSM_STATE))
    return xres, hn, logits, caches


def _mix_sample(x, cache_k, cache_v, cache_logf, page_table, h0_re, h0_im, w):
    bsz, n_new, _ = x.shape
    rows = n_new * bsz
    xt = x.transpose(1, 0, 2).reshape(1, rows, D_MODEL)
    q, kt, vt, _, _, lf, u_tb = _inproj(xt, w)
    k, v = kt[0].T, vt[0].T
    to_bm = lambda a: a.reshape(n_new, bsz, -1).transpose(1, 0, 2)
    q_b, k_b, v_b, lf_b = to_bm(q), to_bm(k), to_bm(v), to_bm(lf)
    head = jnp.arange(D_ATTN) // HEAD_DIM
    hmask = (head[None, :] == jnp.arange(N_HEADS)[:, None]).astype(BF16)
    qbd = (q_b[:, None, :, :] * hmask[None, :, None, :]).reshape(bsz, N_HEADS * n_new, D_ATTN)
    pad_keys = ((0, 0), (0, PAGE_SIZE - n_new), (0, 0))
    n_pool = cache_k.shape[1]
    keys_minor = lambda c: c[0].transpose(0, 2, 3, 1).reshape(n_pool, D_ATTN, PAGE_SIZE)
    attn = _attn_sample(
        qbd, keys_minor(cache_k), keys_minor(cache_v),
        cache_logf[0].astype(F32).transpose(0, 2, 1), page_table.astype(jnp.int32),
        jnp.pad(k_b, pad_keys).transpose(0, 2, 1), jnp.pad(v_b, pad_keys).transpose(0, 2, 1),
        jnp.pad(lf_b.transpose(0, 2, 1), ((0, 0), (0, 0), (0, PAGE_SIZE - n_new))), n_new)
    attn_tm = attn.transpose(1, 0, 2).reshape(rows, D_ATTN)
    h0 = jnp.concatenate([h0_re.reshape(bsz, D_STATE), h0_im.reshape(bsz, D_STATE)], axis=1).astype(F32)
    ssm_lb, hlast = _ssm(u_tb.reshape(n_new, bsz * D_SSM), h0, w, bsz, n_new)
    xres, hn, logits = _outproj(attn_tm, ssm_lb.reshape(rows, D_SSM), xt, w)
    caches = (k_b.reshape(1, bsz, n_new, N_HEADS, HEAD_DIM), v_b.reshape(1, bsz, n_new, N_HEADS, HEAD_DIM),
              lf_b.reshape(1, bsz, n_new, N_HEADS),
              hlast[:, :D_STATE].reshape(1, bsz, N_SSM_GROUPS, SSM_STATE),
              hlast[:, D_STATE:].reshape(1, bsz, N_SSM_GROUPS, SSM_STATE))
    return xres, hn, logits, caches


def _moe(hn, logits, w):
    n_tok = hn.shape[1]
    tri = (jnp.arange(ROUTE_TILE)[:, None] > jnp.arange(ROUTE_TILE)[None, :]).astype(BF16)
    idx, gates, rank, counts = _route(logits, w['router_bias'], tri)
    counts = counts[0, :N_EXPERTS].astype(jnp.int32)
    padded = (counts + FFN_BLOCK - 1) // FFN_BLOCK * FFN_BLOCK
    pend = jnp.cumsum(padded)
    pstart = pend - padded
    n_blk = -(-(n_tok * TOP_K) // FFN_BLOCK) + N_EXPERTS
    n_slots = n_blk * FFN_BLOCK
    pstart_row = jnp.pad(pstart, (0, LANES - N_EXPERTS)).reshape(1, LANES)
    pos = _slot_positions(idx, rank, pstart_row).reshape(-1)
    blk_start = jnp.arange(n_blk, dtype=jnp.int32) * FFN_BLOCK
    blk_expert = jnp.minimum(jnp.sum(pend[None, :] <= blk_start[:, None], axis=1), N_EXPERTS - 1).astype(jnp.int32)
    blk_valid = jnp.clip(pstart[blk_expert] + counts[blk_expert] - blk_start, 0, FFN_BLOCK).astype(jnp.int32)
    plane_pos = (pos[None, :] + n_slots * jnp.arange(N_PLANES, dtype=jnp.int32)[:, None]).reshape(-1)
    xs = _scatter_rows(hn.reshape(N_PLANES * n_tok, LANES), plane_pos, N_PLANES * n_slots, n_tok)
    ys = _expert_ffn(xs.reshape(N_PLANES, n_slots, LANES), blk_expert, blk_valid,
                     w['w_gate_e'], w['w_up_e'], w['w_down_e'])
    ysg = _gather_rows(ys.reshape(N_PLANES * n_slots, LANES), plane_pos)
    return gates, ysg.reshape(N_PLANES, TOP_K * n_tok, LANES)


def kernel(x_prompt, x_sample, cache_k, cache_v, cache_logf, page_table, state_ssm_re, state_ssm_im, norm_mix_w, w_in, b_forget, q_norm_w, k_norm_w, lambda_re, lambda_im, log_dt, b_re, b_im, c_re, c_im, d_skip, w_glu, b_glu, attn_out_norm_w, ssm_out_norm_w, w_out, norm_ffn_w, w_router, router_bias, w_gate_e, w_up_e, w_down_e, w_gate_s, w_up_s, w_down_s):
    assert norm_mix_w.shape[0] == 1, "single-layer trunk"
    w = _prepare_weights(norm_mix_w[0], w_in[0], b_forget[0], q_norm_w[0], k_norm_w[0], lambda_re[0], lambda_im[0],
                         log_dt[0], b_re[0], b_im[0], c_re[0], c_im[0], d_skip[0], w_glu[0], b_glu[0],
                         attn_out_norm_w[0], ssm_out_norm_w[0], w_out[0], norm_ffn_w[0], w_router[0], router_bias[0],
                         w_gate_e[0], w_up_e[0], w_down_e[0], w_gate_s[0], w_up_s[0], w_down_s[0])
    bp, lp, _ = x_prompt.shape
    bs, ls, _ = x_sample.shape
    xres_p, hn_p, lg_p, caches_p = _mix_prompt(x_prompt, w)
    xres_s, hn_s, lg_s, caches_s = _mix_sample(x_sample, cache_k, cache_v, cache_logf, page_table,
                                               state_ssm_re[0], state_ssm_im[0], w)
    n_p = bp * lp
    n_tok = n_p + bs * ls
    gates, ysg = _moe(jnp.concatenate([hn_p, hn_s], axis=1), jnp.concatenate([lg_p, lg_s], axis=0), w)
    y_p = _combine(xres_p, gates, ysg, 0, n_tok).reshape(bp, lp, D_MODEL)
    y_s = _combine(xres_s, gates, ysg, n_p, n_tok).reshape(ls, bs, D_MODEL).transpose(1, 0, 2)
    return (y_p, y_s) + caches_p + caches_s
```

```python
import functools
import math

import jax
import jax.numpy as jnp
from jax import lax
from jax.experimental import pallas as pl
from jax.experimental.pallas import tpu as pltpu
from jax.experimental.pallas import tpu_sc as plsc

F32 = jnp.float32
BF16 = jnp.bfloat16

D_MODEL = 1024
D_ATTN = 512
D_SSM = 512
HEAD_DIM = 64
N_HEADS = 8
ATTN_SCALE = HEAD_DIM ** -0.5
LOG2E = math.log2(math.e)
SSM_GROUP = 16
N_SSM_GROUPS = 32
SSM_STATE = 64
D_STATE = N_SSM_GROUPS * SSM_STATE
N_EXPERTS = 64
TOP_K = 8
D_EXPERT = 256
D_SHARED = 256
ROUTE_SCALE = 2.5
PAGE_SIZE = 128
RMS_EPS = 1e-6

LANES = 128
VMEM_LIMIT = 56 * 1024 * 1024

ROW_TILE = 512
ATTN_TILE = 512
PAGES_PER_STEP = 32
PAGES_PER_CHUNK = 32
ROUTE_TILE = 256
FFN_BLOCK = 512
PROMPT_GROUPS = 2
FFN_BLOCK_SMALL = 64
COMBINE_TILE = 256
SC_WINDOW = 128


def _params(*sem):
    return pltpu.CompilerParams(dimension_semantics=sem, vmem_limit_bytes=VMEM_LIMIT)


def _rms(x, w):
    return x * lax.rsqrt(jnp.mean(x * x, axis=-1, keepdims=True) + RMS_EPS) * w


def _lane_cumsum(x):
    n = x.shape[1]
    lane = lax.broadcasted_iota(jnp.int32, x.shape, 1)
    s = 1
    while s < n:
        x = x + jnp.where(lane >= s, pltpu.roll(x, s, axis=1), 0.0)
        s *= 2
    return x


def _inproj_body(x_ref, nw_ref, wqkv_ref, wf_ref, wu_ref, bf_ref, qnw_ref, knw_ref, seg_ref,
                 q_ref, k_ref, v_ref, kb_ref, vb_ref, lf_ref, u_ref):
    hn = _rms(x_ref[0], nw_ref[...]).astype(BF16)
    qkv = jnp.dot(hn, wqkv_ref[...], preferred_element_type=F32)
    seg = seg_ref[...]

    def head_norm(t, w):
        t2 = t * t
        hi = t2.astype(BF16)
        lo = (t2 - hi.astype(F32)).astype(BF16)
        ms = jnp.dot(hi, seg, preferred_element_type=F32) + jnp.dot(lo, seg, preferred_element_type=F32)
        return t * lax.rsqrt(ms + RMS_EPS) * w

    q = head_norm(qkv[:, :D_ATTN], qnw_ref[...])
    k = head_norm(qkv[:, D_ATTN:2 * D_ATTN], knw_ref[...])
    v = qkv[:, 2 * D_ATTN:]
    q_ref[...] = (q * (ATTN_SCALE * LOG2E)).astype(BF16)
    k_ref[0] = k.T
    v_ref[0] = v.T
    kb_ref[...] = k.astype(BF16)
    vb_ref[...] = v.astype(BF16)
    z = jnp.dot(hn, wf_ref[...], preferred_element_type=F32) + bf_ref[...]
    lf = jnp.minimum(z, 0.0) - jnp.log(1.0 + jnp.exp(-jnp.abs(z)))
    lf_ref[...] = lf[:, :N_HEADS]
    u_ref[...] = jnp.dot(hn, wu_ref[...], preferred_element_type=F32)


def _inproj(x3, w):
    bsz, length, _ = x3.shape
    tm = min(ROW_TILE, length)
    nt = length // tm
    rows = bsz * length
    row_map = lambda b, t: (b * nt + t, 0)
    const = lambda b, t: (0, 0)
    full = lambda a: pl.BlockSpec(a.shape, const)
    consts = [w['norm_mix_w'], w['wqkv'], w['wf'], w['wu'], w['b_forget'], w['q_norm_w'], w['k_norm_w'], w['seg']]
    return pl.pallas_call(
        _inproj_body,
        grid=(bsz, nt),
        in_specs=[pl.BlockSpec((1, tm, D_MODEL), lambda b, t: (b, t, 0))] + [full(a) for a in consts],
        out_specs=[pl.BlockSpec((tm, D_ATTN), row_map)]
        + [pl.BlockSpec((1, D_ATTN, tm), lambda b, t: (b, 0, t))] * 2
        + [pl.BlockSpec((tm, D_ATTN), row_map)] * 2
        + [pl.BlockSpec((tm, N_HEADS), row_map), pl.BlockSpec((tm, D_SSM), lambda b, t: (t, b))],
        out_shape=[jax.ShapeDtypeStruct((rows, D_ATTN), BF16),
                   jax.ShapeDtypeStruct((bsz, D_ATTN, length), F32),
                   jax.ShapeDtypeStruct((bsz, D_ATTN, length), F32),
                   jax.ShapeDtypeStruct((rows, D_ATTN), BF16),
                   jax.ShapeDtypeStruct((rows, D_ATTN), BF16),
                   jax.ShapeDtypeStruct((rows, N_HEADS), F32),
                   jax.ShapeDtypeStruct((length, bsz * D_SSM), F32)],
        compiler_params=_params("parallel", "parallel"),
        name="inproj",
    )(x3, *consts)


def _cumsum_body(lf_ref, c_ref):
    c_ref[0] = _lane_cumsum(lf_ref[0])


def _cumsum_lanes(lft):
    bsz, nh, length = lft.shape
    spec = pl.BlockSpec((1, nh, length), lambda b: (b, 0, 0))
    return pl.pallas_call(
        _cumsum_body, grid=(bsz,), in_specs=[spec], out_specs=spec,
        out_shape=jax.ShapeDtypeStruct(lft.shape, F32),
        compiler_params=_params("parallel"), name="logf_cumsum",
    )(lft)


def _attn_prompt_body(q_ref, k_ref, v_ref, ct_ref, o_ref, acc_ref, m_ref, *, tile):
    i = pl.program_id(1)
    lane = lax.broadcasted_iota(jnp.int32, (1, LANES), 1)
    row = lax.broadcasted_iota(jnp.int32, (tile, tile), 0)
    col = lax.broadcasted_iota(jnp.int32, (tile, tile), 1)
    hmasks = (lane < HEAD_DIM, lane >= HEAD_DIM)
    sum_lane = (HEAD_DIM, 0)
    ones_col = [jnp.where(lane == sl, 1.0, 0.0).astype(BF16) for sl in sum_lane]
    for hp in range(N_HEADS // 2):
        lanes = slice(LANES * hp, LANES * (hp + 1))
        qp = q_ref[:, lanes]
        qh = [jnp.where(hm, qp, jnp.zeros_like(qp)) for hm in hmasks]
        m_ref[...] = jnp.full(m_ref.shape, -jnp.inf, F32)
        acc_ref[...] = jnp.zeros(acc_ref.shape, F32)

        def kstep(j, diagonal, qh=qh, lanes=lanes, hp=hp):
            r0 = pl.multiple_of(j * tile, tile)
            kj = k_ref[pl.ds(r0, tile), lanes]
            vj = v_ref[pl.ds(r0, tile), lanes]
            for hh in range(2):
                s = lax.dot_general(qh[hh], kj, (((1,), (1,)), ((), ())), preferred_element_type=F32)
                s = s - ct_ref[0, 2 * hp + hh:2 * hp + hh + 1, pl.ds(r0, tile)] * LOG2E
                if diagonal:
                    s = jnp.where(row >= col, s, -jnp.inf)
                m_old = m_ref[hh]
                m_new = jnp.maximum(m_old, jnp.max(s, axis=1, keepdims=True))
                alpha = jnp.exp2(m_old - m_new)
                p = jnp.exp2(s - m_new).astype(BF16)
                vh = jnp.where(hmasks[hh], vj, ones_col[hh])
                acc_ref[hh] = alpha * acc_ref[hh] + jnp.dot(p, vh, preferred_element_type=F32)
                m_ref[hh] = m_new

        def body(j, carry):
            kstep(j, False)
            return carry

        lax.fori_loop(0, i, body, 0)
        kstep(i, True)
        acc0, acc1 = acc_ref[0], acc_ref[1]
        o_ref[:, lanes] = jnp.where(hmasks[0], acc0 / acc0[:, sum_lane[0]:sum_lane[0] + 1],
                                    acc1 / acc1[:, sum_lane[1]:sum_lane[1] + 1])


def _attn_prompt(q, kb, vb, ct, bsz, length):
    tile = min(ATTN_TILE, length)
    nq = length // tile
    q_map = lambda b, i: (b * nq + i, 0)
    seq_map = lambda b, i: (b, 0)
    return pl.pallas_call(
        functools.partial(_attn_prompt_body, tile=tile),
        grid=(bsz, nq),
        in_specs=[pl.BlockSpec((tile, D_ATTN), q_map),
                  pl.BlockSpec((length, D_ATTN), seq_map),
                  pl.BlockSpec((length, D_ATTN), seq_map),
                  pl.BlockSpec((1, N_HEADS, length), lambda b, i: (b, 0, 0))],
        out_specs=pl.BlockSpec((tile, D_ATTN), q_map),
        out_shape=jax.ShapeDtypeStruct((bsz * length, D_ATTN), F32),
        scratch_shapes=[pltpu.VMEM((2, tile, LANES), F32), pltpu.VMEM((2, tile, 1), F32)],
        compiler_params=_params("parallel", "parallel"),
        name="attn_prompt",
    )(q, kb, vb, ct)


def _attn_sample_body(pt_ref, q_ref, *refs, pages, n_new):
    del pt_ref
    k_refs, v_refs, lf_refs = refs[:pages], refs[pages:2 * pages], refs[2 * pages:3 * pages]
    kn_ref, vn_ref, lfn_ref, o_ref, m_ref, l_ref, acc_ref, c_ref = refs[3 * pages:]
    j = pl.program_id(1)
    rows = N_HEADS * n_new

    @pl.when(j == 0)
    def _():
        m_ref[...] = jnp.full(m_ref.shape, -jnp.inf, F32)
        l_ref[...] = jnp.zeros(l_ref.shape, F32)
        acc_ref[...] = jnp.zeros(acc_ref.shape, F32)
        c_ref[...] = jnp.zeros(c_ref.shape, F32)

    q = q_ref[0]

    def chunk(kc, vc, lft, valid):
        n = kc.shape[1]
        s = jnp.dot(q, kc, preferred_element_type=F32)
        ck = _lane_cumsum(lft) + c_ref[:, 0:1]
        c_ref[...] = jnp.broadcast_to(ck[:, n - 1:n], c_ref.shape)
        ck2 = ck * LOG2E
        s = s - jnp.concatenate([jnp.broadcast_to(ck2[h:h + 1], (n_new, n)) for h in range(N_HEADS)], axis=0)
        if valid is not None:
            s = jnp.where(valid, s, -jnp.inf)
        m_old = m_ref[...]
        m_new = jnp.maximum(m_old, jnp.max(s, axis=1, keepdims=True))
        alpha = jnp.exp2(m_old - m_new)
        p = jnp.exp2(s - m_new)
        l_ref[...] = alpha * l_ref[...] + jnp.sum(p, axis=1, keepdims=True)
        acc_ref[...] = alpha * acc_ref[...] + lax.dot_general(
            p.astype(BF16), vc, (((1,), (1,)), ((), ())), preferred_element_type=F32)
        m_ref[...] = m_new

    for g in range(0, pages, PAGES_PER_CHUNK):
        grp = slice(g, g + PAGES_PER_CHUNK)
        chunk(jnp.concatenate([r[0] for r in k_refs[grp]], axis=1).astype(BF16),
              jnp.concatenate([r[0] for r in v_refs[grp]], axis=1).astype(BF16),
              jnp.concatenate([r[0] for r in lf_refs[grp]], axis=1), None)

    @pl.when(j == pl.num_programs(1) - 1)
    def _():
        qi = lax.broadcasted_iota(jnp.int32, (rows, PAGE_SIZE), 0) % n_new
        kj = lax.broadcasted_iota(jnp.int32, (rows, PAGE_SIZE), 1)
        chunk(kn_ref[0].astype(BF16), vn_ref[0].astype(BF16), lfn_ref[0], kj <= qi)
        o_full = acc_ref[...] / l_ref[...]
        lane = lax.broadcasted_iota(jnp.int32, (1, D_ATTN), 1)
        out = jnp.zeros((n_new, D_ATTN), F32)
        for h in range(N_HEADS):
            hmask = (lane >= HEAD_DIM * h) & (lane < HEAD_DIM * (h + 1))
            out = out + jnp.where(hmask, o_full[n_new * h:n_new * (h + 1)], 0.0)
        o_ref[0] = out


def _attn_sample(qbd, cache_k, cache_v, cache_lft, page_table, kn, vn, lfn, n_new):
    bsz, n_pages = page_table.shape
    pages = min(PAGES_PER_STEP, n_pages)
    nj = n_pages // pages
    rows = N_HEADS * n_new

    def page_map(p):
        return lambda b, j, pt: (pt[b * n_pages + j * pages + p], 0, 0)

    seq_map = lambda b, j, pt: (b, 0, 0)
    in_specs = [pl.BlockSpec((1, rows, D_ATTN), seq_map)]
    in_specs += [pl.BlockSpec((1, D_ATTN, PAGE_SIZE), page_map(p)) for p in range(pages)]
    in_specs += [pl.BlockSpec((1, D_ATTN, PAGE_SIZE), page_map(p)) for p in range(pages)]
    in_specs += [pl.BlockSpec((1, N_HEADS, PAGE_SIZE), page_map(p)) for p in range(pages)]
    in_specs += [pl.BlockSpec((1, D_ATTN, PAGE_SIZE), seq_map)] * 2 + [pl.BlockSpec((1, N_HEADS, PAGE_SIZE), seq_map)]
    return pl.pallas_call(
        functools.partial(_attn_sample_body, pages=pages, n_new=n_new),
        grid_spec=pltpu.PrefetchScalarGridSpec(
            num_scalar_prefetch=1, grid=(bsz, nj), in_specs=in_specs,
            out_specs=pl.BlockSpec((1, n_new, D_ATTN), seq_map),
            scratch_shapes=[pltpu.VMEM((rows, 1), F32), pltpu.VMEM((rows, 1), F32),
                            pltpu.VMEM((rows, D_ATTN), F32), pltpu.VMEM((N_HEADS, LANES), F32)]),
        out_shape=jax.ShapeDtypeStruct((bsz, n_new, D_ATTN), F32),
        compiler_params=_params("parallel", "arbitrary"),
        name="attn_sample",
    )(page_table.reshape(-1), qbd, *([cache_k] * pages), *([cache_v] * pages), *([cache_lft] * pages), kn, vn, lfn)


SCAN_LANES = 512


def _ssm_body(u_ref, h0_ref, bw_ref, cre_ref, cim_ref, ar_ref, ai_ref, dsk_ref, wglu_ref, bglu_ref,
              out_ref, hlast_ref, hbuf_ref, state_ref, ubuf_ref, *, steps, bsz):
    g = pl.program_id(0)

    @pl.when(g == 0)
    def _():
        state_ref[...] = h0_ref[...]

    n_tiles = D_SSM // LANES
    for b in range(bsz):
        for t in range(n_tiles):
            col = D_SSM * b + LANES * t
            ubuf_ref[t, pl.ds(b, steps, stride=bsz), :] = u_ref[:, col:col + LANES]
    u = jnp.concatenate([ubuf_ref[t] for t in range(n_tiles)], axis=1)
    ub = u.astype(BF16)
    for p in range(N_SSM_GROUPS // 2):
        t = p // 4
        bu = jnp.dot(ub[:, LANES * t:LANES * (t + 1)], bw_ref[p], preferred_element_type=F32)
        hbuf_ref[:, LANES * p:LANES * (p + 1)] = bu[:, :LANES]
        hbuf_ref[:, D_STATE + LANES * p:D_STATE + LANES * (p + 1)] = bu[:, LANES:]

    for c in range(D_STATE // SCAN_LANES):
        re_l = slice(SCAN_LANES * c, SCAN_LANES * (c + 1))
        im_l = slice(D_STATE + SCAN_LANES * c, D_STATE + SCAN_LANES * (c + 1))
        ar = jnp.broadcast_to(ar_ref[:, re_l], (bsz, SCAN_LANES))
        ai = jnp.broadcast_to(ai_ref[:, re_l], (bsz, SCAN_LANES))

        def step(t, carry, re_l=re_l, im_l=im_l, ar=ar, ai=ai):
            re, im = carry
            r0 = pl.multiple_of(t * bsz, bsz)
            nre = ar * re - ai * im + hbuf_ref[pl.ds(r0, bsz), re_l]
            nim = ar * im + ai * re + hbuf_ref[pl.ds(r0, bsz), im_l]
            hbuf_ref[pl.ds(r0, bsz), re_l] = nre
            hbuf_ref[pl.ds(r0, bsz), im_l] = nim
            return nre, nim

        re, im = lax.fori_loop(0, steps, step, (state_ref[:, re_l], state_ref[:, im_l]))
        state_ref[:, re_l] = re
        state_ref[:, im_l] = im

    hlast_ref[...] = state_ref[...]

    ys = []
    for t in range(D_SSM // LANES):
        w = 4 * LANES
        hre = hbuf_ref[:, w * t:w * (t + 1)].astype(BF16)
        him = hbuf_ref[:, D_STATE + w * t:D_STATE + w * (t + 1)].astype(BF16)
        ys.append(jnp.dot(hre, cre_ref[t], preferred_element_type=F32)
                  + jnp.dot(him, cim_ref[t], preferred_element_type=F32))
    y = jnp.concatenate(ys, axis=1) + dsk_ref[...] * u
    z = 0.5 * y * (1.0 + jnp.tanh(math.sqrt(2.0 / math.pi) * (y + 0.044715 * (y * y * y))))
    gate = jnp.dot(z.astype(BF16), wglu_ref[...], preferred_element_type=F32) + bglu_ref[...]
    out = z / (1.0 + jnp.exp(-gate))
    for t in range(n_tiles):
        ubuf_ref[t] = out[:, LANES * t:LANES * (t + 1)]
    for b in range(bsz):
        for t in range(n_tiles):
            col = D_SSM * b + LANES * t
            out_ref[:, col:col + LANES] = ubuf_ref[t, pl.ds(b, steps, stride=bsz), :]


def _ssm(u_lb, h0, w, bsz, length):
    steps = max(1, min(length, ROW_TILE // bsz))
    rows = steps * bsz
    const2 = lambda g: (0, 0)
    const3 = lambda g: (0, 0, 0)
    full = lambda a: pl.BlockSpec(a.shape, const2 if a.ndim == 2 else const3)
    consts = [h0, w['ssm_bw'], w['ssm_cre'], w['ssm_cim'], w['ssm_ar'], w['ssm_ai'], w['d_skip'], w['w_glu'], w['b_glu']]
    return pl.pallas_call(
        functools.partial(_ssm_body, steps=steps, bsz=bsz),
        grid=(length // steps,),
        in_specs=[pl.BlockSpec((steps, bsz * D_SSM), lambda g: (g, 0))] + [full(a) for a in consts],
        out_specs=[pl.BlockSpec((steps, bsz * D_SSM), lambda g: (g, 0)), pl.BlockSpec((bsz, 2 * D_STATE), const2)],
        out_shape=[jax.ShapeDtypeStruct((length, bsz * D_SSM), F32), jax.ShapeDtypeStruct((bsz, 2 * D_STATE), F32)],
        scratch_shapes=[pltpu.VMEM((rows, 2 * D_STATE), F32), pltpu.VMEM((bsz, 2 * D_STATE), F32),
                        pltpu.VMEM((D_SSM // LANES, rows, LANES), F32)],
        compiler_params=_params("arbitrary"),
        name="ssm",
    )(u_lb, *consts)


N_PLANES = D_MODEL // (2 * LANES)


def _pack_rows(x, out_ref):
    bits = lax.bitcast_convert_type(x.astype(BF16).astype(F32), jnp.uint32)
    for c in range(N_PLANES):
        lo = bits[:, 2 * LANES * c:2 * LANES * c + LANES]
        hi = bits[:, 2 * LANES * c + LANES:2 * LANES * (c + 1)]
        out_ref[c] = lax.bitcast_convert_type(hi | (lo >> 16), jnp.int32)


def _unpack_rows(ref):
    parts = []
    for c in range(N_PLANES):
        bits = lax.bitcast_convert_type(ref[c], jnp.uint32)
        parts.append(lax.bitcast_convert_type(bits << 16, F32))
        parts.append(lax.bitcast_convert_type(bits & jnp.uint32(0xFFFF0000), F32))
    return parts


def _outproj_body(attn_ref, ssm_ref, x_ref, aw_ref, sw_ref, woa_ref, wos_ref, nfw_ref, wrh_ref, wrl_ref,
                  wgus_ref, wds_ref, xres_ref, hn_ref, logit_ref):
    an = _rms(attn_ref[...], aw_ref[...]).astype(BF16)
    sn = _rms(ssm_ref[...], sw_ref[...]).astype(BF16)
    x1 = x_ref[0] + (jnp.dot(an, woa_ref[...], preferred_element_type=F32)
                     + jnp.dot(sn, wos_ref[...], preferred_element_type=F32))
    h2 = _rms(x1, nfw_ref[...])
    hb = h2.astype(BF16)
    hlo = (h2 - hb.astype(F32)).astype(BF16)
    _pack_rows(h2, hn_ref)
    logit_ref[...] = (jnp.dot(hb, wrh_ref[...], preferred_element_type=F32)
                      + (jnp.dot(hb, wrl_ref[...], preferred_element_type=F32)
                         + jnp.dot(hlo, wrh_ref[...], preferred_element_type=F32)))
    gu = jnp.dot(hb, wgus_ref[...], preferred_element_type=F32)
    gs = gu[:, :D_SHARED]
    act = (gs / (1.0 + jnp.exp(-gs))) * gu[:, D_SHARED:]
    xres_ref[...] = x1 + jnp.dot(act.astype(BF16), wds_ref[...], preferred_element_type=F32)


def _outproj(attn, ssm_tb, x3, w):
    bsz, length, _ = x3.shape
    tm = min(ROW_TILE, length)
    nt = length // tm
    rows = bsz * length
    row_map = lambda b, t: (b * nt + t, 0)
    const = lambda b, t: (0, 0)
    full = lambda a: pl.BlockSpec(a.shape, const)
    consts = [w['attn_out_norm_w'], w['ssm_out_norm_w'], w['wo_a'], w['wo_s'], w['norm_ffn_w'],
              w['wr_hi'], w['wr_lo'], w['wgu_s'], w['wd_s']]
    return pl.pallas_call(
        _outproj_body,
        grid=(bsz, nt),
        in_specs=[pl.BlockSpec((tm, D_ATTN), row_map), pl.BlockSpec((tm, D_SSM), lambda b, t: (t, b)),
                  pl.BlockSpec((1, tm, D_MODEL), lambda b, t: (b, t, 0))] + [full(a) for a in consts],
        out_specs=[pl.BlockSpec((tm, D_MODEL), row_map),
                   pl.BlockSpec((N_PLANES, tm, LANES), lambda b, t: (0, b * nt + t, 0)),
                   pl.BlockSpec((tm, LANES), row_map)],
        out_shape=[jax.ShapeDtypeStruct((rows, D_MODEL), F32),
                   jax.ShapeDtypeStruct((N_PLANES, rows, LANES), jnp.int32),
                   jax.ShapeDtypeStruct((rows, LANES), F32)],
        compiler_params=_params("parallel", "parallel"),
        name="outproj",
    )(attn, ssm_tb, x3, *consts)


def _route_body(logit_ref, bias_ref, tri_ref, idx_ref, gate_ref, rank_ref, cnt_ref, carry_ref):
    @pl.when(pl.program_id(0) == 0)
    def _():
        carry_ref[...] = jnp.zeros(carry_ref.shape, F32)

    lg = logit_ref[...]
    tr = lg.shape[0]
    lane = lax.broadcasted_iota(jnp.int32, (tr, LANES), 1).astype(F32)
    score = 1.0 / (1.0 + jnp.exp(-lg))
    sel = jnp.where(lane < N_EXPERTS, score + bias_ref[...], -jnp.inf)
    picked = []
    gates = jnp.zeros((tr, LANES), F32)
    idxs = jnp.zeros((tr, LANES), F32)
    member = jnp.zeros((tr, LANES), F32)
    for k in range(TOP_K):
        best = jnp.max(sel, axis=1, keepdims=True)
        e = jnp.min(jnp.where(sel == best, lane, float(LANES)), axis=1, keepdims=True)
        hit = lane == e
        picked.append(hit)
        gates = jnp.where(lane == k, jnp.sum(jnp.where(hit, score, 0.0), axis=1, keepdims=True), gates)
        idxs = jnp.where(lane == k, e, idxs)
        member = jnp.where(hit, 1.0, member)
        sel = jnp.where(hit, -jnp.inf, sel)
    gate_ref[...] = ROUTE_SCALE * gates / jnp.sum(gates, axis=1, keepdims=True)
    idx_ref[...] = idxs.astype(jnp.int32)
    before = jnp.dot(tri_ref[...], member.astype(BF16), preferred_element_type=F32) + carry_ref[...]
    ranks = jnp.zeros((tr, LANES), F32)
    for k in range(TOP_K):
        ranks = jnp.where(lane == k, jnp.sum(jnp.where(picked[k], before, 0.0), axis=1, keepdims=True), ranks)
    rank_ref[...] = ranks.astype(jnp.int32)
    carry_ref[...] = carry_ref[...] + jnp.sum(member, axis=0, keepdims=True)
    cnt_ref[...] = carry_ref[...]


def _route(logits, bias, tri, row0, n_tok):
    tr = tri.shape[0]
    off = row0 // tr
    tok_map = lambda i: (i, 0)
    const = lambda i: (0, 0)
    return pl.pallas_call(
        _route_body,
        grid=(n_tok // tr,),
        in_specs=[pl.BlockSpec((tr, LANES), lambda i: (i + off, 0)), pl.BlockSpec((1, LANES), const),
                  pl.BlockSpec((tr, tr), const)],
        out_specs=[pl.BlockSpec((tr, LANES), tok_map)] * 3 + [pl.BlockSpec((1, LANES), const)],
        out_shape=[jax.ShapeDtypeStruct((n_tok, LANES), jnp.int32), jax.ShapeDtypeStruct((n_tok, LANES), F32),
                   jax.ShapeDtypeStruct((n_tok, LANES), jnp.int32), jax.ShapeDtypeStruct((1, LANES), F32)],
        scratch_shapes=[pltpu.VMEM((1, LANES), F32)],
        compiler_params=_params("arbitrary"),
        name="route",
    )(logits, bias, tri)


def _slot_body(idx_ref, rank_ref, pstart_ref, pos_ref):
    idx = idx_ref[...]
    base = jnp.take_along_axis(jnp.broadcast_to(pstart_ref[...], idx.shape), idx, axis=1)
    pos_ref[...] = (base + rank_ref[...]).T[:TOP_K]


def _slot_positions(idx, rank, pstart):
    n_tok = idx.shape[0]
    tr = ROUTE_TILE
    tok_map = lambda i: (i, 0)
    return pl.pallas_call(
        _slot_body,
        grid=(n_tok // tr,),
        in_specs=[pl.BlockSpec((tr, LANES), tok_map), pl.BlockSpec((tr, LANES), tok_map),
                  pl.BlockSpec((1, LANES), lambda i: (0, 0))],
        out_specs=pl.BlockSpec((TOP_K, tr), lambda i: (0, i)),
        out_shape=jax.ShapeDtypeStruct((TOP_K, n_tok), jnp.int32),
        compiler_params=_params("parallel"),
        name="slot_positions",
    )(idx, rank, pstart)


def _gather_rows(table, indices):
    n_idx = indices.shape[0]
    width = table.shape[1]
    mesh = plsc.VectorSubcoreMesh(core_axis_name="core", subcore_axis_name="subcore")

    @pl.kernel(out_type=jax.ShapeDtypeStruct((n_idx, width), table.dtype), mesh=mesh)
    def gather(table_hbm, idx_hbm, out_hbm):
        def body(idx_vmem, out_vmem):
            pltpu.sync_copy(table_hbm.at[idx_vmem.at[0]], out_vmem)

        pltpu.emit_pipeline(
            body,
            grid=(n_idx // SC_WINDOW,),
            in_specs=[pl.BlockSpec((1, SC_WINDOW), lambda i: (0, i))],
            out_specs=[pl.BlockSpec((SC_WINDOW, width), lambda i: (i, 0))],
            core_axis_name=("core", "subcore"),
            dimension_semantics=(pltpu.PARALLEL,),
        )(idx_hbm, out_hbm)

    return gather(table, indices.reshape(1, n_idx))


def _scatter_rows(table, n_planes, row0, n_rows, dest, n_out):
    n_idx = dest.shape[0]
    width = table.shape[1]
    plane_win = table.shape[0] // n_planes // SC_WINDOW
    win0 = row0 // SC_WINDOW
    win_per_plane = n_rows // SC_WINDOW
    win_per_rep_plane = n_idx // SC_WINDOW // n_planes
    mesh = plsc.VectorSubcoreMesh(core_axis_name="core", subcore_axis_name="subcore")

    @pl.kernel(out_type=jax.ShapeDtypeStruct((n_out, width), table.dtype), mesh=mesh)
    def scatter(table_hbm, idx_hbm, out_hbm):
        def body(rows_vmem, idx_vmem):
            pltpu.sync_copy(rows_vmem, out_hbm.at[idx_vmem.at[0]])

        pltpu.emit_pipeline(
            body,
            grid=(n_idx // SC_WINDOW,),
            in_specs=[pl.BlockSpec((SC_WINDOW, width),
                                   lambda i: ((i // win_per_rep_plane) * plane_win + win0 + i % win_per_plane, 0)),
                      pl.BlockSpec((1, SC_WINDOW), lambda i: (0, i))],
            out_specs=[],
            core_axis_name=("core", "subcore"),
            dimension_semantics=(pltpu.PARALLEL,),
        )(table_hbm, idx_hbm)

    return scatter(table, dest.reshape(1, n_idx))


def _ffn_body(be_ref, nv_ref, x_ref, wg_ref, wu_ref, wd_ref, y_ref, wgu_s, wd_s):
    r = pl.program_id(0)
    block = x_ref.shape[1]

    @pl.when((r == 0) | (be_ref[r] != be_ref[jnp.maximum(r - 1, 0)]))
    def _():
        wgu_s[:, :D_EXPERT] = wg_ref[0].astype(BF16)
        wgu_s[:, D_EXPERT:] = wu_ref[0].astype(BF16)
        wd_s[...] = wd_ref[0].astype(BF16)

    n_valid = nv_ref[r]

    def ffn(partial_block):
        x = jnp.concatenate(_unpack_rows(x_ref), axis=1)
        if partial_block:
            x = jnp.where(lax.broadcasted_iota(jnp.int32, (block, 1), 0) < n_valid, x, 0.0)
        gu = jnp.dot(x.astype(BF16), wgu_s[...], preferred_element_type=F32)
        gs = gu[:, :D_EXPERT]
        act = (gs / (1.0 + jnp.exp(-gs))) * gu[:, D_EXPERT:]
        _pack_rows(jnp.dot(act.astype(BF16), wd_s[...], preferred_element_type=F32), y_ref)

    pl.when(n_valid == block)(functools.partial(ffn, False))
    pl.when((n_valid > 0) & (n_valid < block))(functools.partial(ffn, True))

    @pl.when(n_valid == 0)
    def _():
        y_ref[...] = jnp.zeros(y_ref.shape, jnp.int32)


def _expert_ffn(xs, blk_expert, blk_valid, w_gate, w_up, w_down, block):
    n_slots = xs.shape[1]
    nb = n_slots // block
    slot_spec = pl.BlockSpec((N_PLANES, block, LANES), lambda r, be, nv: (0, r, 0))
    return pl.pallas_call(
        _ffn_body,
        grid_spec=pltpu.PrefetchScalarGridSpec(
            num_scalar_prefetch=2, grid=(nb,),
            in_specs=[slot_spec,
                      pl.BlockSpec((1, D_MODEL, D_EXPERT), lambda r, be, nv: (be[r], 0, 0)),
                      pl.BlockSpec((1, D_MODEL, D_EXPERT), lambda r, be, nv: (be[r], 0, 0)),
                      pl.BlockSpec((1, D_EXPERT, D_MODEL), lambda r, be, nv: (be[r], 0, 0))],
            out_specs=slot_spec,
            scratch_shapes=[pltpu.VMEM((D_MODEL, 2 * D_EXPERT), BF16), pltpu.VMEM((D_EXPERT, D_MODEL), BF16)]),
        out_shape=jax.ShapeDtypeStruct((N_PLANES, n_slots, LANES), jnp.int32),
        compiler_params=_params("arbitrary"),
        name="expert_ffn",
    )(blk_expert, blk_valid, xs, w_gate, w_up, w_down)


def _combine_body(xres_ref, gate_ref, *refs):
    y_refs, o_ref = refs[:TOP_K], refs[TOP_K]
    g = gate_ref[...]
    acc = [xres_ref[:, LANES * i:LANES * (i + 1)] for i in range(D_MODEL // LANES)]
    for k in range(TOP_K):
        gk = g[:, k:k + 1]
        acc = [a + gk * p for a, p in zip(acc, _unpack_rows(y_refs[k]))]
    o_ref[...] = jnp.concatenate(acc, axis=1)


def _combine_into_body(xres_ref, gate_ref, *refs):
    _combine_body(xres_ref, gate_ref, *refs[:TOP_K], refs[TOP_K + 1])


def _combine(xres, row0, gates, ysg, out=None):
    n_tok = gates.shape[0]
    tc = min(COMBINE_TILE, n_tok)
    off = row0 // tc
    per_k = n_tok // tc
    row_map = lambda i: (i + off, 0)
    y_specs = [pl.BlockSpec((N_PLANES, tc, LANES), lambda i, k=k: (0, k * per_k + i, 0)) for k in range(TOP_K)]
    in_specs = [pl.BlockSpec((tc, D_MODEL), row_map), pl.BlockSpec((tc, LANES), lambda i: (i, 0))] + y_specs
    args = [xres, gates] + [ysg] * TOP_K
    if out is not None:
        in_specs.append(pl.BlockSpec(memory_space=pl.ANY))
        args.append(out)
    return pl.pallas_call(
        _combine_body if out is None else _combine_into_body,
        grid=(n_tok // tc,),
        in_specs=in_specs,
        out_specs=pl.BlockSpec((tc, D_MODEL), row_map),
        out_shape=jax.ShapeDtypeStruct(xres.shape, F32),
        input_output_aliases={} if out is None else {len(args) - 1: 0},
        compiler_params=_params("parallel"),
        name="combine",
    )(*args)


def _prepare_weights(norm_mix_w, w_in, b_forget, q_norm_w, k_norm_w, lambda_re, lambda_im, log_dt, b_re, b_im,
                     c_re, c_im, d_skip, w_glu, b_glu, attn_out_norm_w, ssm_out_norm_w, w_out, norm_ffn_w,
                     w_router, router_bias, w_gate_e, w_up_e, w_down_e, w_gate_s, w_up_s, w_down_s):
    w = {}
    row = lambda a: a.reshape(1, -1).astype(F32)
    w['norm_mix_w'] = row(norm_mix_w)
    w['wqkv'] = w_in[:, :3 * D_ATTN].astype(BF16)
    w['wf'] = jnp.pad(w_in[:, 3 * D_ATTN:3 * D_ATTN + N_HEADS], ((0, 0), (0, LANES - N_HEADS))).astype(BF16)
    w['wu'] = w_in[:, 3 * D_ATTN + N_HEADS:].astype(BF16)
    w['b_forget'] = jnp.pad(row(b_forget), ((0, 0), (0, LANES - N_HEADS)))
    w['q_norm_w'] = jnp.tile(row(q_norm_w), (1, N_HEADS))
    w['k_norm_w'] = jnp.tile(row(k_norm_w), (1, N_HEADS))
    head = jnp.arange(D_ATTN) // HEAD_DIM
    w['seg'] = jnp.where(head[:, None] == head[None, :], 1.0 / HEAD_DIM, 0.0).astype(BF16)

    dt = jnp.exp(log_dt.astype(F32))[:, None]
    lre, lim = lambda_re.astype(F32), lambda_im.astype(F32)
    a, b = lre * dt, lim * dt
    ea = jnp.exp(a)
    bar_re, bar_im = ea * jnp.cos(b), ea * jnp.sin(b)
    num_re = jnp.expm1(a) * jnp.cos(b) - 2.0 * jnp.sin(0.5 * b) ** 2
    num_im = bar_im
    den = lre * lre + lim * lim
    coef_re = (num_re * lre + num_im * lim) / den
    coef_im = (num_im * lre - num_re * lim) / den
    bb_re = coef_re[:, :, None] * b_re - coef_im[:, :, None] * b_im
    bb_im = coef_re[:, :, None] * b_im + coef_im[:, :, None] * b_re
    eye = jnp.eye(N_SSM_GROUPS, dtype=F32)

    def in_block_diag(m):
        return (m.transpose(0, 2, 1)[:, :, None, :] * eye[:, None, :, None]).reshape(D_SSM, D_STATE)

    def out_block_diag(m):
        return (m.transpose(0, 2, 1)[:, :, None, :] * eye[:, None, :, None]).reshape(D_STATE, D_SSM)

    pairs = jnp.arange(N_SSM_GROUPS // 2)

    def pair_blocks(m):
        return m.reshape(4, LANES, N_SSM_GROUPS // 2, LANES).transpose(2, 0, 1, 3)[pairs, pairs // 4]

    w['ssm_bw'] = jnp.concatenate([pair_blocks(in_block_diag(bb_re)), pair_blocks(in_block_diag(bb_im))],
                                  axis=2).astype(BF16)
    tiles = jnp.arange(D_SSM // LANES)

    def tile_blocks(m):
        return m.reshape(4, 4 * LANES, 4, LANES).transpose(0, 2, 1, 3)[tiles, tiles]

    w['ssm_cre'] = tile_blocks(out_block_diag(c_re.astype(F32))).astype(BF16)
    w['ssm_cim'] = tile_blocks(out_block_diag(-c_im.astype(F32))).astype(BF16)
    w['ssm_ar'] = bar_re.reshape(1, D_STATE)
    w['ssm_ai'] = bar_im.reshape(1, D_STATE)
    w['d_skip'] = row(d_skip)
    w['w_glu'] = w_glu.astype(BF16)
    w['b_glu'] = row(b_glu)

    w['attn_out_norm_w'] = row(attn_out_norm_w)
    w['ssm_out_norm_w'] = row(ssm_out_norm_w)
    w['wo_a'] = w_out[:D_ATTN].astype(BF16)
    w['wo_s'] = w_out[D_ATTN:].astype(BF16)
    w['norm_ffn_w'] = row(norm_ffn_w)
    wr = jnp.pad(w_router.astype(F32), ((0, 0), (0, LANES - N_EXPERTS)))
    w['wr_hi'] = wr.astype(BF16)
    w['wr_lo'] = (wr - w['wr_hi'].astype(F32)).astype(BF16)
    w['router_bias'] = jnp.pad(row(router_bias), ((0, 0), (0, LANES - N_EXPERTS)))
    w['wgu_s'] = jnp.concatenate([w_gate_s, w_up_s], axis=1).astype(BF16)
    w['wd_s'] = w_down_s.astype(BF16)
    w['w_gate_e'], w['w_up_e'], w['w_down_e'] = w_gate_e, w_up_e, w_down_e
    return w


def _mix_prompt(x, w):
    bsz, length, _ = x.shape
    q, kt, vt, kb, vb, lf, u_tb = _inproj(x, w)
    ct = _cumsum_lanes(lf.reshape(bsz, length, N_HEADS).transpose(0, 2, 1))
    attn = _attn_prompt(q, kb, vb, ct, bsz, length)
    h0 = jnp.zeros((bsz, 2 * D_STATE), F32)
    ssm_lb, hlast = _ssm(u_tb, h0, w, bsz, length)
    xres, hn, logits = _outproj(attn, ssm_lb, x, w)
    heads_last = lambda a: a.reshape(1, bsz, N_HEADS, HEAD_DIM, length).transpose(0, 1, 4, 2, 3)
    caches = (heads_last(kt), heads_last(vt),
              lf.reshape(1, bsz, length, N_HEADS),
              hlast[:, :D_STATE].reshape(1, bsz, N_SSM_GROUPS, SSM_STATE),
              hlast[:, D_STATE:].reshape(1, bsz, N_SSM_GROUPS, SSM_STATE))
    return xres, hn, logits, caches


def _mix_sample(x, cache_k, cache_v, cache_logf, page_table, h0_re, h0_im, w):
    bsz, n_new, _ = x.shape
    rows = n_new * bsz
    xt = x.transpose(1, 0, 2).reshape(1, rows, D_MODEL)
    q, kt, vt, _, _, lf, u_tb = _inproj(xt, w)
    k, v = kt[0].T, vt[0].T
    to_bm = lambda a: a.reshape(n_new, bsz, -1).transpose(1, 0, 2)
    q_b, k_b, v_b, lf_b = to_bm(q), to_bm(k), to_bm(v), to_bm(lf)
    head = jnp.arange(D_ATTN) // HEAD_DIM
    hmask = (head[None, :] == jnp.arange(N_HEADS)[:, None]).astype(BF16)
    qbd = (q_b[:, None, :, :] * hmask[None, :, None, :]).reshape(bsz, N_HEADS * n_new, D_ATTN)
    pad_keys = ((0, 0), (0, PAGE_SIZE - n_new), (0, 0))
    n_pool = cache_k.shape[1]
    keys_minor = lambda c: c[0].transpose(0, 2, 3, 1).reshape(n_pool, D_ATTN, PAGE_SIZE)
    attn = _attn_sample(
        qbd, keys_minor(cache_k), keys_minor(cache_v),
        cache_logf[0].astype(F32).transpose(0, 2, 1), page_table.astype(jnp.int32),
        jnp.pad(k_b, pad_keys).transpose(0, 2, 1), jnp.pad(v_b, pad_keys).transpose(0, 2, 1),
        jnp.pad(lf_b.transpose(0, 2, 1), ((0, 0), (0, 0), (0, PAGE_SIZE - n_new))), n_new)
    attn_tm = attn.transpose(1, 0, 2).reshape(rows, D_ATTN)
    h0 = jnp.concatenate([h0_re.reshape(bsz, D_STATE), h0_im.reshape(bsz, D_STATE)], axis=1).astype(F32)
    ssm_lb, hlast = _ssm(u_tb.reshape(n_new, bsz * D_SSM), h0, w, bsz, n_new)
    xres, hn, logits = _outproj(attn_tm, ssm_lb.reshape(rows, D_SSM), xt, w)
    caches = (k_b.reshape(1, bsz, n_new, N_HEADS, HEAD_DIM), v_b.reshape(1, bsz, n_new, N_HEADS, HEAD_DIM),
              lf_b.reshape(1, bsz, n_new, N_HEADS),
              hlast[:, :D_STATE].reshape(1, bsz, N_SSM_GROUPS, SSM_STATE),
              hlast[:, D_STATE:].reshape(1, bsz, N_SSM_GROUPS, SSM_STATE))
    return xres, hn, logits, caches


def _moe(hn, logits, w, block, row0, n_tok):
    tri = (jnp.arange(ROUTE_TILE)[:, None] > jnp.arange(ROUTE_TILE)[None, :]).astype(BF16)
    idx, gates, rank, counts = _route(logits, w['router_bias'], tri, row0, n_tok)
    counts = counts[0, :N_EXPERTS].astype(jnp.int32)
    padded = (counts + block - 1) // block * block
    pend = jnp.cumsum(padded)
    pstart = pend - padded
    n_blk = -(-(n_tok * TOP_K) // block) + N_EXPERTS
    n_slots = n_blk * block
    pstart_row = jnp.pad(pstart, (0, LANES - N_EXPERTS)).reshape(1, LANES)
    pos = _slot_positions(idx, rank, pstart_row).reshape(-1)
    blk_start = jnp.arange(n_blk, dtype=jnp.int32) * block
    blk_expert = jnp.minimum(jnp.sum(pend[None, :] <= blk_start[:, None], axis=1), N_EXPERTS - 1).astype(jnp.int32)
    blk_valid = jnp.clip(pstart[blk_expert] + counts[blk_expert] - blk_start, 0, block).astype(jnp.int32)
    plane_pos = (pos[None, :] + n_slots * jnp.arange(N_PLANES, dtype=jnp.int32)[:, None]).reshape(-1)
    xs = _scatter_rows(hn.reshape(N_PLANES * hn.shape[1], LANES), N_PLANES, row0, n_tok, plane_pos,
                       N_PLANES * n_slots)
    ys = _expert_ffn(xs.reshape(N_PLANES, n_slots, LANES), blk_expert, blk_valid,
                     w['w_gate_e'], w['w_up_e'], w['w_down_e'], block)
    ysg = _gather_rows(ys.reshape(N_PLANES * n_slots, LANES), plane_pos)
    return gates, ysg.reshape(N_PLANES, TOP_K * n_tok, LANES)


def kernel(x_prompt, x_sample, cache_k, cache_v, cache_logf, page_table, state_ssm_re, state_ssm_im, norm_mix_w, w_in, b_forget, q_norm_w, k_norm_w, lambda_re, lambda_im, log_dt, b_re, b_im, c_re, c_im, d_skip, w_glu, b_glu, attn_out_norm_w, ssm_out_norm_w, w_out, norm_ffn_w, w_router, router_bias, w_gate_e, w_up_e, w_down_e, w_gate_s, w_up_s, w_down_s):
    assert norm_mix_w.shape[0] == 1, "single-layer trunk"
    w = _prepare_weights(norm_mix_w[0], w_in[0], b_forget[0], q_norm_w[0], k_norm_w[0], lambda_re[0], lambda_im[0],
                         log_dt[0], b_re[0], b_im[0], c_re[0], c_im[0], d_skip[0], w_glu[0], b_glu[0],
                         attn_out_norm_w[0], ssm_out_norm_w[0], w_out[0], norm_ffn_w[0], w_router[0], router_bias[0],
                         w_gate_e[0], w_up_e[0], w_down_e[0], w_gate_s[0], w_up_s[0], w_down_s[0])
    bp, lp, _ = x_prompt.shape
    bs, ls, _ = x_sample.shape
    xres_p, hn_p, lg_p, caches_p = _mix_prompt(x_prompt, w)
    xres_s, hn_s, lg_s, caches_s = _mix_sample(x_sample, cache_k, cache_v, cache_logf, page_table,
                                               state_ssm_re[0], state_ssm_im[0], w)
    y_p = None
    group = bp * lp // PROMPT_GROUPS
    for g in range(PROMPT_GROUPS):
        gates, ysg = _moe(hn_p, lg_p, w, FFN_BLOCK, g * group, group)
        y_p = _combine(xres_p, g * group, gates, ysg, out=y_p)
    gates, ysg = _moe(hn_s, lg_s, w, FFN_BLOCK_SMALL, 0, bs * ls)
    y_s = _combine(xres_s, 0, gates, ysg).reshape(ls, bs, D_MODEL).transpose(1, 0, 2)
    y_p = y_p.reshape(bp, lp, D_MODEL)
    return (y_p, y_s) + caches_p + caches_s
```

```python
import functools
import math

import jax
import jax.numpy as jnp
from jax import lax
from jax.experimental import pallas as pl
from jax.experimental.pallas import tpu as pltpu
from jax.experimental.pallas import tpu_sc as plsc

F32 = jnp.float32
BF16 = jnp.bfloat16

D_MODEL = 1024
D_ATTN = 512
D_SSM = 512
HEAD_DIM = 64
N_HEADS = 8
ATTN_SCALE = HEAD_DIM ** -0.5
LOG2E = math.log2(math.e)
SSM_GROUP = 16
N_SSM_GROUPS = 32
SSM_STATE = 64
D_STATE = N_SSM_GROUPS * SSM_STATE
N_EXPERTS = 64
TOP_K = 8
D_EXPERT = 256
D_SHARED = 256
ROUTE_SCALE = 2.5
PAGE_SIZE = 128
RMS_EPS = 1e-6

LANES = 128
VMEM_LIMIT = 56 * 1024 * 1024

ROW_TILE = 512
ATTN_TILE = 512
PAGES_PER_STEP = 32
PAGES_PER_CHUNK = 32
ROUTE_TILE = 256
FFN_BLOCK = 512
MOE_GROUPS = 2
COMBINE_TILE = 256
SC_WINDOW = 128


def _params(*sem):
    return pltpu.CompilerParams(dimension_semantics=sem, vmem_limit_bytes=VMEM_LIMIT)


def _rms(x, w):
    return x * lax.rsqrt(jnp.mean(x * x, axis=-1, keepdims=True) + RMS_EPS) * w


def _lane_cumsum(x):
    n = x.shape[1]
    lane = lax.broadcasted_iota(jnp.int32, x.shape, 1)
    s = 1
    while s < n:
        x = x + jnp.where(lane >= s, pltpu.roll(x, s, axis=1), 0.0)
        s *= 2
    return x


def _inproj_body(x_ref, nw_ref, wqkv_ref, wf_ref, wu_ref, bf_ref, qnw_ref, knw_ref, seg_ref,
                 q_ref, k_ref, v_ref, kb_ref, vb_ref, lf_ref, u_ref):
    hn = _rms(x_ref[0], nw_ref[...]).astype(BF16)
    qkv = jnp.dot(hn, wqkv_ref[...], preferred_element_type=F32)
    seg = seg_ref[...]

    def head_norm(t, w):
        t2 = t * t
        hi = t2.astype(BF16)
        lo = (t2 - hi.astype(F32)).astype(BF16)
        ms = jnp.dot(hi, seg, preferred_element_type=F32) + jnp.dot(lo, seg, preferred_element_type=F32)
        return t * lax.rsqrt(ms + RMS_EPS) * w

    q = head_norm(qkv[:, :D_ATTN], qnw_ref[...])
    k = head_norm(qkv[:, D_ATTN:2 * D_ATTN], knw_ref[...])
    v = qkv[:, 2 * D_ATTN:]
    q_ref[...] = (q * (ATTN_SCALE * LOG2E)).astype(BF16)
    k_ref[0] = k.T
    v_ref[0] = v.T
    kb_ref[...] = k.astype(BF16)
    vb_ref[...] = v.astype(BF16)
    z = jnp.dot(hn, wf_ref[...], preferred_element_type=F32) + bf_ref[...]
    lf = jnp.minimum(z, 0.0) - jnp.log(1.0 + jnp.exp(-jnp.abs(z)))
    lf_ref[...] = lf[:, :N_HEADS]
    u_ref[...] = jnp.dot(hn, wu_ref[...], preferred_element_type=F32)


def _inproj(x3, w):
    bsz, length, _ = x3.shape
    tm = min(ROW_TILE, length)
    nt = length // tm
    rows = bsz * length
    row_map = lambda b, t: (b * nt + t, 0)
    const = lambda b, t: (0, 0)
    full = lambda a: pl.BlockSpec(a.shape, const)
    consts = [w['norm_mix_w'], w['wqkv'], w['wf'], w['wu'], w['b_forget'], w['q_norm_w'], w['k_norm_w'], w['seg']]
    return pl.pallas_call(
        _inproj_body,
        grid=(bsz, nt),
        in_specs=[pl.BlockSpec((1, tm, D_MODEL), lambda b, t: (b, t, 0))] + [full(a) for a in consts],
        out_specs=[pl.BlockSpec((tm, D_ATTN), row_map)]
        + [pl.BlockSpec((1, D_ATTN, tm), lambda b, t: (b, 0, t))] * 2
        + [pl.BlockSpec((tm, D_ATTN), row_map)] * 2
        + [pl.BlockSpec((tm, N_HEADS), row_map), pl.BlockSpec((tm, D_SSM), lambda b, t: (t, b))],
        out_shape=[jax.ShapeDtypeStruct((rows, D_ATTN), BF16),
                   jax.ShapeDtypeStruct((bsz, D_ATTN, length), F32),
                   jax.ShapeDtypeStruct((bsz, D_ATTN, length), F32),
                   jax.ShapeDtypeStruct((rows, D_ATTN), BF16),
                   jax.ShapeDtypeStruct((rows, D_ATTN), BF16),
                   jax.ShapeDtypeStruct((rows, N_HEADS), F32),
                   jax.ShapeDtypeStruct((length, bsz * D_SSM), F32)],
        compiler_params=_params("parallel", "parallel"),
        name="inproj",
    )(x3, *consts)


def _cumsum_body(lf_ref, c_ref):
    c_ref[0] = _lane_cumsum(lf_ref[0])


def _cumsum_lanes(lft):
    bsz, nh, length = lft.shape
    spec = pl.BlockSpec((1, nh, length), lambda b: (b, 0, 0))
    return pl.pallas_call(
        _cumsum_body, grid=(bsz,), in_specs=[spec], out_specs=spec,
        out_shape=jax.ShapeDtypeStruct(lft.shape, F32),
        compiler_params=_params("parallel"), name="logf_cumsum",
    )(lft)


def _attn_prompt_body(q_ref, k_ref, v_ref, ct_ref, o_ref, acc_ref, m_ref, *, tile):
    i = pl.program_id(1)
    lane = lax.broadcasted_iota(jnp.int32, (1, LANES), 1)
    row = lax.broadcasted_iota(jnp.int32, (tile, tile), 0)
    col = lax.broadcasted_iota(jnp.int32, (tile, tile), 1)
    hmasks = (lane < HEAD_DIM, lane >= HEAD_DIM)
    sum_lane = (HEAD_DIM, 0)
    ones_col = [jnp.where(lane == sl, 1.0, 0.0).astype(BF16) for sl in sum_lane]
    for hp in range(N_HEADS // 2):
        lanes = slice(LANES * hp, LANES * (hp + 1))
        qp = q_ref[:, lanes]
        qh = [jnp.where(hm, qp, jnp.zeros_like(qp)) for hm in hmasks]
        m_ref[...] = jnp.full(m_ref.shape, -jnp.inf, F32)
        acc_ref[...] = jnp.zeros(acc_ref.shape, F32)

        def kstep(j, diagonal, qh=qh, lanes=lanes, hp=hp):
            r0 = pl.multiple_of(j * tile, tile)
            kj = k_ref[pl.ds(r0, tile), lanes]
            vj = v_ref[pl.ds(r0, tile), lanes]
            for hh in range(2):
                s = lax.dot_general(qh[hh], kj, (((1,), (1,)), ((), ())), preferred_element_type=F32)
                s = s - ct_ref[0, 2 * hp + hh:2 * hp + hh + 1, pl.ds(r0, tile)] * LOG2E
                if diagonal:
                    s = jnp.where(row >= col, s, -jnp.inf)
                m_old = m_ref[hh]
                m_new = jnp.maximum(m_old, jnp.max(s, axis=1, keepdims=True))
                alpha = jnp.exp2(m_old - m_new)
                p = jnp.exp2(s - m_new).astype(BF16)
                vh = jnp.where(hmasks[hh], vj, ones_col[hh])
                acc_ref[hh] = alpha * acc_ref[hh] + jnp.dot(p, vh, preferred_element_type=F32)
                m_ref[hh] = m_new

        def body(j, carry):
            kstep(j, False)
            return carry

        lax.fori_loop(0, i, body, 0)
        kstep(i, True)
        acc0, acc1 = acc_ref[0], acc_ref[1]
        o_ref[:, lanes] = jnp.where(hmasks[0], acc0 / acc0[:, sum_lane[0]:sum_lane[0] + 1],
                                    acc1 / acc1[:, sum_lane[1]:sum_lane[1] + 1])


def _attn_prompt(q, kb, vb, ct, bsz, length):
    tile = min(ATTN_TILE, length)
    nq = length // tile
    q_map = lambda b, i: (b * nq + i, 0)
    seq_map = lambda b, i: (b, 0)
    return pl.pallas_call(
        functools.partial(_attn_prompt_body, tile=tile),
        grid=(bsz, nq),
        in_specs=[pl.BlockSpec((tile, D_ATTN), q_map),
                  pl.BlockSpec((length, D_ATTN), seq_map),
                  pl.BlockSpec((length, D_ATTN), seq_map),
                  pl.BlockSpec((1, N_HEADS, length), lambda b, i: (b, 0, 0))],
        out_specs=pl.BlockSpec((tile, D_ATTN), q_map),
        out_shape=jax.ShapeDtypeStruct((bsz * length, D_ATTN), F32),
        scratch_shapes=[pltpu.VMEM((2, tile, LANES), F32), pltpu.VMEM((2, tile, 1), F32)],
        compiler_params=_params("parallel", "parallel"),
        name="attn_prompt",
    )(q, kb, vb, ct)


def _attn_sample_body(pt_ref, q_ref, *refs, pages, n_new):
    del pt_ref
    k_refs, v_refs, lf_refs = refs[:pages], refs[pages:2 * pages], refs[2 * pages:3 * pages]
    kn_ref, vn_ref, lfn_ref, o_ref, m_ref, l_ref, acc_ref, c_ref = refs[3 * pages:]
    j = pl.program_id(1)
    rows = N_HEADS * n_new

    @pl.when(j == 0)
    def _():
        m_ref[...] = jnp.full(m_ref.shape, -jnp.inf, F32)
        l_ref[...] = jnp.zeros(l_ref.shape, F32)
        acc_ref[...] = jnp.zeros(acc_ref.shape, F32)
        c_ref[...] = jnp.zeros(c_ref.shape, F32)

    q = q_ref[0]

    def chunk(kc, vc, lft, valid):
        n = kc.shape[1]
        s = jnp.dot(q, kc, preferred_element_type=F32)
        ck = _lane_cumsum(lft) + c_ref[:, 0:1]
        c_ref[...] = jnp.broadcast_to(ck[:, n - 1:n], c_ref.shape)
        ck2 = ck * LOG2E
        s = s - jnp.concatenate([jnp.broadcast_to(ck2[h:h + 1], (n_new, n)) for h in range(N_HEADS)], axis=0)
        if valid is not None:
            s = jnp.where(valid, s, -jnp.inf)
        m_old = m_ref[...]
        m_new = jnp.maximum(m_old, jnp.max(s, axis=1, keepdims=True))
        alpha = jnp.exp2(m_old - m_new)
        p = jnp.exp2(s - m_new)
        l_ref[...] = alpha * l_ref[...] + jnp.sum(p, axis=1, keepdims=True)
        acc_ref[...] = alpha * acc_ref[...] + lax.dot_general(
            p.astype(BF16), vc, (((1,), (1,)), ((), ())), preferred_element_type=F32)
        m_ref[...] = m_new

    for g in range(0, pages, PAGES_PER_CHUNK):
        grp = slice(g, g + PAGES_PER_CHUNK)
        chunk(jnp.concatenate([r[0] for r in k_refs[grp]], axis=1).astype(BF16),
              jnp.concatenate([r[0] for r in v_refs[grp]], axis=1).astype(BF16),
              jnp.concatenate([r[0] for r in lf_refs[grp]], axis=1), None)

    @pl.when(j == pl.num_programs(1) - 1)
    def _():
        qi = lax.broadcasted_iota(jnp.int32, (rows, PAGE_SIZE), 0) % n_new
        kj = lax.broadcasted_iota(jnp.int32, (rows, PAGE_SIZE), 1)
        chunk(kn_ref[0].astype(BF16), vn_ref[0].astype(BF16), lfn_ref[0], kj <= qi)
        o_full = acc_ref[...] / l_ref[...]
        lane = lax.broadcasted_iota(jnp.int32, (1, D_ATTN), 1)
        out = jnp.zeros((n_new, D_ATTN), F32)
        for h in range(N_HEADS):
            hmask = (lane >= HEAD_DIM * h) & (lane < HEAD_DIM * (h + 1))
            out = out + jnp.where(hmask, o_full[n_new * h:n_new * (h + 1)], 0.0)
        o_ref[0] = out


def _attn_sample(qbd, cache_k, cache_v, cache_lft, page_table, kn, vn, lfn, n_new):
    bsz, n_pages = page_table.shape
    pages = min(PAGES_PER_STEP, n_pages)
    nj = n_pages // pages
    rows = N_HEADS * n_new

    def page_map(p):
        return lambda b, j, pt: (pt[b * n_pages + j * pages + p], 0, 0)

    seq_map = lambda b, j, pt: (b, 0, 0)
    in_specs = [pl.BlockSpec((1, rows, D_ATTN), seq_map)]
    in_specs += [pl.BlockSpec((1, D_ATTN, PAGE_SIZE), page_map(p)) for p in range(pages)]
    in_specs += [pl.BlockSpec((1, D_ATTN, PAGE_SIZE), page_map(p)) for p in range(pages)]
    in_specs += [pl.BlockSpec((1, N_HEADS, PAGE_SIZE), page_map(p)) for p in range(pages)]
    in_specs += [pl.BlockSpec((1, D_ATTN, PAGE_SIZE), seq_map)] * 2 + [pl.BlockSpec((1, N_HEADS, PAGE_SIZE), seq_map)]
    return pl.pallas_call(
        functools.partial(_attn_sample_body, pages=pages, n_new=n_new),
        grid_spec=pltpu.PrefetchScalarGridSpec(
            num_scalar_prefetch=1, grid=(bsz, nj), in_specs=in_specs,
            out_specs=pl.BlockSpec((1, n_new, D_ATTN), seq_map),
            scratch_shapes=[pltpu.VMEM((rows, 1), F32), pltpu.VMEM((rows, 1), F32),
                            pltpu.VMEM((rows, D_ATTN), F32), pltpu.VMEM((N_HEADS, LANES), F32)]),
        out_shape=jax.ShapeDtypeStruct((bsz, n_new, D_ATTN), F32),
        compiler_params=_params("parallel", "arbitrary"),
        name="attn_sample",
    )(page_table.reshape(-1), qbd, *([cache_k] * pages), *([cache_v] * pages), *([cache_lft] * pages), kn, vn, lfn)


SCAN_LANES = 512


def _ssm_body(u_ref, h0_ref, bw_ref, cre_ref, cim_ref, ar_ref, ai_ref, dsk_ref, wglu_ref, bglu_ref,
              out_ref, hlast_ref, hbuf_ref, state_ref, ubuf_ref, *, steps, bsz):
    g = pl.program_id(0)

    @pl.when(g == 0)
    def _():
        state_ref[...] = h0_ref[...]

    n_tiles = D_SSM // LANES
    for b in range(bsz):
        for t in range(n_tiles):
            col = D_SSM * b + LANES * t
            ubuf_ref[t, pl.ds(b, steps, stride=bsz), :] = u_ref[:, col:col + LANES]
    u = jnp.concatenate([ubuf_ref[t] for t in range(n_tiles)], axis=1)
    ub = u.astype(BF16)
    for p in range(N_SSM_GROUPS // 2):
        t = p // 4
        bu = jnp.dot(ub[:, LANES * t:LANES * (t + 1)], bw_ref[p], preferred_element_type=F32)
        hbuf_ref[:, LANES * p:LANES * (p + 1)] = bu[:, :LANES]
        hbuf_ref[:, D_STATE + LANES * p:D_STATE + LANES * (p + 1)] = bu[:, LANES:]

    for c in range(D_STATE // SCAN_LANES):
        re_l = slice(SCAN_LANES * c, SCAN_LANES * (c + 1))
        im_l = slice(D_STATE + SCAN_LANES * c, D_STATE + SCAN_LANES * (c + 1))
        ar = jnp.broadcast_to(ar_ref[:, re_l], (bsz, SCAN_LANES))
        ai = jnp.broadcast_to(ai_ref[:, re_l], (bsz, SCAN_LANES))

        def step(t, carry, re_l=re_l, im_l=im_l, ar=ar, ai=ai):
            re, im = carry
            r0 = pl.multiple_of(t * bsz, bsz)
            nre = ar * re - ai * im + hbuf_ref[pl.ds(r0, bsz), re_l]
            nim = ar * im + ai * re + hbuf_ref[pl.ds(r0, bsz), im_l]
            hbuf_ref[pl.ds(r0, bsz), re_l] = nre
            hbuf_ref[pl.ds(r0, bsz), im_l] = nim
            return nre, nim

        re, im = lax.fori_loop(0, steps, step, (state_ref[:, re_l], state_ref[:, im_l]))
        state_ref[:, re_l] = re
        state_ref[:, im_l] = im

    hlast_ref[...] = state_ref[...]

    ys = []
    for t in range(D_SSM // LANES):
        w = 4 * LANES
        hre = hbuf_ref[:, w * t:w * (t + 1)].astype(BF16)
        him = hbuf_ref[:, D_STATE + w * t:D_STATE + w * (t + 1)].astype(BF16)
        ys.append(jnp.dot(hre, cre_ref[t], preferred_element_type=F32)
                  + jnp.dot(him, cim_ref[t], preferred_element_type=F32))
    y = jnp.concatenate(ys, axis=1) + dsk_ref[...] * u
    z = 0.5 * y * (1.0 + jnp.tanh(math.sqrt(2.0 / math.pi) * (y + 0.044715 * (y * y * y))))
    gate = jnp.dot(z.astype(BF16), wglu_ref[...], preferred_element_type=F32) + bglu_ref[...]
    out = z / (1.0 + jnp.exp(-gate))
    for t in range(n_tiles):
        ubuf_ref[t] = out[:, LANES * t:LANES * (t + 1)]
    for b in range(bsz):
        for t in range(n_tiles):
            col = D_SSM * b + LANES * t
            out_ref[:, col:col + LANES] = ubuf_ref[t, pl.ds(b, steps, stride=bsz), :]


def _ssm(u_lb, h0, w, bsz, length):
    steps = max(1, min(length, ROW_TILE // bsz))
    rows = steps * bsz
    const2 = lambda g: (0, 0)
    const3 = lambda g: (0, 0, 0)
    full = lambda a: pl.BlockSpec(a.shape, const2 if a.ndim == 2 else const3)
    consts = [h0, w['ssm_bw'], w['ssm_cre'], w['ssm_cim'], w['ssm_ar'], w['ssm_ai'], w['d_skip'], w['w_glu'], w['b_glu']]
    return pl.pallas_call(
        functools.partial(_ssm_body, steps=steps, bsz=bsz),
        grid=(length // steps,),
        in_specs=[pl.BlockSpec((steps, bsz * D_SSM), lambda g: (g, 0))] + [full(a) for a in consts],
        out_specs=[pl.BlockSpec((steps, bsz * D_SSM), lambda g: (g, 0)), pl.BlockSpec((bsz, 2 * D_STATE), const2)],
        out_shape=[jax.ShapeDtypeStruct((length, bsz * D_SSM), F32), jax.ShapeDtypeStruct((bsz, 2 * D_STATE), F32)],
        scratch_shapes=[pltpu.VMEM((rows, 2 * D_STATE), F32), pltpu.VMEM((bsz, 2 * D_STATE), F32),
                        pltpu.VMEM((D_SSM // LANES, rows, LANES), F32)],
        compiler_params=_params("arbitrary"),
        name="ssm",
    )(u_lb, *consts)


N_PLANES = D_MODEL // (2 * LANES)


def _pack_rows(x, out_ref):
    bits = lax.bitcast_convert_type(x.astype(BF16).astype(F32), jnp.uint32)
    for c in range(N_PLANES):
        lo = bits[:, 2 * LANES * c:2 * LANES * c + LANES]
        hi = bits[:, 2 * LANES * c + LANES:2 * LANES * (c + 1)]
        out_ref[c] = lax.bitcast_convert_type(hi | (lo >> 16), jnp.int32)


def _unpack_rows(ref):
    parts = []
    for c in range(N_PLANES):
        bits = lax.bitcast_convert_type(ref[c], jnp.uint32)
        parts.append(lax.bitcast_convert_type(bits << 16, F32))
        parts.append(lax.bitcast_convert_type(bits & jnp.uint32(0xFFFF0000), F32))
    return parts


def _outproj_body(attn_ref, ssm_ref, x_ref, aw_ref, sw_ref, woa_ref, wos_ref, nfw_ref, wrh_ref, wrl_ref,
                  wgus_ref, wds_ref, hn_table_ref, logit_table_ref, xres_ref, hn_ref, logit_ref):
    del hn_table_ref, logit_table_ref
    an = _rms(attn_ref[...], aw_ref[...]).astype(BF16)
    sn = _rms(ssm_ref[...], sw_ref[...]).astype(BF16)
    x1 = x_ref[0] + (jnp.dot(an, woa_ref[...], preferred_element_type=F32)
                     + jnp.dot(sn, wos_ref[...], preferred_element_type=F32))
    h2 = _rms(x1, nfw_ref[...])
    hb = h2.astype(BF16)
    hlo = (h2 - hb.astype(F32)).astype(BF16)
    _pack_rows(h2, hn_ref)
    logit_ref[...] = (jnp.dot(hb, wrh_ref[...], preferred_element_type=F32)
                      + (jnp.dot(hb, wrl_ref[...], preferred_element_type=F32)
                         + jnp.dot(hlo, wrh_ref[...], preferred_element_type=F32)))
    gu = jnp.dot(hb, wgus_ref[...], preferred_element_type=F32)
    gs = gu[:, :D_SHARED]
    act = (gs / (1.0 + jnp.exp(-gs))) * gu[:, D_SHARED:]
    xres_ref[...] = x1 + jnp.dot(act.astype(BF16), wds_ref[...], preferred_element_type=F32)


def _outproj(attn, ssm_tb, x3, w, tables, row0):
    table_rows = tables[1].shape[0]
    bsz, length, _ = x3.shape
    tm = min(ROW_TILE, length)
    nt = length // tm
    rows = bsz * length
    off = row0 // tm
    row_map = lambda b, t: (b * nt + t, 0)
    const = lambda b, t: (0, 0)
    full = lambda a: pl.BlockSpec(a.shape, const)
    consts = [w['attn_out_norm_w'], w['ssm_out_norm_w'], w['wo_a'], w['wo_s'], w['norm_ffn_w'],
              w['wr_hi'], w['wr_lo'], w['wgu_s'], w['wd_s']]
    in_specs = [pl.BlockSpec((tm, D_ATTN), row_map), pl.BlockSpec((tm, D_SSM), lambda b, t: (t, b)),
                pl.BlockSpec((1, tm, D_MODEL), lambda b, t: (b, t, 0))] + [full(a) for a in consts]
    args = [attn, ssm_tb, x3] + consts + list(tables)
    in_specs += [pl.BlockSpec(memory_space=pl.ANY)] * 2
    aliases = {len(args) - 2: 1, len(args) - 1: 2}
    return pl.pallas_call(
        _outproj_body,
        grid=(bsz, nt),
        in_specs=in_specs,
        out_specs=[pl.BlockSpec((tm, D_MODEL), row_map),
                   pl.BlockSpec((N_PLANES, tm, LANES), lambda b, t: (0, off + b * nt + t, 0)),
                   pl.BlockSpec((tm, LANES), lambda b, t: (off + b * nt + t, 0))],
        out_shape=[jax.ShapeDtypeStruct((rows, D_MODEL), F32),
                   jax.ShapeDtypeStruct((N_PLANES, table_rows, LANES), jnp.int32),
                   jax.ShapeDtypeStruct((table_rows, LANES), F32)],
        input_output_aliases=aliases,
        compiler_params=_params("parallel", "parallel"),
        name="outproj",
    )(*args)


def _route_body(logit_ref, bias_ref, tri_ref, idx_ref, gate_ref, rank_ref, cnt_ref, carry_ref):
    @pl.when(pl.program_id(0) == 0)
    def _():
        carry_ref[...] = jnp.zeros(carry_ref.shape, F32)

    lg = logit_ref[...]
    tr = lg.shape[0]
    lane = lax.broadcasted_iota(jnp.int32, (tr, LANES), 1).astype(F32)
    score = 1.0 / (1.0 + jnp.exp(-lg))
    sel = jnp.where(lane < N_EXPERTS, score + bias_ref[...], -jnp.inf)
    picked = []
    gates = jnp.zeros((tr, LANES), F32)
    idxs = jnp.zeros((tr, LANES), F32)
    member = jnp.zeros((tr, LANES), F32)
    for k in range(TOP_K):
        best = jnp.max(sel, axis=1, keepdims=True)
        e = jnp.min(jnp.where(sel == best, lane, float(LANES)), axis=1, keepdims=True)
        hit = lane == e
        picked.append(hit)
        gates = jnp.where(lane == k, jnp.sum(jnp.where(hit, score, 0.0), axis=1, keepdims=True), gates)
        idxs = jnp.where(lane == k, e, idxs)
        member = jnp.where(hit, 1.0, member)
        sel = jnp.where(hit, -jnp.inf, sel)
    gate_ref[...] = ROUTE_SCALE * gates / jnp.sum(gates, axis=1, keepdims=True)
    idx_ref[...] = idxs.astype(jnp.int32)
    before = jnp.dot(tri_ref[...], member.astype(BF16), preferred_element_type=F32) + carry_ref[...]
    ranks = jnp.zeros((tr, LANES), F32)
    for k in range(TOP_K):
        ranks = jnp.where(lane == k, jnp.sum(jnp.where(picked[k], before, 0.0), axis=1, keepdims=True), ranks)
    rank_ref[...] = ranks.astype(jnp.int32)
    carry_ref[...] = carry_ref[...] + jnp.sum(member, axis=0, keepdims=True)
    cnt_ref[...] = carry_ref[...]


def _route(logits, bias, tri, row0, n_tok):
    tr = tri.shape[0]
    off = row0 // tr
    tok_map = lambda i: (i, 0)
    const = lambda i: (0, 0)
    return pl.pallas_call(
        _route_body,
        grid=(n_tok // tr,),
        in_specs=[pl.BlockSpec((tr, LANES), lambda i: (i + off, 0)), pl.BlockSpec((1, LANES), const),
                  pl.BlockSpec((tr, tr), const)],
        out_specs=[pl.BlockSpec((tr, LANES), tok_map)] * 3 + [pl.BlockSpec((1, LANES), const)],
        out_shape=[jax.ShapeDtypeStruct((n_tok, LANES), jnp.int32), jax.ShapeDtypeStruct((n_tok, LANES), F32),
                   jax.ShapeDtypeStruct((n_tok, LANES), jnp.int32), jax.ShapeDtypeStruct((1, LANES), F32)],
        scratch_shapes=[pltpu.VMEM((1, LANES), F32)],
        compiler_params=_params("arbitrary"),
        name="route",
    )(logits, bias, tri)


def _slot_body(idx_ref, rank_ref, pstart_ref, pos_ref):
    idx = idx_ref[...]
    base = jnp.take_along_axis(jnp.broadcast_to(pstart_ref[...], idx.shape), idx, axis=1)
    pos_ref[...] = (base + rank_ref[...]).T[:TOP_K]


def _slot_positions(idx, rank, pstart):
    n_tok = idx.shape[0]
    tr = ROUTE_TILE
    tok_map = lambda i: (i, 0)
    return pl.pallas_call(
        _slot_body,
        grid=(n_tok // tr,),
        in_specs=[pl.BlockSpec((tr, LANES), tok_map), pl.BlockSpec((tr, LANES), tok_map),
                  pl.BlockSpec((1, LANES), lambda i: (0, 0))],
        out_specs=pl.BlockSpec((TOP_K, tr), lambda i: (0, i)),
        out_shape=jax.ShapeDtypeStruct((TOP_K, n_tok), jnp.int32),
        compiler_params=_params("parallel"),
        name="slot_positions",
    )(idx, rank, pstart)


def _gather_rows(table, indices):
    n_idx = indices.shape[0]
    width = table.shape[1]
    mesh = plsc.VectorSubcoreMesh(core_axis_name="core", subcore_axis_name="subcore")

    @pl.kernel(out_type=jax.ShapeDtypeStruct((n_idx, width), table.dtype), mesh=mesh)
    def gather(table_hbm, idx_hbm, out_hbm):
        def body(idx_vmem, out_vmem):
            pltpu.sync_copy(table_hbm.at[idx_vmem.at[0]], out_vmem)

        pltpu.emit_pipeline(
            body,
            grid=(n_idx // SC_WINDOW,),
            in_specs=[pl.BlockSpec((1, SC_WINDOW), lambda i: (0, i))],
            out_specs=[pl.BlockSpec((SC_WINDOW, width), lambda i: (i, 0))],
            core_axis_name=("core", "subcore"),
            dimension_semantics=(pltpu.PARALLEL,),
        )(idx_hbm, out_hbm)

    return gather(table, indices.reshape(1, n_idx))


def _scatter_rows(table, n_planes, row0, n_rows, dest, n_out):
    n_idx = dest.shape[0]
    width = table.shape[1]
    plane_win = table.shape[0] // n_planes // SC_WINDOW
    win0 = row0 // SC_WINDOW
    win_per_plane = n_rows // SC_WINDOW
    win_per_rep_plane = n_idx // SC_WINDOW // n_planes
    mesh = plsc.VectorSubcoreMesh(core_axis_name="core", subcore_axis_name="subcore")

    @pl.kernel(out_type=jax.ShapeDtypeStruct((n_out, width), table.dtype), mesh=mesh)
    def scatter(table_hbm, idx_hbm, out_hbm):
        def body(rows_vmem, idx_vmem):
            pltpu.sync_copy(rows_vmem, out_hbm.at[idx_vmem.at[0]])

        pltpu.emit_pipeline(
            body,
            grid=(n_idx // SC_WINDOW,),
            in_specs=[pl.BlockSpec((SC_WINDOW, width),
                                   lambda i: ((i // win_per_rep_plane) * plane_win + win0 + i % win_per_plane, 0)),
                      pl.BlockSpec((1, SC_WINDOW), lambda i: (0, i))],
            out_specs=[],
            core_axis_name=("core", "subcore"),
            dimension_semantics=(pltpu.PARALLEL,),
        )(table_hbm, idx_hbm)

    return scatter(table, dest.reshape(1, n_idx))


def _ffn_body(be_ref, nv_ref, x_ref, wg_ref, wu_ref, wd_ref, y_ref, wgu_s, wd_s):
    r = pl.program_id(0)
    block = x_ref.shape[1]

    @pl.when((r == 0) | (be_ref[r] != be_ref[jnp.maximum(r - 1, 0)]))
    def _():
        wgu_s[:, :D_EXPERT] = wg_ref[0].astype(BF16)
        wgu_s[:, D_EXPERT:] = wu_ref[0].astype(BF16)
        wd_s[...] = wd_ref[0].astype(BF16)

    n_valid = nv_ref[r]

    def ffn(partial_block):
        x = jnp.concatenate(_unpack_rows(x_ref), axis=1)
        if partial_block:
            x = jnp.where(lax.broadcasted_iota(jnp.int32, (block, 1), 0) < n_valid, x, 0.0)
        gu = jnp.dot(x.astype(BF16), wgu_s[...], preferred_element_type=F32)
        gs = gu[:, :D_EXPERT]
        act = (gs / (1.0 + jnp.exp(-gs))) * gu[:, D_EXPERT:]
        _pack_rows(jnp.dot(act.astype(BF16), wd_s[...], preferred_element_type=F32), y_ref)

    pl.when(n_valid == block)(functools.partial(ffn, False))
    pl.when((n_valid > 0) & (n_valid < block))(functools.partial(ffn, True))

    @pl.when(n_valid == 0)
    def _():
        y_ref[...] = jnp.zeros(y_ref.shape, jnp.int32)


def _expert_ffn(xs, blk_expert, blk_valid, w_gate, w_up, w_down, block):
    n_slots = xs.shape[1]
    nb = n_slots // block
    slot_spec = pl.BlockSpec((N_PLANES, block, LANES), lambda r, be, nv: (0, r, 0))
    return pl.pallas_call(
        _ffn_body,
        grid_spec=pltpu.PrefetchScalarGridSpec(
            num_scalar_prefetch=2, grid=(nb,),
            in_specs=[slot_spec,
                      pl.BlockSpec((1, D_MODEL, D_EXPERT), lambda r, be, nv: (be[r], 0, 0)),
                      pl.BlockSpec((1, D_MODEL, D_EXPERT), lambda r, be, nv: (be[r], 0, 0)),
                      pl.BlockSpec((1, D_EXPERT, D_MODEL), lambda r, be, nv: (be[r], 0, 0))],
            out_specs=slot_spec,
            scratch_shapes=[pltpu.VMEM((D_MODEL, 2 * D_EXPERT), BF16), pltpu.VMEM((D_EXPERT, D_MODEL), BF16)]),
        out_shape=jax.ShapeDtypeStruct((N_PLANES, n_slots, LANES), jnp.int32),
        compiler_params=_params("arbitrary"),
        name="expert_ffn",
    )(blk_expert, blk_valid, xs, w_gate, w_up, w_down)


def _combine_body(xres_ref, gate_ref, *refs):
    y_refs, o_ref = refs[:TOP_K], refs[TOP_K]
    g = gate_ref[...]
    acc = [xres_ref[:, LANES * i:LANES * (i + 1)] for i in range(D_MODEL // LANES)]
    for k in range(TOP_K):
        gk = g[:, k:k + 1]
        acc = [a + gk * p for a, p in zip(acc, _unpack_rows(y_refs[k]))]
    o_ref[...] = jnp.concatenate(acc, axis=1)


def _combine(xres, row0, gates, ysg, tok0, n):
    tc = min(COMBINE_TILE, n)
    off = row0 // tc
    tok_off = tok0 // tc
    per_k = gates.shape[0] // tc
    row_map = lambda i: (i + off, 0)
    y_specs = [pl.BlockSpec((N_PLANES, tc, LANES), lambda i, k=k: (0, k * per_k + tok_off + i, 0))
               for k in range(TOP_K)]
    in_specs = [pl.BlockSpec((tc, D_MODEL), row_map), pl.BlockSpec((tc, LANES), lambda i: (tok_off + i, 0))] + y_specs
    return pl.pallas_call(
        _combine_body,
        grid=(n // tc,),
        in_specs=in_specs,
        out_specs=pl.BlockSpec((tc, D_MODEL), row_map),
        out_shape=jax.ShapeDtypeStruct(xres.shape, F32),
        input_output_aliases={0: 0},
        compiler_params=_params("parallel"),
        name="combine",
    )(xres, gates, *([ysg] * TOP_K))


def _prepare_weights(norm_mix_w, w_in, b_forget, q_norm_w, k_norm_w, lambda_re, lambda_im, log_dt, b_re, b_im,
                     c_re, c_im, d_skip, w_glu, b_glu, attn_out_norm_w, ssm_out_norm_w, w_out, norm_ffn_w,
                     w_router, router_bias, w_gate_e, w_up_e, w_down_e, w_gate_s, w_up_s, w_down_s):
    w = {}
    row = lambda a: a.reshape(1, -1).astype(F32)
    w['norm_mix_w'] = row(norm_mix_w)
    w['wqkv'] = w_in[:, :3 * D_ATTN].astype(BF16)
    w['wf'] = jnp.pad(w_in[:, 3 * D_ATTN:3 * D_ATTN + N_HEADS], ((0, 0), (0, LANES - N_HEADS))).astype(BF16)
    w['wu'] = w_in[:, 3 * D_ATTN + N_HEADS:].astype(BF16)
    w['b_forget'] = jnp.pad(row(b_forget), ((0, 0), (0, LANES - N_HEADS)))
    w['q_norm_w'] = jnp.tile(row(q_norm_w), (1, N_HEADS))
    w['k_norm_w'] = jnp.tile(row(k_norm_w), (1, N_HEADS))
    head = jnp.arange(D_ATTN) // HEAD_DIM
    w['seg'] = jnp.where(head[:, None] == head[None, :], 1.0 / HEAD_DIM, 0.0).astype(BF16)

    dt = jnp.exp(log_dt.astype(F32))[:, None]
    lre, lim = lambda_re.astype(F32), lambda_im.astype(F32)
    a, b = lre * dt, lim * dt
    ea = jnp.exp(a)
    bar_re, bar_im = ea * jnp.cos(b), ea * jnp.sin(b)
    num_re = jnp.expm1(a) * jnp.cos(b) - 2.0 * jnp.sin(0.5 * b) ** 2
    num_im = bar_im
    den = lre * lre + lim * lim
    coef_re = (num_re * lre + num_im * lim) / den
    coef_im = (num_im * lre - num_re * lim) / den
    bb_re = coef_re[:, :, None] * b_re - coef_im[:, :, None] * b_im
    bb_im = coef_re[:, :, None] * b_im + coef_im[:, :, None] * b_re
    eye = jnp.eye(N_SSM_GROUPS, dtype=F32)

    def in_block_diag(m):
        return (m.transpose(0, 2, 1)[:, :, None, :] * eye[:, None, :, None]).reshape(D_SSM, D_STATE)

    def out_block_diag(m):
        return (m.transpose(0, 2, 1)[:, :, None, :] * eye[:, None, :, None]).reshape(D_STATE, D_SSM)

    pairs = jnp.arange(N_SSM_GROUPS // 2)

    def pair_blocks(m):
        return m.reshape(4, LANES, N_SSM_GROUPS // 2, LANES).transpose(2, 0, 1, 3)[pairs, pairs // 4]

    w['ssm_bw'] = jnp.concatenate([pair_blocks(in_block_diag(bb_re)), pair_blocks(in_block_diag(bb_im))],
                                  axis=2).astype(BF16)
    tiles = jnp.arange(D_SSM // LANES)

    def tile_blocks(m):
        return m.reshape(4, 4 * LANES, 4, LANES).transpose(0, 2, 1, 3)[tiles, tiles]

    w['ssm_cre'] = tile_blocks(out_block_diag(c_re.astype(F32))).astype(BF16)
    w['ssm_cim'] = tile_blocks(out_block_diag(-c_im.astype(F32))).astype(BF16)
    w['ssm_ar'] = bar_re.reshape(1, D_STATE)
    w['ssm_ai'] = bar_im.reshape(1, D_STATE)
    w['d_skip'] = row(d_skip)
    w['w_glu'] = w_glu.astype(BF16)
    w['b_glu'] = row(b_glu)

    w['attn_out_norm_w'] = row(attn_out_norm_w)
    w['ssm_out_norm_w'] = row(ssm_out_norm_w)
    w['wo_a'] = w_out[:D_ATTN].astype(BF16)
    w['wo_s'] = w_out[D_ATTN:].astype(BF16)
    w['norm_ffn_w'] = row(norm_ffn_w)
    wr = jnp.pad(w_router.astype(F32), ((0, 0), (0, LANES - N_EXPERTS)))
    w['wr_hi'] = wr.astype(BF16)
    w['wr_lo'] = (wr - w['wr_hi'].astype(F32)).astype(BF16)
    w['router_bias'] = jnp.pad(row(router_bias), ((0, 0), (0, LANES - N_EXPERTS)))
    w['wgu_s'] = jnp.concatenate([w_gate_s, w_up_s], axis=1).astype(BF16)
    w['wd_s'] = w_down_s.astype(BF16)
    w['w_gate_e'], w['w_up_e'], w['w_down_e'] = w_gate_e, w_up_e, w_down_e
    return w


def _mix_prompt(x, w, tables):
    bsz, length, _ = x.shape
    q, kt, vt, kb, vb, lf, u_tb = _inproj(x, w)
    ct = _cumsum_lanes(lf.reshape(bsz, length, N_HEADS).transpose(0, 2, 1))
    attn = _attn_prompt(q, kb, vb, ct, bsz, length)
    h0 = jnp.zeros((bsz, 2 * D_STATE), F32)
    ssm_lb, hlast = _ssm(u_tb, h0, w, bsz, length)
    xres, hn, logits = _outproj(attn, ssm_lb, x, w, tables, 0)
    heads_last = lambda a: a.reshape(1, bsz, N_HEADS, HEAD_DIM, length).transpose(0, 1, 4, 2, 3)
    caches = (heads_last(kt), heads_last(vt),
              lf.reshape(1, bsz, length, N_HEADS),
              hlast[:, :D_STATE].reshape(1, bsz, N_SSM_GROUPS, SSM_STATE),
              hlast[:, D_STATE:].reshape(1, bsz, N_SSM_GROUPS, SSM_STATE))
    return xres, hn, logits, caches


def _mix_sample(x, cache_k, cache_v, cache_logf, page_table, h0_re, h0_im, w, tables, row0):
    bsz, n_new, _ = x.shape
    rows = n_new * bsz
    xt = x.transpose(1, 0, 2).reshape(1, rows, D_MODEL)
    q, kt, vt, _, _, lf, u_tb = _inproj(xt, w)
    k, v = kt[0].T, vt[0].T
    to_bm = lambda a: a.reshape(n_new, bsz, -1).transpose(1, 0, 2)
    q_b, k_b, v_b, lf_b = to_bm(q), to_bm(k), to_bm(v), to_bm(lf)
    head = jnp.arange(D_ATTN) // HEAD_DIM
    hmask = (head[None, :] == jnp.arange(N_HEADS)[:, None]).astype(BF16)
    qbd = (q_b[:, None, :, :] * hmask[None, :, None, :]).reshape(bsz, N_HEADS * n_new, D_ATTN)
    pad_keys = ((0, 0), (0, PAGE_SIZE - n_new), (0, 0))
    n_pool = cache_k.shape[1]
    keys_minor = lambda c: c[0].transpose(0, 2, 3, 1).reshape(n_pool, D_ATTN, PAGE_SIZE)
    attn = _attn_sample(
        qbd, keys_minor(cache_k), keys_minor(cache_v),
        cache_logf[0].astype(F32).transpose(0, 2, 1), page_table.astype(jnp.int32),
        jnp.pad(k_b, pad_keys).transpose(0, 2, 1), jnp.pad(v_b, pad_keys).transpose(0, 2, 1),
        jnp.pad(lf_b.transpose(0, 2, 1), ((0, 0), (0, 0), (0, PAGE_SIZE - n_new))), n_new)
    attn_tm = attn.transpose(1, 0, 2).reshape(rows, D_ATTN)
    h0 = jnp.concatenate([h0_re.reshape(bsz, D_STATE), h0_im.reshape(bsz, D_STATE)], axis=1).astype(F32)
    ssm_lb, hlast = _ssm(u_tb.reshape(n_new, bsz * D_SSM), h0, w, bsz, n_new)
    xres, hn, logits = _outproj(attn_tm, ssm_lb.reshape(rows, D_SSM), xt, w, tables, row0)
    caches = (k_b.reshape(1, bsz, n_new, N_HEADS, HEAD_DIM), v_b.reshape(1, bsz, n_new, N_HEADS, HEAD_DIM),
              lf_b.reshape(1, bsz, n_new, N_HEADS),
              hlast[:, :D_STATE].reshape(1, bsz, N_SSM_GROUPS, SSM_STATE),
              hlast[:, D_STATE:].reshape(1, bsz, N_SSM_GROUPS, SSM_STATE))
    return xres, hn, logits, caches


def _moe(hn, logits, w, block, row0, n_tok):
    tri = (jnp.arange(ROUTE_TILE)[:, None] > jnp.arange(ROUTE_TILE)[None, :]).astype(BF16)
    idx, gates, rank, counts = _route(logits, w['router_bias'], tri, row0, n_tok)
    counts = counts[0, :N_EXPERTS].astype(jnp.int32)
    padded = (counts + block - 1) // block * block
    pend = jnp.cumsum(padded)
    pstart = pend - padded
    n_blk = -(-(n_tok * TOP_K) // block) + N_EXPERTS
    n_slots = n_blk * block
    pstart_row = jnp.pad(pstart, (0, LANES - N_EXPERTS)).reshape(1, LANES)
    pos = _slot_positions(idx, rank, pstart_row).reshape(-1)
    blk_start = jnp.arange(n_blk, dtype=jnp.int32) * block
    blk_expert = jnp.minimum(jnp.sum(pend[None, :] <= blk_start[:, None], axis=1), N_EXPERTS - 1).astype(jnp.int32)
    blk_valid = jnp.clip(pstart[blk_expert] + counts[blk_expert] - blk_start, 0, block).astype(jnp.int32)
    plane_pos = (pos[None, :] + n_slots * jnp.arange(N_PLANES, dtype=jnp.int32)[:, None]).reshape(-1)
    xs = _scatter_rows(hn.reshape(N_PLANES * hn.shape[1], LANES), N_PLANES, row0, n_tok, plane_pos,
                       N_PLANES * n_slots)
    ys = _expert_ffn(xs.reshape(N_PLANES, n_slots, LANES), blk_expert, blk_valid,
                     w['w_gate_e'], w['w_up_e'], w['w_down_e'], block)
    ysg = _gather_rows(ys.reshape(N_PLANES * n_slots, LANES), plane_pos)
    return gates, ysg.reshape(N_PLANES, TOP_K * n_tok, LANES)


def kernel(x_prompt, x_sample, cache_k, cache_v, cache_logf, page_table, state_ssm_re, state_ssm_im, norm_mix_w, w_in, b_forget, q_norm_w, k_norm_w, lambda_re, lambda_im, log_dt, b_re, b_im, c_re, c_im, d_skip, w_glu, b_glu, attn_out_norm_w, ssm_out_norm_w, w_out, norm_ffn_w, w_router, router_bias, w_gate_e, w_up_e, w_down_e, w_gate_s, w_up_s, w_down_s):
    assert norm_mix_w.shape[0] == 1, "single-layer trunk"
    w = _prepare_weights(norm_mix_w[0], w_in[0], b_forget[0], q_norm_w[0], k_norm_w[0], lambda_re[0], lambda_im[0],
                         log_dt[0], b_re[0], b_im[0], c_re[0], c_im[0], d_skip[0], w_glu[0], b_glu[0],
                         attn_out_norm_w[0], ssm_out_norm_w[0], w_out[0], norm_ffn_w[0], w_router[0], router_bias[0],
                         w_gate_e[0], w_up_e[0], w_down_e[0], w_gate_s[0], w_up_s[0], w_down_s[0])
    bp, lp, _ = x_prompt.shape
    bs, ls, _ = x_sample.shape
    n_p, n_s = bp * lp, bs * ls
    tables = (jnp.zeros((N_PLANES, n_p + n_s, LANES), jnp.int32), jnp.zeros((n_p + n_s, LANES), F32))
    y_p, hn, lg, caches_p = _mix_prompt(x_prompt, w, tables)
    y_s, hn, lg, caches_s = _mix_sample(x_sample, cache_k, cache_v, cache_logf, page_table,
                                        state_ssm_re[0], state_ssm_im[0], w, (hn, lg), n_p)
    group = n_p // MOE_GROUPS
    for g in range(MOE_GROUPS):
        last = g == MOE_GROUPS - 1
        gates, ysg = _moe(hn, lg, w, FFN_BLOCK, g * group, group + (n_s if last else 0))
        y_p = _combine(y_p, g * group, gates, ysg, 0, group)
    y_s = _combine(y_s, 0, gates, ysg, group, n_s).reshape(ls, bs, D_MODEL).transpose(1, 0, 2)
    y_p = y_p.reshape(bp, lp, D_MODEL)
    return (y_p, y_s) + caches_p + caches_s
```

```python
import functools
import math

import jax
import jax.numpy as jnp
from jax import lax
from jax.experimental import pallas as pl
from jax.experimental.pallas import tpu as pltpu
from jax.experimental.pallas import tpu_sc as plsc

F32 = jnp.float32
BF16 = jnp.bfloat16

D_MODEL = 1024
D_ATTN = 512
D_SSM = 512
HEAD_DIM = 64
N_HEADS = 8
ATTN_SCALE = HEAD_DIM ** -0.5
LOG2E = math.log2(math.e)
SSM_GROUP = 16
N_SSM_GROUPS = 32
SSM_STATE = 64
D_STATE = N_SSM_GROUPS * SSM_STATE
N_EXPERTS = 64
TOP_K = 8
D_EXPERT = 256
D_SHARED = 256
ROUTE_SCALE = 2.5
PAGE_SIZE = 128
RMS_EPS = 1e-6

LANES = 128
VMEM_LIMIT = 56 * 1024 * 1024

ROW_TILE = 512
ATTN_TILE = 512
PAGES_PER_STEP = 32
PAGES_PER_CHUNK = 32
ROUTE_TILE = 256
FFN_BLOCK = 512
MOE_GROUPS = 2
COMBINE_TILE = 256
SC_WINDOW = 128


def _params(*sem):
    return pltpu.CompilerParams(dimension_semantics=sem, vmem_limit_bytes=VMEM_LIMIT)


def _rms(x, w):
    return x * lax.rsqrt(jnp.mean(x * x, axis=-1, keepdims=True) + RMS_EPS) * w


def _lane_cumsum(x):
    n = x.shape[1]
    lane = lax.broadcasted_iota(jnp.int32, x.shape, 1)
    s = 1
    while s < n:
        x = x + jnp.where(lane >= s, pltpu.roll(x, s, axis=1), 0.0)
        s *= 2
    return x


def _inproj_body(x_ref, nw_ref, wqkv_ref, wf_ref, wu_ref, bf_ref, qnw_ref, knw_ref, seg_ref,
                 q_ref, k_ref, v_ref, kb_ref, vb_ref, lf_ref, u_ref):
    hn = _rms(x_ref[0], nw_ref[...]).astype(BF16)
    qkv = jnp.dot(hn, wqkv_ref[...], preferred_element_type=F32)
    seg = seg_ref[...]

    def head_norm(t, w):
        t2 = t * t
        hi = t2.astype(BF16)
        lo = (t2 - hi.astype(F32)).astype(BF16)
        ms = jnp.dot(hi, seg, preferred_element_type=F32) + jnp.dot(lo, seg, preferred_element_type=F32)
        return t * lax.rsqrt(ms + RMS_EPS) * w

    q = head_norm(qkv[:, :D_ATTN], qnw_ref[...])
    k = head_norm(qkv[:, D_ATTN:2 * D_ATTN], knw_ref[...])
    v = qkv[:, 2 * D_ATTN:]
    q_ref[...] = (q * (ATTN_SCALE * LOG2E)).astype(BF16)
    k_ref[0] = k.T
    v_ref[0] = v.T
    kb_ref[...] = k.astype(BF16)
    vb_ref[...] = v.astype(BF16)
    z = jnp.dot(hn, wf_ref[...], preferred_element_type=F32) + bf_ref[...]
    lf = jnp.minimum(z, 0.0) - jnp.log(1.0 + jnp.exp(-jnp.abs(z)))
    lf_ref[...] = lf[:, :N_HEADS]
    u_ref[...] = jnp.dot(hn, wu_ref[...], preferred_element_type=F32)


def _inproj(x3, w):
    bsz, length, _ = x3.shape
    tm = min(ROW_TILE, length)
    nt = length // tm
    rows = bsz * length
    row_map = lambda b, t: (b * nt + t, 0)
    const = lambda b, t: (0, 0)
    full = lambda a: pl.BlockSpec(a.shape, const)
    consts = [w['norm_mix_w'], w['wqkv'], w['wf'], w['wu'], w['b_forget'], w['q_norm_w'], w['k_norm_w'], w['seg']]
    return pl.pallas_call(
        _inproj_body,
        grid=(bsz, nt),
        in_specs=[pl.BlockSpec((1, tm, D_MODEL), lambda b, t: (b, t, 0))] + [full(a) for a in consts],
        out_specs=[pl.BlockSpec((tm, D_ATTN), row_map)]
        + [pl.BlockSpec((1, D_ATTN, tm), lambda b, t: (b, 0, t))] * 2
        + [pl.BlockSpec((tm, D_ATTN), row_map)] * 2
        + [pl.BlockSpec((tm, N_HEADS), row_map), pl.BlockSpec((tm, D_SSM), lambda b, t: (t, b))],
        out_shape=[jax.ShapeDtypeStruct((rows, D_ATTN), BF16),
                   jax.ShapeDtypeStruct((bsz, D_ATTN, length), F32),
                   jax.ShapeDtypeStruct((bsz, D_ATTN, length), F32),
                   jax.ShapeDtypeStruct((rows, D_ATTN), BF16),
                   jax.ShapeDtypeStruct((rows, D_ATTN), BF16),
                   jax.ShapeDtypeStruct((rows, N_HEADS), F32),
                   jax.ShapeDtypeStruct((length, bsz * D_SSM), F32)],
        compiler_params=_params("parallel", "parallel"),
        name="inproj",
    )(x3, *consts)


def _cumsum_body(lf_ref, c_ref):
    c_ref[0] = _lane_cumsum(lf_ref[0])


def _cumsum_lanes(lft):
    bsz, nh, length = lft.shape
    spec = pl.BlockSpec((1, nh, length), lambda b: (b, 0, 0))
    return pl.pallas_call(
        _cumsum_body, grid=(bsz,), in_specs=[spec], out_specs=spec,
        out_shape=jax.ShapeDtypeStruct(lft.shape, F32),
        compiler_params=_params("parallel"), name="logf_cumsum",
    )(lft)


def _attn_prompt_body(q_ref, k_ref, v_ref, ct_ref, o_ref, acc_ref, m_ref, *, tile):
    i = pl.program_id(1)
    lane = lax.broadcasted_iota(jnp.int32, (1, LANES), 1)
    row = lax.broadcasted_iota(jnp.int32, (tile, tile), 0)
    col = lax.broadcasted_iota(jnp.int32, (tile, tile), 1)
    hmasks = (lane < HEAD_DIM, lane >= HEAD_DIM)
    sum_lane = (HEAD_DIM, 0)
    ones_col = [jnp.where(lane == sl, 1.0, 0.0).astype(BF16) for sl in sum_lane]
    for hp in range(N_HEADS // 2):
        lanes = slice(LANES * hp, LANES * (hp + 1))
        qp = q_ref[:, lanes]
        qh = [jnp.where(hm, qp, jnp.zeros_like(qp)) for hm in hmasks]
        m_ref[...] = jnp.full(m_ref.shape, -jnp.inf, F32)
        acc_ref[...] = jnp.zeros(acc_ref.shape, F32)

        def kstep(j, diagonal, qh=qh, lanes=lanes, hp=hp):
            r0 = pl.multiple_of(j * tile, tile)
            kj = k_ref[pl.ds(r0, tile), lanes]
            vj = v_ref[pl.ds(r0, tile), lanes]
            for hh in range(2):
                s = lax.dot_general(qh[hh], kj, (((1,), (1,)), ((), ())), preferred_element_type=F32)
                s = s - ct_ref[0, 2 * hp + hh:2 * hp + hh + 1, pl.ds(r0, tile)] * LOG2E
                if diagonal:
                    s = jnp.where(row >= col, s, -jnp.inf)
                m_old = m_ref[hh]
                m_new = jnp.maximum(m_old, jnp.max(s, axis=1, keepdims=True))
                alpha = jnp.exp2(m_old - m_new)
                p = jnp.exp2(s - m_new).astype(BF16)
                vh = jnp.where(hmasks[hh], vj, ones_col[hh])
                acc_ref[hh] = alpha * acc_ref[hh] + jnp.dot(p, vh, preferred_element_type=F32)
                m_ref[hh] = m_new

        def body(j, carry):
            kstep(j, False)
            return carry

        lax.fori_loop(0, i, body, 0)
        kstep(i, True)
        acc0, acc1 = acc_ref[0], acc_ref[1]
        o_ref[:, lanes] = jnp.where(hmasks[0], acc0 / acc0[:, sum_lane[0]:sum_lane[0] + 1],
                                    acc1 / acc1[:, sum_lane[1]:sum_lane[1] + 1])


def _attn_prompt(q, kb, vb, ct, bsz, length):
    tile = min(ATTN_TILE, length)
    nq = length // tile
    q_map = lambda b, i: (b * nq + i, 0)
    seq_map = lambda b, i: (b, 0)
    return pl.pallas_call(
        functools.partial(_attn_prompt_body, tile=tile),
        grid=(bsz, nq),
        in_specs=[pl.BlockSpec((tile, D_ATTN), q_map),
                  pl.BlockSpec((length, D_ATTN), seq_map),
                  pl.BlockSpec((length, D_ATTN), seq_map),
                  pl.BlockSpec((1, N_HEADS, length), lambda b, i: (b, 0, 0))],
        out_specs=pl.BlockSpec((tile, D_ATTN), q_map),
        out_shape=jax.ShapeDtypeStruct((bsz * length, D_ATTN), F32),
        scratch_shapes=[pltpu.VMEM((2, tile, LANES), F32), pltpu.VMEM((2, tile, 1), F32)],
        compiler_params=_params("parallel", "parallel"),
        name="attn_prompt",
    )(q, kb, vb, ct)


def _attn_sample_body(pt_ref, q_ref, *refs, pages, n_new):
    del pt_ref
    k_refs, v_refs, lf_refs = refs[:pages], refs[pages:2 * pages], refs[2 * pages:3 * pages]
    kn_ref, vn_ref, lfn_ref, o_ref, m_ref, l_ref, acc_ref, c_ref = refs[3 * pages:]
    j = pl.program_id(1)
    rows = N_HEADS * n_new

    @pl.when(j == 0)
    def _():
        m_ref[...] = jnp.full(m_ref.shape, -jnp.inf, F32)
        l_ref[...] = jnp.zeros(l_ref.shape, F32)
        acc_ref[...] = jnp.zeros(acc_ref.shape, F32)
        c_ref[...] = jnp.zeros(c_ref.shape, F32)

    q = q_ref[0]

    def chunk(kc, vc, lft, valid):
        n = kc.shape[1]
        s = jnp.dot(q, kc, preferred_element_type=F32)
        ck = _lane_cumsum(lft) + c_ref[:, 0:1]
        c_ref[...] = jnp.broadcast_to(ck[:, n - 1:n], c_ref.shape)
        ck2 = ck * LOG2E
        s = s - jnp.concatenate([jnp.broadcast_to(ck2[h:h + 1], (n_new, n)) for h in range(N_HEADS)], axis=0)
        if valid is not None:
            s = jnp.where(valid, s, -jnp.inf)
        m_old = m_ref[...]
        m_new = jnp.maximum(m_old, jnp.max(s, axis=1, keepdims=True))
        alpha = jnp.exp2(m_old - m_new)
        p = jnp.exp2(s - m_new)
        l_ref[...] = alpha * l_ref[...] + jnp.sum(p, axis=1, keepdims=True)
        acc_ref[...] = alpha * acc_ref[...] + lax.dot_general(
            p.astype(BF16), vc, (((1,), (1,)), ((), ())), preferred_element_type=F32)
        m_ref[...] = m_new

    for g in range(0, pages, PAGES_PER_CHUNK):
        grp = slice(g, g + PAGES_PER_CHUNK)
        chunk(jnp.concatenate([r[0] for r in k_refs[grp]], axis=1).astype(BF16),
              jnp.concatenate([r[0] for r in v_refs[grp]], axis=1).astype(BF16),
              jnp.concatenate([r[0] for r in lf_refs[grp]], axis=1), None)

    @pl.when(j == pl.num_programs(1) - 1)
    def _():
        qi = lax.broadcasted_iota(jnp.int32, (rows, PAGE_SIZE), 0) % n_new
        kj = lax.broadcasted_iota(jnp.int32, (rows, PAGE_SIZE), 1)
        chunk(kn_ref[0].astype(BF16), vn_ref[0].astype(BF16), lfn_ref[0], kj <= qi)
        o_full = acc_ref[...] / l_ref[...]
        lane = lax.broadcasted_iota(jnp.int32, (1, D_ATTN), 1)
        out = jnp.zeros((n_new, D_ATTN), F32)
        for h in range(N_HEADS):
            hmask = (lane >= HEAD_DIM * h) & (lane < HEAD_DIM * (h + 1))
            out = out + jnp.where(hmask, o_full[n_new * h:n_new * (h + 1)], 0.0)
        o_ref[0] = out


def _attn_sample(qbd, cache_k, cache_v, cache_lft, page_table, kn, vn, lfn, n_new):
    bsz, n_pages = page_table.shape
    pages = min(PAGES_PER_STEP, n_pages)
    nj = n_pages // pages
    rows = N_HEADS * n_new

    def page_map(p):
        return lambda b, j, pt: (pt[b * n_pages + j * pages + p], 0, 0)

    seq_map = lambda b, j, pt: (b, 0, 0)
    in_specs = [pl.BlockSpec((1, rows, D_ATTN), seq_map)]
    in_specs += [pl.BlockSpec((1, D_ATTN, PAGE_SIZE), page_map(p)) for p in range(pages)]
    in_specs += [pl.BlockSpec((1, D_ATTN, PAGE_SIZE), page_map(p)) for p in range(pages)]
    in_specs += [pl.BlockSpec((1, N_HEADS, PAGE_SIZE), page_map(p)) for p in range(pages)]
    in_specs += [pl.BlockSpec((1, D_ATTN, PAGE_SIZE), seq_map)] * 2 + [pl.BlockSpec((1, N_HEADS, PAGE_SIZE), seq_map)]
    return pl.pallas_call(
        functools.partial(_attn_sample_body, pages=pages, n_new=n_new),
        grid_spec=pltpu.PrefetchScalarGridSpec(
            num_scalar_prefetch=1, grid=(bsz, nj), in_specs=in_specs,
            out_specs=pl.BlockSpec((1, n_new, D_ATTN), seq_map),
            scratch_shapes=[pltpu.VMEM((rows, 1), F32), pltpu.VMEM((rows, 1), F32),
                            pltpu.VMEM((rows, D_ATTN), F32), pltpu.VMEM((N_HEADS, LANES), F32)]),
        out_shape=jax.ShapeDtypeStruct((bsz, n_new, D_ATTN), F32),
        compiler_params=_params("parallel", "arbitrary"),
        name="attn_sample",
    )(page_table.reshape(-1), qbd, *([cache_k] * pages), *([cache_v] * pages), *([cache_lft] * pages), kn, vn, lfn)


SCAN_LANES = 512


def _ssm_body(u_ref, h0_ref, bw_ref, cre_ref, cim_ref, ar_ref, ai_ref, dsk_ref, wglu_ref, bglu_ref,
              out_ref, hlast_ref, hbuf_ref, state_ref, ubuf_ref, *, steps, bsz):
    g = pl.program_id(0)

    @pl.when(g == 0)
    def _():
        state_ref[...] = h0_ref[...]

    n_tiles = D_SSM // LANES
    for b in range(bsz):
        for t in range(n_tiles):
            col = D_SSM * b + LANES * t
            ubuf_ref[t, pl.ds(b, steps, stride=bsz), :] = u_ref[:, col:col + LANES]
    u = jnp.concatenate([ubuf_ref[t] for t in range(n_tiles)], axis=1)
    ub = u.astype(BF16)
    for p in range(N_SSM_GROUPS // 2):
        t = p // 4
        bu = jnp.dot(ub[:, LANES * t:LANES * (t + 1)], bw_ref[p], preferred_element_type=F32)
        hbuf_ref[:, LANES * p:LANES * (p + 1)] = bu[:, :LANES]
        hbuf_ref[:, D_STATE + LANES * p:D_STATE + LANES * (p + 1)] = bu[:, LANES:]

    for c in range(D_STATE // SCAN_LANES):
        re_l = slice(SCAN_LANES * c, SCAN_LANES * (c + 1))
        im_l = slice(D_STATE + SCAN_LANES * c, D_STATE + SCAN_LANES * (c + 1))
        ar = jnp.broadcast_to(ar_ref[:, re_l], (bsz, SCAN_LANES))
        ai = jnp.broadcast_to(ai_ref[:, re_l], (bsz, SCAN_LANES))

        def step(t, carry, re_l=re_l, im_l=im_l, ar=ar, ai=ai):
            re, im = carry
            r0 = pl.multiple_of(t * bsz, bsz)
            nre = ar * re - ai * im + hbuf_ref[pl.ds(r0, bsz), re_l]
            nim = ar * im + ai * re + hbuf_ref[pl.ds(r0, bsz), im_l]
            hbuf_ref[pl.ds(r0, bsz), re_l] = nre
            hbuf_ref[pl.ds(r0, bsz), im_l] = nim
            return nre, nim

        re, im = lax.fori_loop(0, steps, step, (state_ref[:, re_l], state_ref[:, im_l]))
        state_ref[:, re_l] = re
        state_ref[:, im_l] = im

    hlast_ref[...] = state_ref[...]

    ys = []
    for t in range(D_SSM // LANES):
        w = 4 * LANES
        hre = hbuf_ref[:, w * t:w * (t + 1)].astype(BF16)
        him = hbuf_ref[:, D_STATE + w * t:D_STATE + w * (t + 1)].astype(BF16)
        ys.append(jnp.dot(hre, cre_ref[t], preferred_element_type=F32)
                  + jnp.dot(him, cim_ref[t], preferred_element_type=F32))
    y = jnp.concatenate(ys, axis=1) + dsk_ref[...] * u
    z = 0.5 * y * (1.0 + jnp.tanh(math.sqrt(2.0 / math.pi) * (y + 0.044715 * (y * y * y))))
    gate = jnp.dot(z.astype(BF16), wglu_ref[...], preferred_element_type=F32) + bglu_ref[...]
    out = z / (1.0 + jnp.exp(-gate))
    for t in range(n_tiles):
        ubuf_ref[t] = out[:, LANES * t:LANES * (t + 1)]
    for b in range(bsz):
        for t in range(n_tiles):
            col = D_SSM * b + LANES * t
            out_ref[:, col:col + LANES] = ubuf_ref[t, pl.ds(b, steps, stride=bsz), :]


def _ssm(u_lb, h0, w, bsz, length):
    steps = max(1, min(length, ROW_TILE // bsz))
    rows = steps * bsz
    const2 = lambda g: (0, 0)
    const3 = lambda g: (0, 0, 0)
    full = lambda a: pl.BlockSpec(a.shape, const2 if a.ndim == 2 else const3)
    consts = [h0, w['ssm_bw'], w['ssm_cre'], w['ssm_cim'], w['ssm_ar'], w['ssm_ai'], w['d_skip'], w['w_glu'], w['b_glu']]
    return pl.pallas_call(
        functools.partial(_ssm_body, steps=steps, bsz=bsz),
        grid=(length // steps,),
        in_specs=[pl.BlockSpec((steps, bsz * D_SSM), lambda g: (g, 0))] + [full(a) for a in consts],
        out_specs=[pl.BlockSpec((steps, bsz * D_SSM), lambda g: (g, 0)), pl.BlockSpec((bsz, 2 * D_STATE), const2)],
        out_shape=[jax.ShapeDtypeStruct((length, bsz * D_SSM), F32), jax.ShapeDtypeStruct((bsz, 2 * D_STATE), F32)],
        scratch_shapes=[pltpu.VMEM((rows, 2 * D_STATE), F32), pltpu.VMEM((bsz, 2 * D_STATE), F32),
                        pltpu.VMEM((D_SSM // LANES, rows, LANES), F32)],
        compiler_params=_params("arbitrary"),
        name="ssm",
    )(u_lb, *consts)


N_PLANES = D_MODEL // (2 * LANES)


def _pack_rows(x, out_ref):
    bits = lax.bitcast_convert_type(x.astype(BF16).astype(F32), jnp.uint32)
    for c in range(N_PLANES):
        lo = bits[:, 2 * LANES * c:2 * LANES * c + LANES]
        hi = bits[:, 2 * LANES * c + LANES:2 * LANES * (c + 1)]
        out_ref[c] = lax.bitcast_convert_type(hi | (lo >> 16), jnp.int32)


def _unpack_rows(ref):
    parts = []
    for c in range(N_PLANES):
        bits = lax.bitcast_convert_type(ref[c], jnp.uint32)
        parts.append(lax.bitcast_convert_type(bits << 16, F32))
        parts.append(lax.bitcast_convert_type(bits & jnp.uint32(0xFFFF0000), F32))
    return parts


def _outproj_body(attn_ref, ssm_ref, x_ref, aw_ref, sw_ref, woa_ref, wos_ref, nfw_ref, wrh_ref, wrl_ref,
                  wgus_ref, wds_ref, hn_table_ref, logit_table_ref, xres_ref, hn_ref, logit_ref):
    del hn_table_ref, logit_table_ref
    an = _rms(attn_ref[...], aw_ref[...]).astype(BF16)
    sn = _rms(ssm_ref[...], sw_ref[...]).astype(BF16)
    x1 = x_ref[0] + (jnp.dot(an, woa_ref[...], preferred_element_type=F32)
                     + jnp.dot(sn, wos_ref[...], preferred_element_type=F32))
    h2 = _rms(x1, nfw_ref[...])
    hb = h2.astype(BF16)
    hlo = (h2 - hb.astype(F32)).astype(BF16)
    _pack_rows(h2, hn_ref)
    logit_ref[...] = (jnp.dot(hb, wrh_ref[...], preferred_element_type=F32)
                      + (jnp.dot(hb, wrl_ref[...], preferred_element_type=F32)
                         + jnp.dot(hlo, wrh_ref[...], preferred_element_type=F32))).T
    gu = jnp.dot(hb, wgus_ref[...], preferred_element_type=F32)
    gs = gu[:, :D_SHARED]
    act = (gs / (1.0 + jnp.exp(-gs))) * gu[:, D_SHARED:]
    xres_ref[...] = x1 + jnp.dot(act.astype(BF16), wds_ref[...], preferred_element_type=F32)


def _outproj(attn, ssm_tb, x3, w, tables, row0):
    table_rows = tables[1].shape[1]
    bsz, length, _ = x3.shape
    tm = min(ROW_TILE, length)
    nt = length // tm
    rows = bsz * length
    off = row0 // tm
    row_map = lambda b, t: (b * nt + t, 0)
    const = lambda b, t: (0, 0)
    full = lambda a: pl.BlockSpec(a.shape, const)
    consts = [w['attn_out_norm_w'], w['ssm_out_norm_w'], w['wo_a'], w['wo_s'], w['norm_ffn_w'],
              w['wr_hi'], w['wr_lo'], w['wgu_s'], w['wd_s']]
    in_specs = [pl.BlockSpec((tm, D_ATTN), row_map), pl.BlockSpec((tm, D_SSM), lambda b, t: (t, b)),
                pl.BlockSpec((1, tm, D_MODEL), lambda b, t: (b, t, 0))] + [full(a) for a in consts]
    args = [attn, ssm_tb, x3] + consts + list(tables)
    in_specs += [pl.BlockSpec(memory_space=pl.ANY)] * 2
    aliases = {len(args) - 2: 1, len(args) - 1: 2}
    return pl.pallas_call(
        _outproj_body,
        grid=(bsz, nt),
        in_specs=in_specs,
        out_specs=[pl.BlockSpec((tm, D_MODEL), row_map),
                   pl.BlockSpec((N_PLANES, tm, LANES), lambda b, t: (0, off + b * nt + t, 0)),
                   pl.BlockSpec((LANES, tm), lambda b, t: (0, off + b * nt + t))],
        out_shape=[jax.ShapeDtypeStruct((rows, D_MODEL), F32),
                   jax.ShapeDtypeStruct((N_PLANES, table_rows, LANES), jnp.int32),
                   jax.ShapeDtypeStruct((LANES, table_rows), F32)],
        input_output_aliases=aliases,
        compiler_params=_params("parallel", "parallel"),
        name="outproj",
    )(*args)


def _route_body(logit_ref, bias_ref, tri_ref, idx_ref, gate_ref, rank_ref, cnt_ref, carry_ref):
    @pl.when(pl.program_id(0) == 0)
    def _():
        carry_ref[...] = jnp.zeros(carry_ref.shape, F32)

    lg = logit_ref[:N_EXPERTS, :]
    tr = lg.shape[1]
    row = lax.broadcasted_iota(jnp.int32, (N_EXPERTS, tr), 0).astype(F32)
    score = 1.0 / (1.0 + jnp.exp(-lg))
    sel = score + bias_ref[:N_EXPERTS, 0:1]
    picked, chosen, gate = [], [], []
    member = jnp.zeros((N_EXPERTS, tr), F32)
    for k in range(TOP_K):
        best = jnp.max(sel, axis=0, keepdims=True)
        e = jnp.min(jnp.where(sel == best, row, float(N_EXPERTS)), axis=0, keepdims=True)
        hit = row == e
        picked.append(hit)
        chosen.append(e)
        gate.append(jnp.sum(jnp.where(hit, score, 0.0), axis=0, keepdims=True))
        member = jnp.where(hit, 1.0, member)
        sel = jnp.where(hit, -jnp.inf, sel)
    gates = jnp.concatenate(gate, axis=0)
    gates = ROUTE_SCALE * gates / jnp.sum(gates, axis=0, keepdims=True)
    gate_ref[...] = jnp.concatenate([gates, jnp.zeros((LANES - TOP_K, tr), F32)], axis=0).T
    idx_ref[...] = jnp.concatenate(chosen, axis=0).astype(jnp.int32)
    before = jnp.dot(member.astype(BF16), tri_ref[...], preferred_element_type=F32) + carry_ref[:, 0:1]
    ranks = [jnp.sum(jnp.where(picked[k], before, 0.0), axis=0, keepdims=True) for k in range(TOP_K)]
    rank_ref[...] = jnp.concatenate(ranks, axis=0).astype(jnp.int32)
    carry_ref[...] = carry_ref[...] + jnp.sum(member, axis=1, keepdims=True)
    cnt_ref[...] = carry_ref[...]


def _route(logits_t, bias_col, tri, row0, n_tok):
    tr = tri.shape[0]
    off = row0 // tr
    k_map = lambda i: (0, i)
    const = lambda i: (0, 0)
    return pl.pallas_call(
        _route_body,
        grid=(n_tok // tr,),
        in_specs=[pl.BlockSpec((LANES, tr), lambda i: (0, i + off)), pl.BlockSpec((LANES, LANES), const),
                  pl.BlockSpec((tr, tr), const)],
        out_specs=[pl.BlockSpec((TOP_K, tr), k_map), pl.BlockSpec((tr, LANES), lambda i: (i, 0)),
                   pl.BlockSpec((TOP_K, tr), k_map), pl.BlockSpec((N_EXPERTS, LANES), const)],
        out_shape=[jax.ShapeDtypeStruct((TOP_K, n_tok), jnp.int32), jax.ShapeDtypeStruct((n_tok, LANES), F32),
                   jax.ShapeDtypeStruct((TOP_K, n_tok), jnp.int32), jax.ShapeDtypeStruct((N_EXPERTS, LANES), F32)],
        scratch_shapes=[pltpu.VMEM((N_EXPERTS, LANES), F32)],
        compiler_params=_params("arbitrary"),
        name="route",
    )(logits_t, bias_col, tri)


def _slot_body(idx_ref, rank_ref, pstart_ref, pos_ref):
    table = pstart_ref[...]
    for c in range(idx_ref.shape[1] // LANES):
        lanes = slice(LANES * c, LANES * (c + 1))
        pos_ref[:, lanes] = jnp.take_along_axis(table, idx_ref[:, lanes], axis=1) + rank_ref[:, lanes]


def _slot_positions(idx, rank, pstart):
    return pl.pallas_call(
        _slot_body,
        out_shape=jax.ShapeDtypeStruct(idx.shape, jnp.int32),
        compiler_params=pltpu.CompilerParams(vmem_limit_bytes=VMEM_LIMIT),
        name="slot_positions",
    )(idx, rank, pstart)


def _gather_rows(table, indices):
    n_idx = indices.shape[0]
    width = table.shape[1]
    mesh = plsc.VectorSubcoreMesh(core_axis_name="core", subcore_axis_name="subcore")

    @pl.kernel(out_type=jax.ShapeDtypeStruct((n_idx, width), table.dtype), mesh=mesh)
    def gather(table_hbm, idx_hbm, out_hbm):
        def body(idx_vmem, out_vmem):
            pltpu.sync_copy(table_hbm.at[idx_vmem.at[0]], out_vmem)

        pltpu.emit_pipeline(
            body,
            grid=(n_idx // SC_WINDOW,),
            in_specs=[pl.BlockSpec((1, SC_WINDOW), lambda i: (0, i))],
            out_specs=[pl.BlockSpec((SC_WINDOW, width), lambda i: (i, 0))],
            core_axis_name=("core", "subcore"),
            dimension_semantics=(pltpu.PARALLEL,),
        )(idx_hbm, out_hbm)

    return gather(table, indices.reshape(1, n_idx))


def _scatter_rows(table, n_planes, row0, n_rows, dest, n_out):
    n_idx = dest.shape[0]
    width = table.shape[1]
    plane_win = table.shape[0] // n_planes // SC_WINDOW
    win0 = row0 // SC_WINDOW
    win_per_plane = n_rows // SC_WINDOW
    win_per_rep_plane = n_idx // SC_WINDOW // n_planes
    mesh = plsc.VectorSubcoreMesh(core_axis_name="core", subcore_axis_name="subcore")

    @pl.kernel(out_type=jax.ShapeDtypeStruct((n_out, width), table.dtype), mesh=mesh)
    def scatter(table_hbm, idx_hbm, out_hbm):
        def body(rows_vmem, idx_vmem):
            pltpu.sync_copy(rows_vmem, out_hbm.at[idx_vmem.at[0]])

        pltpu.emit_pipeline(
            body,
            grid=(n_idx // SC_WINDOW,),
            in_specs=[pl.BlockSpec((SC_WINDOW, width),
                                   lambda i: ((i // win_per_rep_plane) * plane_win + win0 + i % win_per_plane, 0)),
                      pl.BlockSpec((1, SC_WINDOW), lambda i: (0, i))],
            out_specs=[],
            core_axis_name=("core", "subcore"),
            dimension_semantics=(pltpu.PARALLEL,),
        )(table_hbm, idx_hbm)

    return scatter(table, dest.reshape(1, n_idx))


def _ffn_body(be_ref, nv_ref, x_ref, wg_ref, wu_ref, wd_ref, y_ref, wgu_s, wd_s):
    r = pl.program_id(0)
    block = x_ref.shape[1]

    @pl.when((r == 0) | (be_ref[r] != be_ref[jnp.maximum(r - 1, 0)]))
    def _():
        wgu_s[:, :D_EXPERT] = wg_ref[0].astype(BF16)
        wgu_s[:, D_EXPERT:] = wu_ref[0].astype(BF16)
        wd_s[...] = wd_ref[0].astype(BF16)

    n_valid = nv_ref[r]

    def ffn(partial_block):
        x = jnp.concatenate(_unpack_rows(x_ref), axis=1)
        if partial_block:
            x = jnp.where(lax.broadcasted_iota(jnp.int32, (block, 1), 0) < n_valid, x, 0.0)
        gu = jnp.dot(x.astype(BF16), wgu_s[...], preferred_element_type=F32)
        gs = gu[:, :D_EXPERT]
        act = (gs / (1.0 + jnp.exp(-gs))) * gu[:, D_EXPERT:]
        _pack_rows(jnp.dot(act.astype(BF16), wd_s[...], preferred_element_type=F32), y_ref)

    pl.when(n_valid == block)(functools.partial(ffn, False))
    pl.when((n_valid > 0) & (n_valid < block))(functools.partial(ffn, True))

    @pl.when(n_valid == 0)
    def _():
        y_ref[...] = jnp.zeros(y_ref.shape, jnp.int32)


def _expert_ffn(xs, blk_expert, blk_valid, w_gate, w_up, w_down, block):
    n_slots = xs.shape[1]
    nb = n_slots // block
    slot_spec = pl.BlockSpec((N_PLANES, block, LANES), lambda r, be, nv: (0, r, 0))
    return pl.pallas_call(
        _ffn_body,
        grid_spec=pltpu.PrefetchScalarGridSpec(
            num_scalar_prefetch=2, grid=(nb,),
            in_specs=[slot_spec,
                      pl.BlockSpec((1, D_MODEL, D_EXPERT), lambda r, be, nv: (be[r], 0, 0)),
                      pl.BlockSpec((1, D_MODEL, D_EXPERT), lambda r, be, nv: (be[r], 0, 0)),
                      pl.BlockSpec((1, D_EXPERT, D_MODEL), lambda r, be, nv: (be[r], 0, 0))],
            out_specs=slot_spec,
            scratch_shapes=[pltpu.VMEM((D_MODEL, 2 * D_EXPERT), BF16), pltpu.VMEM((D_EXPERT, D_MODEL), BF16)]),
        out_shape=jax.ShapeDtypeStruct((N_PLANES, n_slots, LANES), jnp.int32),
        compiler_params=_params("arbitrary"),
        name="expert_ffn",
    )(blk_expert, blk_valid, xs, w_gate, w_up, w_down)


def _combine_body(xres_ref, gate_ref, *refs):
    y_refs, o_ref = refs[:TOP_K], refs[TOP_K]
    g = gate_ref[...]
    acc = [xres_ref[:, LANES * i:LANES * (i + 1)] for i in range(D_MODEL // LANES)]
    for k in range(TOP_K):
        gk = g[:, k:k + 1]
        acc = [a + gk * p for a, p in zip(acc, _unpack_rows(y_refs[k]))]
    o_ref[...] = jnp.concatenate(acc, axis=1)


def _combine(xres, row0, gates, ysg, tok0, n):
    tc = min(COMBINE_TILE, n)
    off = row0 // tc
    tok_off = tok0 // tc
    per_k = gates.shape[0] // tc
    row_map = lambda i: (i + off, 0)
    y_specs = [pl.BlockSpec((N_PLANES, tc, LANES), lambda i, k=k: (0, k * per_k + tok_off + i, 0))
               for k in range(TOP_K)]
    in_specs = [pl.BlockSpec((tc, D_MODEL), row_map), pl.BlockSpec((tc, LANES), lambda i: (tok_off + i, 0))] + y_specs
    return pl.pallas_call(
        _combine_body,
        grid=(n // tc,),
        in_specs=in_specs,
        out_specs=pl.BlockSpec((tc, D_MODEL), row_map),
        out_shape=jax.ShapeDtypeStruct(xres.shape, F32),
        input_output_aliases={0: 0},
        compiler_params=_params("parallel"),
        name="combine",
    )(xres, gates, *([ysg] * TOP_K))


def _prepare_weights(norm_mix_w, w_in, b_forget, q_norm_w, k_norm_w, lambda_re, lambda_im, log_dt, b_re, b_im,
                     c_re, c_im, d_skip, w_glu, b_glu, attn_out_norm_w, ssm_out_norm_w, w_out, norm_ffn_w,
                     w_router, router_bias, w_gate_e, w_up_e, w_down_e, w_gate_s, w_up_s, w_down_s):
    w = {}
    row = lambda a: a.reshape(1, -1).astype(F32)
    w['norm_mix_w'] = row(norm_mix_w)
    w['wqkv'] = w_in[:, :3 * D_ATTN].astype(BF16)
    w['wf'] = jnp.pad(w_in[:, 3 * D_ATTN:3 * D_ATTN + N_HEADS], ((0, 0), (0, LANES - N_HEADS))).astype(BF16)
    w['wu'] = w_in[:, 3 * D_ATTN + N_HEADS:].astype(BF16)
    w['b_forget'] = jnp.pad(row(b_forget), ((0, 0), (0, LANES - N_HEADS)))
    w['q_norm_w'] = jnp.tile(row(q_norm_w), (1, N_HEADS))
    w['k_norm_w'] = jnp.tile(row(k_norm_w), (1, N_HEADS))
    head = jnp.arange(D_ATTN) // HEAD_DIM
    w['seg'] = jnp.where(head[:, None] == head[None, :], 1.0 / HEAD_DIM, 0.0).astype(BF16)

    dt = jnp.exp(log_dt.astype(F32))[:, None]
    lre, lim = lambda_re.astype(F32), lambda_im.astype(F32)
    a, b = lre * dt, lim * dt
    ea = jnp.exp(a)
    bar_re, bar_im = ea * jnp.cos(b), ea * jnp.sin(b)
    num_re = jnp.expm1(a) * jnp.cos(b) - 2.0 * jnp.sin(0.5 * b) ** 2
    num_im = bar_im
    den = lre * lre + lim * lim
    coef_re = (num_re * lre + num_im * lim) / den
    coef_im = (num_im * lre - num_re * lim) / den
    bb_re = coef_re[:, :, None] * b_re - coef_im[:, :, None] * b_im
    bb_im = coef_re[:, :, None] * b_im + coef_im[:, :, None] * b_re
    eye = jnp.eye(N_SSM_GROUPS, dtype=F32)

    def in_block_diag(m):
        return (m.transpose(0, 2, 1)[:, :, None, :] * eye[:, None, :, None]).reshape(D_SSM, D_STATE)

    def out_block_diag(m):
        return (m.transpose(0, 2, 1)[:, :, None, :] * eye[:, None, :, None]).reshape(D_STATE, D_SSM)

    pairs = jnp.arange(N_SSM_GROUPS // 2)

    def pair_blocks(m):
        return m.reshape(4, LANES, N_SSM_GROUPS // 2, LANES).transpose(2, 0, 1, 3)[pairs, pairs // 4]

    w['ssm_bw'] = jnp.concatenate([pair_blocks(in_block_diag(bb_re)), pair_blocks(in_block_diag(bb_im))],
                                  axis=2).astype(BF16)
    tiles = jnp.arange(D_SSM // LANES)

    def tile_blocks(m):
        return m.reshape(4, 4 * LANES, 4, LANES).transpose(0, 2, 1, 3)[tiles, tiles]

    w['ssm_cre'] = tile_blocks(out_block_diag(c_re.astype(F32))).astype(BF16)
    w['ssm_cim'] = tile_blocks(out_block_diag(-c_im.astype(F32))).astype(BF16)
    w['ssm_ar'] = bar_re.reshape(1, D_STATE)
    w['ssm_ai'] = bar_im.reshape(1, D_STATE)
    w['d_skip'] = row(d_skip)
    w['w_glu'] = w_glu.astype(BF16)
    w['b_glu'] = row(b_glu)

    w['attn_out_norm_w'] = row(attn_out_norm_w)
    w['ssm_out_norm_w'] = row(ssm_out_norm_w)
    w['wo_a'] = w_out[:D_ATTN].astype(BF16)
    w['wo_s'] = w_out[D_ATTN:].astype(BF16)
    w['norm_ffn_w'] = row(norm_ffn_w)
    wr = jnp.pad(w_router.astype(F32), ((0, 0), (0, LANES - N_EXPERTS)))
    w['wr_hi'] = wr.astype(BF16)
    w['wr_lo'] = (wr - w['wr_hi'].astype(F32)).astype(BF16)
    w['router_bias'] = jnp.broadcast_to(jnp.pad(router_bias.astype(F32), (0, LANES - N_EXPERTS))[:, None],
                                        (LANES, LANES))
    w['wgu_s'] = jnp.concatenate([w_gate_s, w_up_s], axis=1).astype(BF16)
    w['wd_s'] = w_down_s.astype(BF16)
    w['w_gate_e'], w['w_up_e'], w['w_down_e'] = w_gate_e, w_up_e, w_down_e
    return w


def _mix_prompt(x, w, tables):
    bsz, length, _ = x.shape
    q, kt, vt, kb, vb, lf, u_tb = _inproj(x, w)
    ct = _cumsum_lanes(lf.reshape(bsz, length, N_HEADS).transpose(0, 2, 1))
    attn = _attn_prompt(q, kb, vb, ct, bsz, length)
    h0 = jnp.zeros((bsz, 2 * D_STATE), F32)
    ssm_lb, hlast = _ssm(u_tb, h0, w, bsz, length)
    xres, hn, logits = _outproj(attn, ssm_lb, x, w, tables, 0)
    heads_last = lambda a: a.reshape(1, bsz, N_HEADS, HEAD_DIM, length).transpose(0, 1, 4, 2, 3)
    caches = (heads_last(kt), heads_last(vt),
              lf.reshape(1, bsz, length, N_HEADS),
              hlast[:, :D_STATE].reshape(1, bsz, N_SSM_GROUPS, SSM_STATE),
              hlast[:, D_STATE:].reshape(1, bsz, N_SSM_GROUPS, SSM_STATE))
    return xres, hn, logits, caches


def _mix_sample(x, cache_k, cache_v, cache_logf, page_table, h0_re, h0_im, w, tables, row0):
    bsz, n_new, _ = x.shape
    rows = n_new * bsz
    xt = x.transpose(1, 0, 2).reshape(1, rows, D_MODEL)
    q, kt, vt, _, _, lf, u_tb = _inproj(xt, w)
    k, v = kt[0].T, vt[0].T
    to_bm = lambda a: a.reshape(n_new, bsz, -1).transpose(1, 0, 2)
    q_b, k_b, v_b, lf_b = to_bm(q), to_bm(k), to_bm(v), to_bm(lf)
    head = jnp.arange(D_ATTN) // HEAD_DIM
    hmask = (head[None, :] == jnp.arange(N_HEADS)[:, None]).astype(BF16)
    qbd = (q_b[:, None, :, :] * hmask[None, :, None, :]).reshape(bsz, N_HEADS * n_new, D_ATTN)
    pad_keys = ((0, 0), (0, PAGE_SIZE - n_new), (0, 0))
    n_pool = cache_k.shape[1]
    keys_minor = lambda c: c[0].transpose(0, 2, 3, 1).reshape(n_pool, D_ATTN, PAGE_SIZE)
    attn = _attn_sample(
        qbd, keys_minor(cache_k), keys_minor(cache_v),
        cache_logf[0].astype(F32).transpose(0, 2, 1), page_table.astype(jnp.int32),
        jnp.pad(k_b, pad_keys).transpose(0, 2, 1), jnp.pad(v_b, pad_keys).transpose(0, 2, 1),
        jnp.pad(lf_b.transpose(0, 2, 1), ((0, 0), (0, 0), (0, PAGE_SIZE - n_new))), n_new)
    attn_tm = attn.transpose(1, 0, 2).reshape(rows, D_ATTN)
    h0 = jnp.concatenate([h0_re.reshape(bsz, D_STATE), h0_im.reshape(bsz, D_STATE)], axis=1).astype(F32)
    ssm_lb, hlast = _ssm(u_tb.reshape(n_new, bsz * D_SSM), h0, w, bsz, n_new)
    xres, hn, logits = _outproj(attn_tm, ssm_lb.reshape(rows, D_SSM), xt, w, tables, row0)
    caches = (k_b.reshape(1, bsz, n_new, N_HEADS, HEAD_DIM), v_b.reshape(1, bsz, n_new, N_HEADS, HEAD_DIM),
              lf_b.reshape(1, bsz, n_new, N_HEADS),
              hlast[:, :D_STATE].reshape(1, bsz, N_SSM_GROUPS, SSM_STATE),
              hlast[:, D_STATE:].reshape(1, bsz, N_SSM_GROUPS, SSM_STATE))
    return xres, hn, logits, caches


def _moe(hn, logits, w, block, row0, n_tok):
    tri = (jnp.arange(ROUTE_TILE)[:, None] < jnp.arange(ROUTE_TILE)[None, :]).astype(BF16)
    idx, gates, rank, counts = _route(logits, w['router_bias'], tri, row0, n_tok)
    counts = counts[:, 0].astype(jnp.int32)
    padded = (counts + block - 1) // block * block
    experts = jnp.arange(N_EXPERTS, dtype=jnp.int32)
    pend = jnp.sum(jnp.where(experts[None, :] <= experts[:, None], padded[None, :], 0), axis=1)
    pstart = pend - padded
    n_blk = -(-(n_tok * TOP_K) // block) + N_EXPERTS
    n_slots = n_blk * block
    pstart_rows = jnp.broadcast_to(jnp.pad(pstart, (0, LANES - N_EXPERTS)), (TOP_K, LANES))
    pos = _slot_positions(idx, rank, pstart_rows).reshape(-1)
    blk_start = jnp.arange(n_blk, dtype=jnp.int32) * block
    blk_expert = jnp.minimum(jnp.sum(pend[None, :] <= blk_start[:, None], axis=1), N_EXPERTS - 1).astype(jnp.int32)
    seg_end = jnp.sum(jnp.where(blk_expert[:, None] == experts[None, :], (pstart + counts)[None, :], 0), axis=1)
    blk_valid = jnp.clip(seg_end - blk_start, 0, block).astype(jnp.int32)
    plane_pos = (pos[None, :] + n_slots * jnp.arange(N_PLANES, dtype=jnp.int32)[:, None]).reshape(-1)
    xs = _scatter_rows(hn.reshape(N_PLANES * hn.shape[1], LANES), N_PLANES, row0, n_tok, plane_pos,
                       N_PLANES * n_slots)
    ys = _expert_ffn(xs.reshape(N_PLANES, n_slots, LANES), blk_expert, blk_valid,
                     w['w_gate_e'], w['w_up_e'], w['w_down_e'], block)
    ysg = _gather_rows(ys.reshape(N_PLANES * n_slots, LANES), plane_pos)
    return gates, ysg.reshape(N_PLANES, TOP_K * n_tok, LANES)


def kernel(x_prompt, x_sample, cache_k, cache_v, cache_logf, page_table, state_ssm_re, state_ssm_im, norm_mix_w, w_in, b_forget, q_norm_w, k_norm_w, lambda_re, lambda_im, log_dt, b_re, b_im, c_re, c_im, d_skip, w_glu, b_glu, attn_out_norm_w, ssm_out_norm_w, w_out, norm_ffn_w, w_router, router_bias, w_gate_e, w_up_e, w_down_e, w_gate_s, w_up_s, w_down_s):
    assert norm_mix_w.shape[0] == 1, "single-layer trunk"
    w = _prepare_weights(norm_mix_w[0], w_in[0], b_forget[0], q_norm_w[0], k_norm_w[0], lambda_re[0], lambda_im[0],
                         log_dt[0], b_re[0], b_im[0], c_re[0], c_im[0], d_skip[0], w_glu[0], b_glu[0],
                         attn_out_norm_w[0], ssm_out_norm_w[0], w_out[0], norm_ffn_w[0], w_router[0], router_bias[0],
                         w_gate_e[0], w_up_e[0], w_down_e[0], w_gate_s[0], w_up_s[0], w_down_s[0])
    bp, lp, _ = x_prompt.shape
    bs, ls, _ = x_sample.shape
    n_p, n_s = bp * lp, bs * ls
    tables = (jnp.zeros((N_PLANES, n_p + n_s, LANES), jnp.int32), jnp.zeros((LANES, n_p + n_s), F32))
    y_p, hn, lg, caches_p = _mix_prompt(x_prompt, w, tables)
    y_s, hn, lg, caches_s = _mix_sample(x_sample, cache_k, cache_v, cache_logf, page_table,
                                        state_ssm_re[0], state_ssm_im[0], w, (hn, lg), n_p)
    group = n_p // MOE_GROUPS
    for g in range(MOE_GROUPS):
        last = g == MOE_GROUPS - 1
        gates, ysg = _moe(hn, lg, w, FFN_BLOCK, g * group, group + (n_s if last else 0))
        y_p = _combine(y_p, g * group, gates, ysg, 0, group)
    y_s = _combine(y_s, 0, gates, ysg, group, n_s).reshape(ls, bs, D_MODEL).transpose(1, 0, 2)
    y_p = y_p.reshape(bp, lp, D_MODEL)
    return (y_p, y_s) + caches_p + caches_s
```

```python
import functools
import math

import jax
import jax.numpy as jnp
from jax import lax
from jax.experimental import pallas as pl
from jax.experimental.pallas import tpu as pltpu
from jax.experimental.pallas import tpu_sc as plsc

F32 = jnp.float32
BF16 = jnp.bfloat16

D_MODEL = 1024
D_ATTN = 512
D_SSM = 512
HEAD_DIM = 64
N_HEADS = 8
ATTN_SCALE = HEAD_DIM ** -0.5
LOG2E = math.log2(math.e)
SSM_GROUP = 16
N_SSM_GROUPS = 32
SSM_STATE = 64
D_STATE = N_SSM_GROUPS * SSM_STATE
N_EXPERTS = 64
TOP_K = 8
D_EXPERT = 256
D_SHARED = 256
ROUTE_SCALE = 2.5
PAGE_SIZE = 128
RMS_EPS = 1e-6

LANES = 128
VMEM_LIMIT = 56 * 1024 * 1024

ROW_TILE = 512
ATTN_TILE = 1024
PAGES_PER_STEP = 32
PAGES_PER_CHUNK = 32
ROUTE_TILE = 256
FFN_BLOCK = 512
MOE_GROUPS = 2
COMBINE_TILE = 256
SC_WINDOW = 128


def _params(*sem):
    return pltpu.CompilerParams(dimension_semantics=sem, vmem_limit_bytes=VMEM_LIMIT)


def _rms(x, w):
    return x * lax.rsqrt(jnp.mean(x * x, axis=-1, keepdims=True) + RMS_EPS) * w


def _lane_cumsum(x):
    n = x.shape[1]
    lane = lax.broadcasted_iota(jnp.int32, x.shape, 1)
    s = 1
    while s < n:
        x = x + jnp.where(lane >= s, pltpu.roll(x, s, axis=1), 0.0)
        s *= 2
    return x


def _inproj_body(x_ref, nw_ref, wqkv_ref, wf_ref, wu_ref, bf_ref, qnw_ref, knw_ref, seg_ref,
                 q_ref, k_ref, v_ref, kb_ref, vb_ref, lf_ref, u_ref):
    hn = _rms(x_ref[0], nw_ref[...]).astype(BF16)
    qkv = jnp.dot(hn, wqkv_ref[...], preferred_element_type=F32)
    seg = seg_ref[...]

    def head_norm(t, w):
        t2 = t * t
        hi = t2.astype(BF16)
        lo = (t2 - hi.astype(F32)).astype(BF16)
        ms = jnp.dot(hi, seg, preferred_element_type=F32) + jnp.dot(lo, seg, preferred_element_type=F32)
        return t * lax.rsqrt(ms + RMS_EPS) * w

    q = head_norm(qkv[:, :D_ATTN], qnw_ref[...])
    k = head_norm(qkv[:, D_ATTN:2 * D_ATTN], knw_ref[...])
    v = qkv[:, 2 * D_ATTN:]
    q_ref[...] = (q * (ATTN_SCALE * LOG2E)).astype(BF16)
    k_ref[0] = k.T
    v_ref[0] = v.T
    kb_ref[...] = k.astype(BF16)
    vb_ref[...] = v.astype(BF16)
    z = jnp.dot(hn, wf_ref[...], preferred_element_type=F32) + bf_ref[...]
    lf = jnp.minimum(z, 0.0) - jnp.log(1.0 + jnp.exp(-jnp.abs(z)))
    lf_ref[...] = lf[:, :N_HEADS]
    u_ref[...] = jnp.dot(hn, wu_ref[...], preferred_element_type=F32)


def _inproj(x3, w):
    bsz, length, _ = x3.shape
    tm = min(ROW_TILE, length)
    nt = length // tm
    rows = bsz * length
    row_map = lambda b, t: (b * nt + t, 0)
    const = lambda b, t: (0, 0)
    full = lambda a: pl.BlockSpec(a.shape, const)
    consts = [w['norm_mix_w'], w['wqkv'], w['wf'], w['wu'], w['b_forget'], w['q_norm_w'], w['k_norm_w'], w['seg']]
    return pl.pallas_call(
        _inproj_body,
        grid=(bsz, nt),
        in_specs=[pl.BlockSpec((1, tm, D_MODEL), lambda b, t: (b, t, 0))] + [full(a) for a in consts],
        out_specs=[pl.BlockSpec((tm, D_ATTN), row_map)]
        + [pl.BlockSpec((1, D_ATTN, tm), lambda b, t: (b, 0, t))] * 2
        + [pl.BlockSpec((tm, D_ATTN), row_map)] * 2
        + [pl.BlockSpec((tm, N_HEADS), row_map), pl.BlockSpec((tm, D_SSM), lambda b, t: (t, b))],
        out_shape=[jax.ShapeDtypeStruct((rows, D_ATTN), BF16),
                   jax.ShapeDtypeStruct((bsz, D_ATTN, length), F32),
                   jax.ShapeDtypeStruct((bsz, D_ATTN, length), F32),
                   jax.ShapeDtypeStruct((rows, D_ATTN), BF16),
                   jax.ShapeDtypeStruct((rows, D_ATTN), BF16),
                   jax.ShapeDtypeStruct((rows, N_HEADS), F32),
                   jax.ShapeDtypeStruct((length, bsz * D_SSM), F32)],
        compiler_params=_params("parallel", "parallel"),
        name="inproj",
    )(x3, *consts)


def _cumsum_body(lf_ref, c_ref):
    c_ref[0] = _lane_cumsum(lf_ref[0])


def _cumsum_lanes(lft):
    bsz, nh, length = lft.shape
    spec = pl.BlockSpec((1, nh, length), lambda b: (b, 0, 0))
    return pl.pallas_call(
        _cumsum_body, grid=(bsz,), in_specs=[spec], out_specs=spec,
        out_shape=jax.ShapeDtypeStruct(lft.shape, F32),
        compiler_params=_params("parallel"), name="logf_cumsum",
    )(lft)


def _attn_prompt_body(q_ref, k_ref, v_ref, ct_ref, o_ref, acc_ref, m_ref, *, tile):
    i = pl.program_id(1)
    lane = lax.broadcasted_iota(jnp.int32, (1, LANES), 1)
    row = lax.broadcasted_iota(jnp.int32, (tile, tile), 0)
    col = lax.broadcasted_iota(jnp.int32, (tile, tile), 1)
    hmasks = (lane < HEAD_DIM, lane >= HEAD_DIM)
    sum_lane = (HEAD_DIM, 0)
    ones_col = [jnp.where(lane == sl, 1.0, 0.0).astype(BF16) for sl in sum_lane]
    for hp in range(N_HEADS // 2):
        lanes = slice(LANES * hp, LANES * (hp + 1))
        qp = q_ref[:, lanes]
        qh = [jnp.where(hm, qp, jnp.zeros_like(qp)) for hm in hmasks]
        m_ref[...] = jnp.full(m_ref.shape, -jnp.inf, F32)
        acc_ref[...] = jnp.zeros(acc_ref.shape, F32)

        def kstep(j, diagonal, qh=qh, lanes=lanes, hp=hp):
            r0 = pl.multiple_of(j * tile, tile)
            kj = k_ref[pl.ds(r0, tile), lanes]
            vj = v_ref[pl.ds(r0, tile), lanes]
            for hh in range(2):
                s = lax.dot_general(qh[hh], kj, (((1,), (1,)), ((), ())), preferred_element_type=F32)
                s = s - ct_ref[0, 2 * hp + hh:2 * hp + hh + 1, pl.ds(r0, tile)] * LOG2E
                if diagonal:
                    s = jnp.where(row >= col, s, -jnp.inf)
                m_old = m_ref[hh]
                m_new = jnp.maximum(m_old, jnp.max(s, axis=1, keepdims=True))
                alpha = jnp.exp2(m_old - m_new)
                p = jnp.exp2(s - m_new).astype(BF16)
                vh = jnp.where(hmasks[hh], vj, ones_col[hh])
                acc_ref[hh] = alpha * acc_ref[hh] + jnp.dot(p, vh, preferred_element_type=F32)
                m_ref[hh] = m_new

        def body(j, carry):
            kstep(j, False)
            return carry

        lax.fori_loop(0, i, body, 0)
        kstep(i, True)
        acc0, acc1 = acc_ref[0], acc_ref[1]
        o_ref[:, lanes] = jnp.where(hmasks[0], acc0 / acc0[:, sum_lane[0]:sum_lane[0] + 1],
                                    acc1 / acc1[:, sum_lane[1]:sum_lane[1] + 1])


def _attn_prompt(q, kb, vb, ct, bsz, length):
    tile = min(ATTN_TILE, length)
    nq = length // tile
    q_map = lambda b, i: (b * nq + i, 0)
    seq_map = lambda b, i: (b, 0)
    return pl.pallas_call(
        functools.partial(_attn_prompt_body, tile=tile),
        grid=(bsz, nq),
        in_specs=[pl.BlockSpec((tile, D_ATTN), q_map),
                  pl.BlockSpec((length, D_ATTN), seq_map),
                  pl.BlockSpec((length, D_ATTN), seq_map),
                  pl.BlockSpec((1, N_HEADS, length), lambda b, i: (b, 0, 0))],
        out_specs=pl.BlockSpec((tile, D_ATTN), q_map),
        out_shape=jax.ShapeDtypeStruct((bsz * length, D_ATTN), F32),
        scratch_shapes=[pltpu.VMEM((2, tile, LANES), F32), pltpu.VMEM((2, tile, 1), F32)],
        compiler_params=_params("parallel", "parallel"),
        name="attn_prompt",
    )(q, kb, vb, ct)


def _attn_sample_body(pt_ref, q_ref, *refs, pages, n_new):
    del pt_ref
    k_refs, v_refs, lf_refs = refs[:pages], refs[pages:2 * pages], refs[2 * pages:3 * pages]
    kn_ref, vn_ref, lfn_ref, o_ref, m_ref, l_ref, acc_ref, c_ref = refs[3 * pages:]
    j = pl.program_id(1)
    rows = N_HEADS * n_new

    @pl.when(j == 0)
    def _():
        m_ref[...] = jnp.full(m_ref.shape, -jnp.inf, F32)
        l_ref[...] = jnp.zeros(l_ref.shape, F32)
        acc_ref[...] = jnp.zeros(acc_ref.shape, F32)
        c_ref[...] = jnp.zeros(c_ref.shape, F32)

    q = q_ref[0]

    def chunk(kc, vc, lft, valid):
        n = kc.shape[1]
        s = jnp.dot(q, kc, preferred_element_type=F32)
        ck = _lane_cumsum(lft) + c_ref[:, 0:1]
        c_ref[...] = jnp.broadcast_to(ck[:, n - 1:n], c_ref.shape)
        ck2 = ck * LOG2E
        s = s - jnp.concatenate([jnp.broadcast_to(ck2[h:h + 1], (n_new, n)) for h in range(N_HEADS)], axis=0)
        if valid is not None:
            s = jnp.where(valid, s, -jnp.inf)
        m_old = m_ref[...]
        m_new = jnp.maximum(m_old, jnp.max(s, axis=1, keepdims=True))
        alpha = jnp.exp2(m_old - m_new)
        p = jnp.exp2(s - m_new)
        l_ref[...] = alpha * l_ref[...] + jnp.sum(p, axis=1, keepdims=True)
        acc_ref[...] = alpha * acc_ref[...] + lax.dot_general(
            p.astype(BF16), vc, (((1,), (1,)), ((), ())), preferred_element_type=F32)
        m_ref[...] = m_new

    for g in range(0, pages, PAGES_PER_CHUNK):
        grp = slice(g, g + PAGES_PER_CHUNK)
        chunk(jnp.concatenate([r[0] for r in k_refs[grp]], axis=1).astype(BF16),
              jnp.concatenate([r[0] for r in v_refs[grp]], axis=1).astype(BF16),
              jnp.concatenate([r[0] for r in lf_refs[grp]], axis=1), None)

    @pl.when(j == pl.num_programs(1) - 1)
    def _():
        qi = lax.broadcasted_iota(jnp.int32, (rows, PAGE_SIZE), 0) % n_new
        kj = lax.broadcasted_iota(jnp.int32, (rows, PAGE_SIZE), 1)
        chunk(kn_ref[0].astype(BF16), vn_ref[0].astype(BF16), lfn_ref[0], kj <= qi)
        o_full = acc_ref[...] / l_ref[...]
        lane = lax.broadcasted_iota(jnp.int32, (1, D_ATTN), 1)
        out = jnp.zeros((n_new, D_ATTN), F32)
        for h in range(N_HEADS):
            hmask = (lane >= HEAD_DIM * h) & (lane < HEAD_DIM * (h + 1))
            out = out + jnp.where(hmask, o_full[n_new * h:n_new * (h + 1)], 0.0)
        o_ref[0] = out


def _attn_sample(qbd, cache_k, cache_v, cache_lft, page_table, kn, vn, lfn, n_new):
    bsz, n_pages = page_table.shape
    pages = min(PAGES_PER_STEP, n_pages)
    nj = n_pages // pages
    rows = N_HEADS * n_new

    def page_map(p):
        return lambda b, j, pt: (pt[b * n_pages + j * pages + p], 0, 0)

    seq_map = lambda b, j, pt: (b, 0, 0)
    in_specs = [pl.BlockSpec((1, rows, D_ATTN), seq_map)]
    in_specs += [pl.BlockSpec((1, D_ATTN, PAGE_SIZE), page_map(p)) for p in range(pages)]
    in_specs += [pl.BlockSpec((1, D_ATTN, PAGE_SIZE), page_map(p)) for p in range(pages)]
    in_specs += [pl.BlockSpec((1, N_HEADS, PAGE_SIZE), page_map(p)) for p in range(pages)]
    in_specs += [pl.BlockSpec((1, D_ATTN, PAGE_SIZE), seq_map)] * 2 + [pl.BlockSpec((1, N_HEADS, PAGE_SIZE), seq_map)]
    return pl.pallas_call(
        functools.partial(_attn_sample_body, pages=pages, n_new=n_new),
        grid_spec=pltpu.PrefetchScalarGridSpec(
            num_scalar_prefetch=1, grid=(bsz, nj), in_specs=in_specs,
            out_specs=pl.BlockSpec((1, n_new, D_ATTN), seq_map),
            scratch_shapes=[pltpu.VMEM((rows, 1), F32), pltpu.VMEM((rows, 1), F32),
                            pltpu.VMEM((rows, D_ATTN), F32), pltpu.VMEM((N_HEADS, LANES), F32)]),
        out_shape=jax.ShapeDtypeStruct((bsz, n_new, D_ATTN), F32),
        compiler_params=_params("parallel", "arbitrary"),
        name="attn_sample",
    )(page_table.reshape(-1), qbd, *([cache_k] * pages), *([cache_v] * pages), *([cache_lft] * pages), kn, vn, lfn)


SCAN_LANES = 512


def _ssm_body(u_ref, h0_ref, bw_ref, cre_ref, cim_ref, ar_ref, ai_ref, dsk_ref, wglu_ref, bglu_ref,
              out_ref, hlast_ref, hbuf_ref, state_ref, ubuf_ref, *, steps, bsz):
    g = pl.program_id(0)

    @pl.when(g == 0)
    def _():
        state_ref[...] = h0_ref[...]

    n_tiles = D_SSM // LANES
    for b in range(bsz):
        for t in range(n_tiles):
            col = D_SSM * b + LANES * t
            ubuf_ref[t, pl.ds(b, steps, stride=bsz), :] = u_ref[:, col:col + LANES]
    u = jnp.concatenate([ubuf_ref[t] for t in range(n_tiles)], axis=1)
    ub = u.astype(BF16)
    for p in range(N_SSM_GROUPS // 2):
        t = p // 4
        bu = jnp.dot(ub[:, LANES * t:LANES * (t + 1)], bw_ref[p], preferred_element_type=F32)
        hbuf_ref[:, LANES * p:LANES * (p + 1)] = bu[:, :LANES]
        hbuf_ref[:, D_STATE + LANES * p:D_STATE + LANES * (p + 1)] = bu[:, LANES:]

    for c in range(D_STATE // SCAN_LANES):
        re_l = slice(SCAN_LANES * c, SCAN_LANES * (c + 1))
        im_l = slice(D_STATE + SCAN_LANES * c, D_STATE + SCAN_LANES * (c + 1))
        ar = jnp.broadcast_to(ar_ref[:, re_l], (bsz, SCAN_LANES))
        ai = jnp.broadcast_to(ai_ref[:, re_l], (bsz, SCAN_LANES))

        def step(t, carry, re_l=re_l, im_l=im_l, ar=ar, ai=ai):
            re, im = carry
            r0 = pl.multiple_of(t * bsz, bsz)
            nre = ar * re - ai * im + hbuf_ref[pl.ds(r0, bsz), re_l]
            nim = ar * im + ai * re + hbuf_ref[pl.ds(r0, bsz), im_l]
            hbuf_ref[pl.ds(r0, bsz), re_l] = nre
            hbuf_ref[pl.ds(r0, bsz), im_l] = nim
            return nre, nim

        re, im = lax.fori_loop(0, steps, step, (state_ref[:, re_l], state_ref[:, im_l]))
        state_ref[:, re_l] = re
        state_ref[:, im_l] = im

    hlast_ref[...] = state_ref[...]

    ys = []
    for t in range(D_SSM // LANES):
        w = 4 * LANES
        hre = hbuf_ref[:, w * t:w * (t + 1)].astype(BF16)
        him = hbuf_ref[:, D_STATE + w * t:D_STATE + w * (t + 1)].astype(BF16)
        ys.append(jnp.dot(hre, cre_ref[t], preferred_element_type=F32)
                  + jnp.dot(him, cim_ref[t], preferred_element_type=F32))
    y = jnp.concatenate(ys, axis=1) + dsk_ref[...] * u
    z = 0.5 * y * (1.0 + jnp.tanh(math.sqrt(2.0 / math.pi) * (y + 0.044715 * (y * y * y))))
    gate = jnp.dot(z.astype(BF16), wglu_ref[...], preferred_element_type=F32) + bglu_ref[...]
    out = z / (1.0 + jnp.exp(-gate))
    for t in range(n_tiles):
        ubuf_ref[t] = out[:, LANES * t:LANES * (t + 1)]
    for b in range(bsz):
        for t in range(n_tiles):
            col = D_SSM * b + LANES * t
            out_ref[:, col:col + LANES] = ubuf_ref[t, pl.ds(b, steps, stride=bsz), :]


def _ssm(u_lb, h0, w, bsz, length):
    steps = max(1, min(length, ROW_TILE // bsz))
    rows = steps * bsz
    const2 = lambda g: (0, 0)
    const3 = lambda g: (0, 0, 0)
    full = lambda a: pl.BlockSpec(a.shape, const2 if a.ndim == 2 else const3)
    consts = [h0, w['ssm_bw'], w['ssm_cre'], w['ssm_cim'], w['ssm_ar'], w['ssm_ai'], w['d_skip'], w['w_glu'], w['b_glu']]
    return pl.pallas_call(
        functools.partial(_ssm_body, steps=steps, bsz=bsz),
        grid=(length // steps,),
        in_specs=[pl.BlockSpec((steps, bsz * D_SSM), lambda g: (g, 0))] + [full(a) for a in consts],
        out_specs=[pl.BlockSpec((steps, bsz * D_SSM), lambda g: (g, 0)), pl.BlockSpec((bsz, 2 * D_STATE), const2)],
        out_shape=[jax.ShapeDtypeStruct((length, bsz * D_SSM), F32), jax.ShapeDtypeStruct((bsz, 2 * D_STATE), F32)],
        scratch_shapes=[pltpu.VMEM((rows, 2 * D_STATE), F32), pltpu.VMEM((bsz, 2 * D_STATE), F32),
                        pltpu.VMEM((D_SSM // LANES, rows, LANES), F32)],
        compiler_params=_params("arbitrary"),
        name="ssm",
    )(u_lb, *consts)


N_PLANES = D_MODEL // (2 * LANES)


def _pack_rows(x, out_ref):
    bits = lax.bitcast_convert_type(x.astype(BF16).astype(F32), jnp.uint32)
    for c in range(N_PLANES):
        lo = bits[:, 2 * LANES * c:2 * LANES * c + LANES]
        hi = bits[:, 2 * LANES * c + LANES:2 * LANES * (c + 1)]
        out_ref[c] = lax.bitcast_convert_type(hi | (lo >> 16), jnp.int32)


def _unpack_rows(ref):
    parts = []
    for c in range(N_PLANES):
        bits = lax.bitcast_convert_type(ref[c], jnp.uint32)
        parts.append(lax.bitcast_convert_type(bits << 16, F32))
        parts.append(lax.bitcast_convert_type(bits & jnp.uint32(0xFFFF0000), F32))
    return parts


def _outproj_body(attn_ref, ssm_ref, x_ref, aw_ref, sw_ref, woa_ref, wos_ref, nfw_ref, wrh_ref, wrl_ref,
                  wgus_ref, wds_ref, hn_table_ref, logit_table_ref, xres_ref, hn_ref, logit_ref):
    del hn_table_ref, logit_table_ref
    an = _rms(attn_ref[...], aw_ref[...]).astype(BF16)
    sn = _rms(ssm_ref[...], sw_ref[...]).astype(BF16)
    x1 = x_ref[0] + (jnp.dot(an, woa_ref[...], preferred_element_type=F32)
                     + jnp.dot(sn, wos_ref[...], preferred_element_type=F32))
    h2 = _rms(x1, nfw_ref[...])
    hb = h2.astype(BF16)
    hlo = (h2 - hb.astype(F32)).astype(BF16)
    _pack_rows(h2, hn_ref)
    logit_ref[...] = (jnp.dot(hb, wrh_ref[...], preferred_element_type=F32)
                      + (jnp.dot(hb, wrl_ref[...], preferred_element_type=F32)
                         + jnp.dot(hlo, wrh_ref[...], preferred_element_type=F32))).T
    gu = jnp.dot(hb, wgus_ref[...], preferred_element_type=F32)
    gs = gu[:, :D_SHARED]
    act = (gs / (1.0 + jnp.exp(-gs))) * gu[:, D_SHARED:]
    xres_ref[...] = x1 + jnp.dot(act.astype(BF16), wds_ref[...], preferred_element_type=F32)


def _outproj(attn, ssm_tb, x3, w, tables, row0):
    table_rows = tables[1].shape[1]
    bsz, length, _ = x3.shape
    tm = min(ROW_TILE, length)
    nt = length // tm
    rows = bsz * length
    off = row0 // tm
    row_map = lambda b, t: (b * nt + t, 0)
    const = lambda b, t: (0, 0)
    full = lambda a: pl.BlockSpec(a.shape, const)
    consts = [w['attn_out_norm_w'], w['ssm_out_norm_w'], w['wo_a'], w['wo_s'], w['norm_ffn_w'],
              w['wr_hi'], w['wr_lo'], w['wgu_s'], w['wd_s']]
    in_specs = [pl.BlockSpec((tm, D_ATTN), row_map), pl.BlockSpec((tm, D_SSM), lambda b, t: (t, b)),
                pl.BlockSpec((1, tm, D_MODEL), lambda b, t: (b, t, 0))] + [full(a) for a in consts]
    args = [attn, ssm_tb, x3] + consts + list(tables)
    in_specs += [pl.BlockSpec(memory_space=pl.ANY)] * 2
    aliases = {len(args) - 2: 1, len(args) - 1: 2}
    return pl.pallas_call(
        _outproj_body,
        grid=(bsz, nt),
        in_specs=in_specs,
        out_specs=[pl.BlockSpec((tm, D_MODEL), row_map),
                   pl.BlockSpec((N_PLANES, tm, LANES), lambda b, t: (0, off + b * nt + t, 0)),
                   pl.BlockSpec((LANES, tm), lambda b, t: (0, off + b * nt + t))],
        out_shape=[jax.ShapeDtypeStruct((rows, D_MODEL), F32),
                   jax.ShapeDtypeStruct((N_PLANES, table_rows, LANES), jnp.int32),
                   jax.ShapeDtypeStruct((LANES, table_rows), F32)],
        input_output_aliases=aliases,
        compiler_params=_params("parallel", "parallel"),
        name="outproj",
    )(*args)


def _route_body(logit_ref, bias_ref, tri_ref, idx_ref, gate_ref, rank_ref, cnt_ref, carry_ref):
    @pl.when(pl.program_id(0) == 0)
    def _():
        carry_ref[...] = jnp.zeros(carry_ref.shape, F32)

    lg = logit_ref[:N_EXPERTS, :]
    tr = lg.shape[1]
    row = lax.broadcasted_iota(jnp.int32, (N_EXPERTS, tr), 0).astype(F32)
    score = 1.0 / (1.0 + jnp.exp(-lg))
    sel = score + bias_ref[:N_EXPERTS, 0:1]
    picked, chosen, gate = [], [], []
    member = jnp.zeros((N_EXPERTS, tr), F32)
    for k in range(TOP_K):
        best = jnp.max(sel, axis=0, keepdims=True)
        e = jnp.min(jnp.where(sel == best, row, float(N_EXPERTS)), axis=0, keepdims=True)
        hit = row == e
        picked.append(hit)
        chosen.append(e)
        gate.append(jnp.sum(jnp.where(hit, score, 0.0), axis=0, keepdims=True))
        member = jnp.where(hit, 1.0, member)
        sel = jnp.where(hit, -jnp.inf, sel)
    gates = jnp.concatenate(gate, axis=0)
    gates = ROUTE_SCALE * gates / jnp.sum(gates, axis=0, keepdims=True)
    gate_ref[...] = jnp.concatenate([gates, jnp.zeros((LANES - TOP_K, tr), F32)], axis=0).T
    idx_ref[...] = jnp.concatenate(chosen, axis=0).astype(jnp.int32)
    before = jnp.dot(member.astype(BF16), tri_ref[...], preferred_element_type=F32) + carry_ref[:, 0:1]
    ranks = [jnp.sum(jnp.where(picked[k], before, 0.0), axis=0, keepdims=True) for k in range(TOP_K)]
    rank_ref[...] = jnp.concatenate(ranks, axis=0).astype(jnp.int32)
    carry_ref[...] = carry_ref[...] + jnp.sum(member, axis=1, keepdims=True)
    cnt_ref[...] = carry_ref[...]


def _route(logits_t, bias_col, tri, row0, n_tok):
    tr = tri.shape[0]
    off = row0 // tr
    k_map = lambda i: (0, i)
    const = lambda i: (0, 0)
    return pl.pallas_call(
        _route_body,
        grid=(n_tok // tr,),
        in_specs=[pl.BlockSpec((LANES, tr), lambda i: (0, i + off)), pl.BlockSpec((LANES, LANES), const),
                  pl.BlockSpec((tr, tr), const)],
        out_specs=[pl.BlockSpec((TOP_K, tr), k_map), pl.BlockSpec((tr, LANES), lambda i: (i, 0)),
                   pl.BlockSpec((TOP_K, tr), k_map), pl.BlockSpec((N_EXPERTS, LANES), const)],
        out_shape=[jax.ShapeDtypeStruct((TOP_K, n_tok), jnp.int32), jax.ShapeDtypeStruct((n_tok, LANES), F32),
                   jax.ShapeDtypeStruct((TOP_K, n_tok), jnp.int32), jax.ShapeDtypeStruct((N_EXPERTS, LANES), F32)],
        scratch_shapes=[pltpu.VMEM((N_EXPERTS, LANES), F32)],
        compiler_params=_params("arbitrary"),
        name="route",
    )(logits_t, bias_col, tri)


def _slot_body(idx_ref, rank_ref, pstart_ref, pos_ref):
    table = pstart_ref[...]
    for c in range(idx_ref.shape[1] // LANES):
        lanes = slice(LANES * c, LANES * (c + 1))
        pos_ref[:, lanes] = jnp.take_along_axis(table, idx_ref[:, lanes], axis=1) + rank_ref[:, lanes]


def _slot_positions(idx, rank, pstart):
    return pl.pallas_call(
        _slot_body,
        out_shape=jax.ShapeDtypeStruct(idx.shape, jnp.int32),
        compiler_params=pltpu.CompilerParams(vmem_limit_bytes=VMEM_LIMIT),
        name="slot_positions",
    )(idx, rank, pstart)


def _gather_rows(table, indices):
    n_idx = indices.shape[0]
    width = table.shape[1]
    mesh = plsc.VectorSubcoreMesh(core_axis_name="core", subcore_axis_name="subcore")

    @pl.kernel(out_type=jax.ShapeDtypeStruct((n_idx, width), table.dtype), mesh=mesh)
    def gather(table_hbm, idx_hbm, out_hbm):
        def body(idx_vmem, out_vmem):
            pltpu.sync_copy(table_hbm.at[idx_vmem.at[0]], out_vmem)

        pltpu.emit_pipeline(
            body,
            grid=(n_idx // SC_WINDOW,),
            in_specs=[pl.BlockSpec((1, SC_WINDOW), lambda i: (0, i))],
            out_specs=[pl.BlockSpec((SC_WINDOW, width), lambda i: (i, 0))],
            core_axis_name=("core", "subcore"),
            dimension_semantics=(pltpu.PARALLEL,),
        )(idx_hbm, out_hbm)

    return gather(table, indices.reshape(1, n_idx))


def _scatter_rows(table, n_planes, row0, n_rows, dest, n_out):
    n_idx = dest.shape[0]
    width = table.shape[1]
    plane_win = table.shape[0] // n_planes // SC_WINDOW
    win0 = row0 // SC_WINDOW
    win_per_plane = n_rows // SC_WINDOW
    win_per_rep_plane = n_idx // SC_WINDOW // n_planes
    mesh = plsc.VectorSubcoreMesh(core_axis_name="core", subcore_axis_name="subcore")

    @pl.kernel(out_type=jax.ShapeDtypeStruct((n_out, width), table.dtype), mesh=mesh)
    def scatter(table_hbm, idx_hbm, out_hbm):
        def body(rows_vmem, idx_vmem):
            pltpu.sync_copy(rows_vmem, out_hbm.at[idx_vmem.at[0]])

        pltpu.emit_pipeline(
            body,
            grid=(n_idx // SC_WINDOW,),
            in_specs=[pl.BlockSpec((SC_WINDOW, width),
                                   lambda i: ((i // win_per_rep_plane) * plane_win + win0 + i % win_per_plane, 0)),
                      pl.BlockSpec((1, SC_WINDOW), lambda i: (0, i))],
            out_specs=[],
            core_axis_name=("core", "subcore"),
            dimension_semantics=(pltpu.PARALLEL,),
        )(table_hbm, idx_hbm)

    return scatter(table, dest.reshape(1, n_idx))


def _ffn_body(be_ref, nv_ref, x_ref, wg_ref, wu_ref, wd_ref, y_ref, wgu_s, wd_s):
    r = pl.program_id(0)
    block = x_ref.shape[1]

    @pl.when((r == 0) | (be_ref[r] != be_ref[jnp.maximum(r - 1, 0)]))
    def _():
        wgu_s[:, :D_EXPERT] = wg_ref[0].astype(BF16)
        wgu_s[:, D_EXPERT:] = wu_ref[0].astype(BF16)
        wd_s[...] = wd_ref[0].astype(BF16)

    n_valid = nv_ref[r]

    def ffn(partial_block):
        x = jnp.concatenate(_unpack_rows(x_ref), axis=1)
        if partial_block:
            x = jnp.where(lax.broadcasted_iota(jnp.int32, (block, 1), 0) < n_valid, x, 0.0)
        gu = jnp.dot(x.astype(BF16), wgu_s[...], preferred_element_type=F32)
        gs = gu[:, :D_EXPERT]
        act = (gs / (1.0 + jnp.exp(-gs))) * gu[:, D_EXPERT:]
        _pack_rows(jnp.dot(act.astype(BF16), wd_s[...], preferred_element_type=F32), y_ref)

    pl.when(n_valid == block)(functools.partial(ffn, False))
    pl.when((n_valid > 0) & (n_valid < block))(functools.partial(ffn, True))

    @pl.when(n_valid == 0)
    def _():
        y_ref[...] = jnp.zeros(y_ref.shape, jnp.int32)


def _expert_ffn(xs, blk_expert, blk_valid, w_gate, w_up, w_down, block):
    n_slots = xs.shape[1]
    nb = n_slots // block
    slot_spec = pl.BlockSpec((N_PLANES, block, LANES), lambda r, be, nv: (0, r, 0))
    return pl.pallas_call(
        _ffn_body,
        grid_spec=pltpu.PrefetchScalarGridSpec(
            num_scalar_prefetch=2, grid=(nb,),
            in_specs=[slot_spec,
                      pl.BlockSpec((1, D_MODEL, D_EXPERT), lambda r, be, nv: (be[r], 0, 0)),
                      pl.BlockSpec((1, D_MODEL, D_EXPERT), lambda r, be, nv: (be[r], 0, 0)),
                      pl.BlockSpec((1, D_EXPERT, D_MODEL), lambda r, be, nv: (be[r], 0, 0))],
            out_specs=slot_spec,
            scratch_shapes=[pltpu.VMEM((D_MODEL, 2 * D_EXPERT), BF16), pltpu.VMEM((D_EXPERT, D_MODEL), BF16)]),
        out_shape=jax.ShapeDtypeStruct((N_PLANES, n_slots, LANES), jnp.int32),
        compiler_params=_params("arbitrary"),
        name="expert_ffn",
    )(blk_expert, blk_valid, xs, w_gate, w_up, w_down)


def _combine_body(xres_ref, gate_ref, *refs):
    y_refs, o_ref = refs[:TOP_K], refs[TOP_K]
    g = gate_ref[...]
    acc = [xres_ref[:, LANES * i:LANES * (i + 1)] for i in range(D_MODEL // LANES)]
    for k in range(TOP_K):
        gk = g[:, k:k + 1]
        acc = [a + gk * p for a, p in zip(acc, _unpack_rows(y_refs[k]))]
    o_ref[...] = jnp.concatenate(acc, axis=1)


def _combine(xres, row0, gates, ysg, tok0, n):
    tc = min(COMBINE_TILE, n)
    off = row0 // tc
    tok_off = tok0 // tc
    per_k = gates.shape[0] // tc
    row_map = lambda i: (i + off, 0)
    y_specs = [pl.BlockSpec((N_PLANES, tc, LANES), lambda i, k=k: (0, k * per_k + tok_off + i, 0))
               for k in range(TOP_K)]
    in_specs = [pl.BlockSpec((tc, D_MODEL), row_map), pl.BlockSpec((tc, LANES), lambda i: (tok_off + i, 0))] + y_specs
    return pl.pallas_call(
        _combine_body,
        grid=(n // tc,),
        in_specs=in_specs,
        out_specs=pl.BlockSpec((tc, D_MODEL), row_map),
        out_shape=jax.ShapeDtypeStruct(xres.shape, F32),
        input_output_aliases={0: 0},
        compiler_params=_params("parallel"),
        name="combine",
    )(xres, gates, *([ysg] * TOP_K))


def _prepare_weights(norm_mix_w, w_in, b_forget, q_norm_w, k_norm_w, lambda_re, lambda_im, log_dt, b_re, b_im,
                     c_re, c_im, d_skip, w_glu, b_glu, attn_out_norm_w, ssm_out_norm_w, w_out, norm_ffn_w,
                     w_router, router_bias, w_gate_e, w_up_e, w_down_e, w_gate_s, w_up_s, w_down_s):
    w = {}
    row = lambda a: a.reshape(1, -1).astype(F32)
    w['norm_mix_w'] = row(norm_mix_w)
    w['wqkv'] = w_in[:, :3 * D_ATTN].astype(BF16)
    w['wf'] = jnp.pad(w_in[:, 3 * D_ATTN:3 * D_ATTN + N_HEADS], ((0, 0), (0, LANES - N_HEADS))).astype(BF16)
    w['wu'] = w_in[:, 3 * D_ATTN + N_HEADS:].astype(BF16)
    w['b_forget'] = jnp.pad(row(b_forget), ((0, 0), (0, LANES - N_HEADS)))
    w['q_norm_w'] = jnp.tile(row(q_norm_w), (1, N_HEADS))
    w['k_norm_w'] = jnp.tile(row(k_norm_w), (1, N_HEADS))
    head = jnp.arange(D_ATTN) // HEAD_DIM
    w['seg'] = jnp.where(head[:, None] == head[None, :], 1.0 / HEAD_DIM, 0.0).astype(BF16)

    dt = jnp.exp(log_dt.astype(F32))[:, None]
    lre, lim = lambda_re.astype(F32), lambda_im.astype(F32)
    a, b = lre * dt, lim * dt
    ea = jnp.exp(a)
    bar_re, bar_im = ea * jnp.cos(b), ea * jnp.sin(b)
    num_re = jnp.expm1(a) * jnp.cos(b) - 2.0 * jnp.sin(0.5 * b) ** 2
    num_im = bar_im
    den = lre * lre + lim * lim
    coef_re = (num_re * lre + num_im * lim) / den
    coef_im = (num_im * lre - num_re * lim) / den
    bb_re = coef_re[:, :, None] * b_re - coef_im[:, :, None] * b_im
    bb_im = coef_re[:, :, None] * b_im + coef_im[:, :, None] * b_re
    eye = jnp.eye(N_SSM_GROUPS, dtype=F32)

    def in_block_diag(m):
        return (m.transpose(0, 2, 1)[:, :, None, :] * eye[:, None, :, None]).reshape(D_SSM, D_STATE)

    def out_block_diag(m):
        return (m.transpose(0, 2, 1)[:, :, None, :] * eye[:, None, :, None]).reshape(D_STATE, D_SSM)

    pairs = jnp.arange(N_SSM_GROUPS // 2)

    def pair_blocks(m):
        return m.reshape(4, LANES, N_SSM_GROUPS // 2, LANES).transpose(2, 0, 1, 3)[pairs, pairs // 4]

    w['ssm_bw'] = jnp.concatenate([pair_blocks(in_block_diag(bb_re)), pair_blocks(in_block_diag(bb_im))],
                                  axis=2).astype(BF16)
    tiles = jnp.arange(D_SSM // LANES)

    def tile_blocks(m):
        return m.reshape(4, 4 * LANES, 4, LANES).transpose(0, 2, 1, 3)[tiles, tiles]

    w['ssm_cre'] = tile_blocks(out_block_diag(c_re.astype(F32))).astype(BF16)
    w['ssm_cim'] = tile_blocks(out_block_diag(-c_im.astype(F32))).astype(BF16)
    w['ssm_ar'] = bar_re.reshape(1, D_STATE)
    w['ssm_ai'] = bar_im.reshape(1, D_STATE)
    w['d_skip'] = row(d_skip)
    w['w_glu'] = w_glu.astype(BF16)
    w['b_glu'] = row(b_glu)

    w['attn_out_norm_w'] = row(attn_out_norm_w)
    w['ssm_out_norm_w'] = row(ssm_out_norm_w)
    w['wo_a'] = w_out[:D_ATTN].astype(BF16)
    w['wo_s'] = w_out[D_ATTN:].astype(BF16)
    w['norm_ffn_w'] = row(norm_ffn_w)
    wr = jnp.pad(w_router.astype(F32), ((0, 0), (0, LANES - N_EXPERTS)))
    w['wr_hi'] = wr.astype(BF16)
    w['wr_lo'] = (wr - w['wr_hi'].astype(F32)).astype(BF16)
    w['router_bias'] = jnp.broadcast_to(jnp.pad(router_bias.astype(F32), (0, LANES - N_EXPERTS))[:, None],
                                        (LANES, LANES))
    w['wgu_s'] = jnp.concatenate([w_gate_s, w_up_s], axis=1).astype(BF16)
    w['wd_s'] = w_down_s.astype(BF16)
    w['w_gate_e'], w['w_up_e'], w['w_down_e'] = w_gate_e, w_up_e, w_down_e
    return w


def _mix_prompt(x, w, tables):
    bsz, length, _ = x.shape
    q, kt, vt, kb, vb, lf, u_tb = _inproj(x, w)
    ct = _cumsum_lanes(lf.reshape(bsz, length, N_HEADS).transpose(0, 2, 1))
    attn = _attn_prompt(q, kb, vb, ct, bsz, length)
    h0 = jnp.zeros((bsz, 2 * D_STATE), F32)
    ssm_lb, hlast = _ssm(u_tb, h0, w, bsz, length)
    xres, hn, logits = _outproj(attn, ssm_lb, x, w, tables, 0)
    heads_last = lambda a: a.reshape(1, bsz, N_HEADS, HEAD_DIM, length).transpose(0, 1, 4, 2, 3)
    caches = (heads_last(kt), heads_last(vt),
              lf.reshape(1, bsz, length, N_HEADS),
              hlast[:, :D_STATE].reshape(1, bsz, N_SSM_GROUPS, SSM_STATE),
              hlast[:, D_STATE:].reshape(1, bsz, N_SSM_GROUPS, SSM_STATE))
    return xres, hn, logits, caches


def _mix_sample(x, cache_k, cache_v, cache_logf, page_table, h0_re, h0_im, w, tables, row0):
    bsz, n_new, _ = x.shape
    rows = n_new * bsz
    xt = x.transpose(1, 0, 2).reshape(1, rows, D_MODEL)
    q, kt, vt, _, _, lf, u_tb = _inproj(xt, w)
    k, v = kt[0].T, vt[0].T
    to_bm = lambda a: a.reshape(n_new, bsz, -1).transpose(1, 0, 2)
    q_b, k_b, v_b, lf_b = to_bm(q), to_bm(k), to_bm(v), to_bm(lf)
    head = jnp.arange(D_ATTN) // HEAD_DIM
    hmask = (head[None, :] == jnp.arange(N_HEADS)[:, None]).astype(BF16)
    qbd = (q_b[:, None, :, :] * hmask[None, :, None, :]).reshape(bsz, N_HEADS * n_new, D_ATTN)
    pad_keys = ((0, 0), (0, PAGE_SIZE - n_new), (0, 0))
    n_pool = cache_k.shape[1]
    keys_minor = lambda c: c[0].transpose(0, 2, 3, 1).reshape(n_pool, D_ATTN, PAGE_SIZE)
    attn = _attn_sample(
        qbd, keys_minor(cache_k), keys_minor(cache_v),
        cache_logf[0].astype(F32).transpose(0, 2, 1), page_table.astype(jnp.int32),
        jnp.pad(k_b, pad_keys).transpose(0, 2, 1), jnp.pad(v_b, pad_keys).transpose(0, 2, 1),
        jnp.pad(lf_b.transpose(0, 2, 1), ((0, 0), (0, 0), (0, PAGE_SIZE - n_new))), n_new)
    attn_tm = attn.transpose(1, 0, 2).reshape(rows, D_ATTN)
    h0 = jnp.concatenate([h0_re.reshape(bsz, D_STATE), h0_im.reshape(bsz, D_STATE)], axis=1).astype(F32)
    ssm_lb, hlast = _ssm(u_tb.reshape(n_new, bsz * D_SSM), h0, w, bsz, n_new)
    xres, hn, logits = _outproj(attn_tm, ssm_lb.reshape(rows, D_SSM), xt, w, tables, row0)
    caches = (k_b.reshape(1, bsz, n_new, N_HEADS, HEAD_DIM), v_b.reshape(1, bsz, n_new, N_HEADS, HEAD_DIM),
              lf_b.reshape(1, bsz, n_new, N_HEADS),
              hlast[:, :D_STATE].reshape(1, bsz, N_SSM_GROUPS, SSM_STATE),
              hlast[:, D_STATE:].reshape(1, bsz, N_SSM_GROUPS, SSM_STATE))
    return xres, hn, logits, caches


def _moe(hn, logits, w, block, row0, n_tok):
    tri = (jnp.arange(ROUTE_TILE)[:, None] < jnp.arange(ROUTE_TILE)[None, :]).astype(BF16)
    idx, gates, rank, counts = _route(logits, w['router_bias'], tri, row0, n_tok)
    counts = counts[:, 0].astype(jnp.int32)
    padded = (counts + block - 1) // block * block
    experts = jnp.arange(N_EXPERTS, dtype=jnp.int32)
    pend = jnp.sum(jnp.where(experts[None, :] <= experts[:, None], padded[None, :], 0), axis=1)
    pstart = pend - padded
    n_blk = -(-(n_tok * TOP_K) // block) + N_EXPERTS
    n_slots = n_blk * block
    pstart_rows = jnp.broadcast_to(jnp.pad(pstart, (0, LANES - N_EXPERTS)), (TOP_K, LANES))
    pos = _slot_positions(idx, rank, pstart_rows).reshape(-1)
    blk_start = jnp.arange(n_blk, dtype=jnp.int32) * block
    blk_expert = jnp.minimum(jnp.sum(pend[None, :] <= blk_start[:, None], axis=1), N_EXPERTS - 1).astype(jnp.int32)
    seg_end = jnp.sum(jnp.where(blk_expert[:, None] == experts[None, :], (pstart + counts)[None, :], 0), axis=1)
    blk_valid = jnp.clip(seg_end - blk_start, 0, block).astype(jnp.int32)
    plane_pos = (pos[None, :] + n_slots * jnp.arange(N_PLANES, dtype=jnp.int32)[:, None]).reshape(-1)
    xs = _scatter_rows(hn.reshape(N_PLANES * hn.shape[1], LANES), N_PLANES, row0, n_tok, plane_pos,
                       N_PLANES * n_slots)
    ys = _expert_ffn(xs.reshape(N_PLANES, n_slots, LANES), blk_expert, blk_valid,
                     w['w_gate_e'], w['w_up_e'], w['w_down_e'], block)
    ysg = _gather_rows(ys.reshape(N_PLANES * n_slots, LANES), plane_pos)
    return gates, ysg.reshape(N_PLANES, TOP_K * n_tok, LANES)


def kernel(x_prompt, x_sample, cache_k, cache_v, cache_logf, page_table, state_ssm_re, state_ssm_im, norm_mix_w, w_in, b_forget, q_norm_w, k_norm_w, lambda_re, lambda_im, log_dt, b_re, b_im, c_re, c_im, d_skip, w_glu, b_glu, attn_out_norm_w, ssm_out_norm_w, w_out, norm_ffn_w, w_router, router_bias, w_gate_e, w_up_e, w_down_e, w_gate_s, w_up_s, w_down_s):
    assert norm_mix_w.shape[0] == 1, "single-layer trunk"
    w = _prepare_weights(norm_mix_w[0], w_in[0], b_forget[0], q_norm_w[0], k_norm_w[0], lambda_re[0], lambda_im[0],
                         log_dt[0], b_re[0], b_im[0], c_re[0], c_im[0], d_skip[0], w_glu[0], b_glu[0],
                         attn_out_norm_w[0], ssm_out_norm_w[0], w_out[0], norm_ffn_w[0], w_router[0], router_bias[0],
                         w_gate_e[0], w_up_e[0], w_down_e[0], w_gate_s[0], w_up_s[0], w_down_s[0])
    bp, lp, _ = x_prompt.shape
    bs, ls, _ = x_sample.shape
    n_p, n_s = bp * lp, bs * ls
    tables = (jnp.zeros((N_PLANES, n_p + n_s, LANES), jnp.int32), jnp.zeros((LANES, n_p + n_s), F32))
    y_p, hn, lg, caches_p = _mix_prompt(x_prompt, w, tables)
    y_s, hn, lg, caches_s = _mix_sample(x_sample, cache_k, cache_v, cache_logf, page_table,
                                        state_ssm_re[0], state_ssm_im[0], w, (hn, lg), n_p)
    group = n_p // MOE_GROUPS
    for g in range(MOE_GROUPS):
        last = g == MOE_GROUPS - 1
        gates, ysg = _moe(hn, lg, w, FFN_BLOCK, g * group, group + (n_s if last else 0))
        y_p = _combine(y_p, g * group, gates, ysg, 0, group)
    y_s = _combine(y_s, 0, gates, ysg, group, n_s).reshape(ls, bs, D_MODEL).transpose(1, 0, 2)
    y_p = y_p.reshape(bp, lp, D_MODEL)
    return (y_p, y_s) + caches_p + caches_s
```

```python
import functools
import math

import jax
import jax.numpy as jnp
from jax import lax
from jax.experimental import pallas as pl
from jax.experimental.pallas import tpu as pltpu
from jax.experimental.pallas import tpu_sc as plsc

F32 = jnp.float32
BF16 = jnp.bfloat16

D_MODEL = 1024
D_ATTN = 512
D_SSM = 512
HEAD_DIM = 64
N_HEADS = 8
ATTN_SCALE = HEAD_DIM ** -0.5
LOG2E = math.log2(math.e)
SSM_GROUP = 16
N_SSM_GROUPS = 32
SSM_STATE = 64
D_STATE = N_SSM_GROUPS * SSM_STATE
N_EXPERTS = 64
TOP_K = 8
D_EXPERT = 256
D_SHARED = 256
ROUTE_SCALE = 2.5
PAGE_SIZE = 128
RMS_EPS = 1e-6

LANES = 128
VMEM_LIMIT = 56 * 1024 * 1024

ROW_TILE = 512
ATTN_TILE = 1024
PAGES_PER_STEP = 32
ROUTE_TILE = 256
FFN_BLOCK = 512
MOE_GROUPS = 2
COMBINE_TILE = 256
SC_WINDOW = 128


def _params(*sem):
    return pltpu.CompilerParams(dimension_semantics=sem, vmem_limit_bytes=VMEM_LIMIT)


def _rms(x, w):
    return x * lax.rsqrt(jnp.mean(x * x, axis=-1, keepdims=True) + RMS_EPS) * w


def _lane_cumsum(x):
    n = x.shape[1]
    lane = lax.broadcasted_iota(jnp.int32, x.shape, 1)
    s = 1
    while s < n:
        x = x + jnp.where(lane >= s, pltpu.roll(x, s, axis=1), 0.0)
        s *= 2
    return x


def _inproj_body(x_ref, nw_ref, wqkv_ref, wf_ref, wu_ref, bf_ref, qnw_ref, knw_ref, seg_ref,
                 q_ref, k_ref, v_ref, kb_ref, vb_ref, lf_ref, u_ref):
    hn = _rms(x_ref[0], nw_ref[...]).astype(BF16)
    qkv = jnp.dot(hn, wqkv_ref[...], preferred_element_type=F32)
    seg = seg_ref[...]

    def head_norm(t, w):
        t2 = t * t
        hi = t2.astype(BF16)
        lo = (t2 - hi.astype(F32)).astype(BF16)
        ms = jnp.dot(hi, seg, preferred_element_type=F32) + jnp.dot(lo, seg, preferred_element_type=F32)
        return t * lax.rsqrt(ms + RMS_EPS) * w

    q = head_norm(qkv[:, :D_ATTN], qnw_ref[...])
    k = head_norm(qkv[:, D_ATTN:2 * D_ATTN], knw_ref[...])
    v = qkv[:, 2 * D_ATTN:]
    q_ref[...] = (q * (ATTN_SCALE * LOG2E)).astype(BF16)
    k_ref[0] = k.T
    v_ref[0] = v.T
    kb_ref[...] = k.astype(BF16)
    vb_ref[...] = v.astype(BF16)
    z = jnp.dot(hn, wf_ref[...], preferred_element_type=F32) + bf_ref[...]
    lf = jnp.minimum(z, 0.0) - jnp.log(1.0 + jnp.exp(-jnp.abs(z)))
    lf_ref[...] = lf[:, :N_HEADS]
    u_ref[...] = jnp.dot(hn, wu_ref[...], preferred_element_type=F32)


def _inproj(x3, w):
    bsz, length, _ = x3.shape
    tm = min(ROW_TILE, length)
    nt = length // tm
    rows = bsz * length
    row_map = lambda b, t: (b * nt + t, 0)
    const = lambda b, t: (0, 0)
    full = lambda a: pl.BlockSpec(a.shape, const)
    consts = [w['norm_mix_w'], w['wqkv'], w['wf'], w['wu'], w['b_forget'], w['q_norm_w'], w['k_norm_w'], w['seg']]
    return pl.pallas_call(
        _inproj_body,
        grid=(bsz, nt),
        in_specs=[pl.BlockSpec((1, tm, D_MODEL), lambda b, t: (b, t, 0))] + [full(a) for a in consts],
        out_specs=[pl.BlockSpec((tm, D_ATTN), row_map)]
        + [pl.BlockSpec((1, D_ATTN, tm), lambda b, t: (b, 0, t))] * 2
        + [pl.BlockSpec((tm, D_ATTN), row_map)] * 2
        + [pl.BlockSpec((tm, N_HEADS), row_map), pl.BlockSpec((tm, D_SSM), lambda b, t: (t, b))],
        out_shape=[jax.ShapeDtypeStruct((rows, D_ATTN), BF16),
                   jax.ShapeDtypeStruct((bsz, D_ATTN, length), F32),
                   jax.ShapeDtypeStruct((bsz, D_ATTN, length), F32),
                   jax.ShapeDtypeStruct((rows, D_ATTN), BF16),
                   jax.ShapeDtypeStruct((rows, D_ATTN), BF16),
                   jax.ShapeDtypeStruct((rows, N_HEADS), F32),
                   jax.ShapeDtypeStruct((length, bsz * D_SSM), F32)],
        compiler_params=_params("parallel", "parallel"),
        name="inproj",
    )(x3, *consts)


def _cumsum_body(lf_ref, c_ref):
    c_ref[0] = _lane_cumsum(lf_ref[0])


def _cumsum_lanes(lft):
    bsz, nh, length = lft.shape
    spec = pl.BlockSpec((1, nh, length), lambda b: (b, 0, 0))
    return pl.pallas_call(
        _cumsum_body, grid=(bsz,), in_specs=[spec], out_specs=spec,
        out_shape=jax.ShapeDtypeStruct(lft.shape, F32),
        compiler_params=_params("parallel"), name="logf_cumsum",
    )(lft)


def _attn_prompt_body(q_ref, k_ref, v_ref, ct_ref, o_ref, acc_ref, m_ref, *, tile):
    i = pl.program_id(1)
    lane = lax.broadcasted_iota(jnp.int32, (1, LANES), 1)
    row = lax.broadcasted_iota(jnp.int32, (tile, tile), 0)
    col = lax.broadcasted_iota(jnp.int32, (tile, tile), 1)
    hmasks = (lane < HEAD_DIM, lane >= HEAD_DIM)
    sum_lane = (HEAD_DIM, 0)
    ones_col = [jnp.where(lane == sl, 1.0, 0.0).astype(BF16) for sl in sum_lane]
    for hp in range(N_HEADS // 2):
        lanes = slice(LANES * hp, LANES * (hp + 1))
        qp = q_ref[:, lanes]
        qh = [jnp.where(hm, qp, jnp.zeros_like(qp)) for hm in hmasks]
        m_ref[...] = jnp.full(m_ref.shape, -jnp.inf, F32)
        acc_ref[...] = jnp.zeros(acc_ref.shape, F32)

        def kstep(j, diagonal, qh=qh, lanes=lanes, hp=hp):
            r0 = pl.multiple_of(j * tile, tile)
            kj = k_ref[pl.ds(r0, tile), lanes]
            vj = v_ref[pl.ds(r0, tile), lanes]
            for hh in range(2):
                s = lax.dot_general(qh[hh], kj, (((1,), (1,)), ((), ())), preferred_element_type=F32)
                s = s - ct_ref[0, 2 * hp + hh:2 * hp + hh + 1, pl.ds(r0, tile)] * LOG2E
                if diagonal:
                    s = jnp.where(row >= col, s, -jnp.inf)
                m_old = m_ref[hh]
                m_new = jnp.maximum(m_old, jnp.max(s, axis=1, keepdims=True))
                alpha = jnp.exp2(m_old - m_new)
                p = jnp.exp2(s - m_new).astype(BF16)
                vh = jnp.where(hmasks[hh], vj, ones_col[hh])
                acc_ref[hh] = alpha * acc_ref[hh] + jnp.dot(p, vh, preferred_element_type=F32)
                m_ref[hh] = m_new

        def body(j, carry):
            kstep(j, False)
            return carry

        lax.fori_loop(0, i, body, 0)
        kstep(i, True)
        acc0, acc1 = acc_ref[0], acc_ref[1]
        o_ref[:, lanes] = jnp.where(hmasks[0], acc0 / acc0[:, sum_lane[0]:sum_lane[0] + 1],
                                    acc1 / acc1[:, sum_lane[1]:sum_lane[1] + 1])


def _attn_prompt(q, kb, vb, ct, bsz, length):
    tile = min(ATTN_TILE, length)
    nq = length // tile
    q_map = lambda b, i: (b * nq + i, 0)
    seq_map = lambda b, i: (b, 0)
    return pl.pallas_call(
        functools.partial(_attn_prompt_body, tile=tile),
        grid=(bsz, nq),
        in_specs=[pl.BlockSpec((tile, D_ATTN), q_map),
                  pl.BlockSpec((length, D_ATTN), seq_map),
                  pl.BlockSpec((length, D_ATTN), seq_map),
                  pl.BlockSpec((1, N_HEADS, length), lambda b, i: (b, 0, 0))],
        out_specs=pl.BlockSpec((tile, D_ATTN), q_map),
        out_shape=jax.ShapeDtypeStruct((bsz * length, D_ATTN), F32),
        scratch_shapes=[pltpu.VMEM((2, tile, LANES), F32), pltpu.VMEM((2, tile, 1), F32)],
        compiler_params=_params("parallel", "parallel"),
        name="attn_prompt",
    )(q, kb, vb, ct)


def _attn_sample_body(pt_ref, q_ref, *refs, pages, n_new):
    del pt_ref
    k_refs, v_refs, lf_refs = refs[:pages], refs[pages:2 * pages], refs[2 * pages:3 * pages]
    kn_ref, vn_ref, lfn_ref, o_ref, m_ref, l_ref, acc_ref, c_ref = refs[3 * pages:]
    j = pl.program_id(1)
    rows = N_HEADS * n_new

    @pl.when(j == 0)
    def _():
        m_ref[...] = jnp.full(m_ref.shape, -jnp.inf, F32)
        l_ref[...] = jnp.zeros(l_ref.shape, F32)
        acc_ref[...] = jnp.zeros(acc_ref.shape, F32)
        c_ref[...] = jnp.zeros(c_ref.shape, F32)

    q = q_ref[0]

    def chunk(kc, vc, lft, valid):
        n = kc.shape[1]
        s = jnp.dot(q, kc, preferred_element_type=F32)
        ck = _lane_cumsum(lft) + c_ref[:, 0:1]
        c_ref[...] = jnp.broadcast_to(ck[:, n - 1:n], c_ref.shape)
        ck2 = ck * LOG2E
        s = s - jnp.concatenate([jnp.broadcast_to(ck2[h:h + 1], (n_new, n)) for h in range(N_HEADS)], axis=0)
        if valid is not None:
            s = jnp.where(valid, s, -jnp.inf)
        m_old = m_ref[...]
        m_new = jnp.maximum(m_old, jnp.max(s, axis=1, keepdims=True))
        alpha = jnp.exp2(m_old - m_new)
        p = jnp.exp2(s - m_new)
        l_ref[...] = alpha * l_ref[...] + jnp.sum(p, axis=1, keepdims=True)
        acc_ref[...] = alpha * acc_ref[...] + lax.dot_general(
            p.astype(BF16), vc, (((1,), (1,)), ((), ())), preferred_element_type=F32)
        m_ref[...] = m_new

    chunk(jnp.concatenate([r[0] for r in k_refs], axis=1).astype(BF16),
          jnp.concatenate([r[0] for r in v_refs], axis=1).astype(BF16),
          jnp.concatenate([r[0] for r in lf_refs], axis=1), None)

    @pl.when(j == pl.num_programs(1) - 1)
    def _():
        qi = lax.broadcasted_iota(jnp.int32, (rows, PAGE_SIZE), 0) % n_new
        kj = lax.broadcasted_iota(jnp.int32, (rows, PAGE_SIZE), 1)
        chunk(kn_ref[0].astype(BF16), vn_ref[0].astype(BF16), lfn_ref[0], kj <= qi)
        o_full = acc_ref[...] / l_ref[...]
        lane = lax.broadcasted_iota(jnp.int32, (1, D_ATTN), 1)
        out = jnp.zeros((n_new, D_ATTN), F32)
        for h in range(N_HEADS):
            hmask = (lane >= HEAD_DIM * h) & (lane < HEAD_DIM * (h + 1))
            out = out + jnp.where(hmask, o_full[n_new * h:n_new * (h + 1)], 0.0)
        o_ref[0] = out


def _attn_sample(qbd, cache_k, cache_v, cache_lft, page_table, kn, vn, lfn, n_new):
    bsz, n_pages = page_table.shape
    pages = min(PAGES_PER_STEP, n_pages)
    nj = n_pages // pages
    rows = N_HEADS * n_new

    def page_map(p):
        return lambda b, j, pt: (pt[b * n_pages + j * pages + p], 0, 0)

    seq_map = lambda b, j, pt: (b, 0, 0)
    in_specs = [pl.BlockSpec((1, rows, D_ATTN), seq_map)]
    in_specs += [pl.BlockSpec((1, D_ATTN, PAGE_SIZE), page_map(p)) for p in range(pages)]
    in_specs += [pl.BlockSpec((1, D_ATTN, PAGE_SIZE), page_map(p)) for p in range(pages)]
    in_specs += [pl.BlockSpec((1, N_HEADS, PAGE_SIZE), page_map(p)) for p in range(pages)]
    in_specs += [pl.BlockSpec((1, D_ATTN, PAGE_SIZE), seq_map)] * 2 + [pl.BlockSpec((1, N_HEADS, PAGE_SIZE), seq_map)]
    return pl.pallas_call(
        functools.partial(_attn_sample_body, pages=pages, n_new=n_new),
        grid_spec=pltpu.PrefetchScalarGridSpec(
            num_scalar_prefetch=1, grid=(bsz, nj), in_specs=in_specs,
            out_specs=pl.BlockSpec((1, n_new, D_ATTN), seq_map),
            scratch_shapes=[pltpu.VMEM((rows, 1), F32), pltpu.VMEM((rows, 1), F32),
                            pltpu.VMEM((rows, D_ATTN), F32), pltpu.VMEM((N_HEADS, LANES), F32)]),
        out_shape=jax.ShapeDtypeStruct((bsz, n_new, D_ATTN), F32),
        compiler_params=_params("parallel", "arbitrary"),
        name="attn_sample",
    )(page_table.reshape(-1), qbd, *([cache_k] * pages), *([cache_v] * pages), *([cache_lft] * pages), kn, vn, lfn)


SCAN_LANES = 512


def _ssm_body(u_ref, h0_ref, bw_ref, cre_ref, cim_ref, ar_ref, ai_ref, dsk_ref, wglu_ref, bglu_ref,
              out_ref, hlast_ref, hbuf_ref, state_ref, ubuf_ref, *, steps, bsz):
    g = pl.program_id(0)

    @pl.when(g == 0)
    def _():
        state_ref[...] = h0_ref[...]

    n_tiles = D_SSM // LANES
    for b in range(bsz):
        for t in range(n_tiles):
            col = D_SSM * b + LANES * t
            ubuf_ref[t, pl.ds(b, steps, stride=bsz), :] = u_ref[:, col:col + LANES]
    u = jnp.concatenate([ubuf_ref[t] for t in range(n_tiles)], axis=1)
    ub = u.astype(BF16)
    for p in range(N_SSM_GROUPS // 2):
        t = p // 4
        bu = jnp.dot(ub[:, LANES * t:LANES * (t + 1)], bw_ref[p], preferred_element_type=F32)
        hbuf_ref[:, LANES * p:LANES * (p + 1)] = bu[:, :LANES]
        hbuf_ref[:, D_STATE + LANES * p:D_STATE + LANES * (p + 1)] = bu[:, LANES:]

    for c in range(D_STATE // SCAN_LANES):
        re_l = slice(SCAN_LANES * c, SCAN_LANES * (c + 1))
        im_l = slice(D_STATE + SCAN_LANES * c, D_STATE + SCAN_LANES * (c + 1))
        ar = jnp.broadcast_to(ar_ref[:, re_l], (bsz, SCAN_LANES))
        ai = jnp.broadcast_to(ai_ref[:, re_l], (bsz, SCAN_LANES))

        def step(t, carry, re_l=re_l, im_l=im_l, ar=ar, ai=ai):
            re, im = carry
            r0 = pl.multiple_of(t * bsz, bsz)
            nre = ar * re - ai * im + hbuf_ref[pl.ds(r0, bsz), re_l]
            nim = ar * im + ai * re + hbuf_ref[pl.ds(r0, bsz), im_l]
            hbuf_ref[pl.ds(r0, bsz), re_l] = nre
            hbuf_ref[pl.ds(r0, bsz), im_l] = nim
            return nre, nim

        re, im = lax.fori_loop(0, steps, step, (state_ref[:, re_l], state_ref[:, im_l]))
        state_ref[:, re_l] = re
        state_ref[:, im_l] = im

    hlast_ref[...] = state_ref[...]

    ys = []
    for t in range(D_SSM // LANES):
        w = 4 * LANES
        hre = hbuf_ref[:, w * t:w * (t + 1)].astype(BF16)
        him = hbuf_ref[:, D_STATE + w * t:D_STATE + w * (t + 1)].astype(BF16)
        ys.append(jnp.dot(hre, cre_ref[t], preferred_element_type=F32)
                  + jnp.dot(him, cim_ref[t], preferred_element_type=F32))
    y = jnp.concatenate(ys, axis=1) + dsk_ref[...] * u
    z = 0.5 * y * (1.0 + jnp.tanh(math.sqrt(2.0 / math.pi) * (y + 0.044715 * (y * y * y))))
    gate = jnp.dot(z.astype(BF16), wglu_ref[...], preferred_element_type=F32) + bglu_ref[...]
    out = z / (1.0 + jnp.exp(-gate))
    for t in range(n_tiles):
        ubuf_ref[t] = out[:, LANES * t:LANES * (t + 1)]
    for b in range(bsz):
        for t in range(n_tiles):
            col = D_SSM * b + LANES * t
            out_ref[:, col:col + LANES] = ubuf_ref[t, pl.ds(b, steps, stride=bsz), :]


def _ssm(u_lb, h0, w, bsz, length):
    steps = max(1, min(length, ROW_TILE // bsz))
    rows = steps * bsz
    const2 = lambda g: (0, 0)
    const3 = lambda g: (0, 0, 0)
    full = lambda a: pl.BlockSpec(a.shape, const2 if a.ndim == 2 else const3)
    consts = [h0, w['ssm_bw'], w['ssm_cre'], w['ssm_cim'], w['ssm_ar'], w['ssm_ai'], w['d_skip'], w['w_glu'], w['b_glu']]
    return pl.pallas_call(
        functools.partial(_ssm_body, steps=steps, bsz=bsz),
        grid=(length // steps,),
        in_specs=[pl.BlockSpec((steps, bsz * D_SSM), lambda g: (g, 0))] + [full(a) for a in consts],
        out_specs=[pl.BlockSpec((steps, bsz * D_SSM), lambda g: (g, 0)), pl.BlockSpec((bsz, 2 * D_STATE), const2)],
        out_shape=[jax.ShapeDtypeStruct((length, bsz * D_SSM), F32), jax.ShapeDtypeStruct((bsz, 2 * D_STATE), F32)],
        scratch_shapes=[pltpu.VMEM((rows, 2 * D_STATE), F32), pltpu.VMEM((bsz, 2 * D_STATE), F32),
                        pltpu.VMEM((D_SSM // LANES, rows, LANES), F32)],
        compiler_params=_params("arbitrary"),
        name="ssm",
    )(u_lb, *consts)


N_PLANES = D_MODEL // (2 * LANES)


def _pack_rows(x, out_ref):
    bits = lax.bitcast_convert_type(x.astype(BF16).astype(F32), jnp.uint32)
    for c in range(N_PLANES):
        lo = bits[:, 2 * LANES * c:2 * LANES * c + LANES]
        hi = bits[:, 2 * LANES * c + LANES:2 * LANES * (c + 1)]
        out_ref[c] = lax.bitcast_convert_type(hi | (lo >> 16), jnp.int32)


def _unpack_rows(ref):
    parts = []
    for c in range(N_PLANES):
        bits = lax.bitcast_convert_type(ref[c], jnp.uint32)
        parts.append(lax.bitcast_convert_type(bits << 16, F32))
        parts.append(lax.bitcast_convert_type(bits & jnp.uint32(0xFFFF0000), F32))
    return parts


def _outproj_body(attn_ref, ssm_ref, x_ref, aw_ref, sw_ref, woa_ref, wos_ref, nfw_ref, wrh_ref, wrl_ref,
                  wgus_ref, wds_ref, hn_table_ref, logit_table_ref, xres_ref, hn_ref, logit_ref):
    del hn_table_ref, logit_table_ref
    an = _rms(attn_ref[...], aw_ref[...]).astype(BF16)
    sn = _rms(ssm_ref[...], sw_ref[...]).astype(BF16)
    x1 = x_ref[0] + (jnp.dot(an, woa_ref[...], preferred_element_type=F32)
                     + jnp.dot(sn, wos_ref[...], preferred_element_type=F32))
    h2 = _rms(x1, nfw_ref[...])
    hb = h2.astype(BF16)
    hlo = (h2 - hb.astype(F32)).astype(BF16)
    _pack_rows(h2, hn_ref)
    logit_ref[...] = (jnp.dot(hb, wrh_ref[...], preferred_element_type=F32)
                      + (jnp.dot(hb, wrl_ref[...], preferred_element_type=F32)
                         + jnp.dot(hlo, wrh_ref[...], preferred_element_type=F32))).T
    gu = jnp.dot(hb, wgus_ref[...], preferred_element_type=F32)
    gs = gu[:, :D_SHARED]
    act = (gs / (1.0 + jnp.exp(-gs))) * gu[:, D_SHARED:]
    xres_ref[...] = x1 + jnp.dot(act.astype(BF16), wds_ref[...], preferred_element_type=F32)


def _outproj(attn, ssm_tb, x3, w, tables, row0):
    table_rows = tables[1].shape[1]
    bsz, length, _ = x3.shape
    tm = min(ROW_TILE, length)
    nt = length // tm
    rows = bsz * length
    off = row0 // tm
    row_map = lambda b, t: (b * nt + t, 0)
    const = lambda b, t: (0, 0)
    full = lambda a: pl.BlockSpec(a.shape, const)
    consts = [w['attn_out_norm_w'], w['ssm_out_norm_w'], w['wo_a'], w['wo_s'], w['norm_ffn_w'],
              w['wr_hi'], w['wr_lo'], w['wgu_s'], w['wd_s']]
    in_specs = [pl.BlockSpec((tm, D_ATTN), row_map), pl.BlockSpec((tm, D_SSM), lambda b, t: (t, b)),
                pl.BlockSpec((1, tm, D_MODEL), lambda b, t: (b, t, 0))] + [full(a) for a in consts]
    args = [attn, ssm_tb, x3] + consts + list(tables)
    in_specs += [pl.BlockSpec(memory_space=pl.ANY)] * 2
    aliases = {len(args) - 2: 1, len(args) - 1: 2}
    return pl.pallas_call(
        _outproj_body,
        grid=(bsz, nt),
        in_specs=in_specs,
        out_specs=[pl.BlockSpec((tm, D_MODEL), row_map),
                   pl.BlockSpec((N_PLANES, tm, LANES), lambda b, t: (0, off + b * nt + t, 0)),
                   pl.BlockSpec((LANES, tm), lambda b, t: (0, off + b * nt + t))],
        out_shape=[jax.ShapeDtypeStruct((rows, D_MODEL), F32),
                   jax.ShapeDtypeStruct((N_PLANES, table_rows, LANES), jnp.int32),
                   jax.ShapeDtypeStruct((LANES, table_rows), F32)],
        input_output_aliases=aliases,
        compiler_params=_params("parallel", "parallel"),
        name="outproj",
    )(*args)


def _route_body(logit_ref, bias_ref, tri_ref, idx_ref, gate_ref, rank_ref, cnt_ref, carry_ref):
    @pl.when(pl.program_id(0) == 0)
    def _():
        carry_ref[...] = jnp.zeros(carry_ref.shape, F32)

    lg = logit_ref[:N_EXPERTS, :]
    tr = lg.shape[1]
    row = lax.broadcasted_iota(jnp.int32, (N_EXPERTS, tr), 0).astype(F32)
    score = 1.0 / (1.0 + jnp.exp(-lg))
    sel = score + bias_ref[:N_EXPERTS, 0:1]
    picked, chosen, gate = [], [], []
    member = jnp.zeros((N_EXPERTS, tr), F32)
    for k in range(TOP_K):
        best = jnp.max(sel, axis=0, keepdims=True)
        e = jnp.min(jnp.where(sel == best, row, float(N_EXPERTS)), axis=0, keepdims=True)
        hit = row == e
        picked.append(hit)
        chosen.append(e)
        gate.append(jnp.sum(jnp.where(hit, score, 0.0), axis=0, keepdims=True))
        member = jnp.where(hit, 1.0, member)
        sel = jnp.where(hit, -jnp.inf, sel)
    gates = jnp.concatenate(gate, axis=0)
    gates = ROUTE_SCALE * gates / jnp.sum(gates, axis=0, keepdims=True)
    gate_ref[...] = jnp.concatenate([gates, jnp.zeros((LANES - TOP_K, tr), F32)], axis=0).T
    idx_ref[...] = jnp.concatenate(chosen, axis=0).astype(jnp.int32)
    before = jnp.dot(member.astype(BF16), tri_ref[...], preferred_element_type=F32) + carry_ref[:, 0:1]
    ranks = [jnp.sum(jnp.where(picked[k], before, 0.0), axis=0, keepdims=True) for k in range(TOP_K)]
    rank_ref[...] = jnp.concatenate(ranks, axis=0).astype(jnp.int32)
    carry_ref[...] = carry_ref[...] + jnp.sum(member, axis=1, keepdims=True)
    cnt_ref[...] = carry_ref[...]


def _route(logits_t, bias_col, tri, row0, n_tok):
    tr = tri.shape[0]
    off = row0 // tr
    k_map = lambda i: (0, i)
    const = lambda i: (0, 0)
    return pl.pallas_call(
        _route_body,
        grid=(n_tok // tr,),
        in_specs=[pl.BlockSpec((LANES, tr), lambda i: (0, i + off)), pl.BlockSpec((LANES, LANES), const),
                  pl.BlockSpec((tr, tr), const)],
        out_specs=[pl.BlockSpec((TOP_K, tr), k_map), pl.BlockSpec((tr, LANES), lambda i: (i, 0)),
                   pl.BlockSpec((TOP_K, tr), k_map), pl.BlockSpec((N_EXPERTS, LANES), const)],
        out_shape=[jax.ShapeDtypeStruct((TOP_K, n_tok), jnp.int32), jax.ShapeDtypeStruct((n_tok, LANES), F32),
                   jax.ShapeDtypeStruct((TOP_K, n_tok), jnp.int32), jax.ShapeDtypeStruct((N_EXPERTS, LANES), F32)],
        scratch_shapes=[pltpu.VMEM((N_EXPERTS, LANES), F32)],
        compiler_params=_params("arbitrary"),
        name="route",
    )(logits_t, bias_col, tri)


def _slot_body(idx_ref, rank_ref, pstart_ref, pos_ref):
    table = pstart_ref[...]
    for c in range(idx_ref.shape[1] // LANES):
        lanes = slice(LANES * c, LANES * (c + 1))
        pos_ref[:, lanes] = jnp.take_along_axis(table, idx_ref[:, lanes], axis=1) + rank_ref[:, lanes]


def _slot_positions(idx, rank, pstart):
    return pl.pallas_call(
        _slot_body,
        out_shape=jax.ShapeDtypeStruct(idx.shape, jnp.int32),
        compiler_params=pltpu.CompilerParams(vmem_limit_bytes=VMEM_LIMIT),
        name="slot_positions",
    )(idx, rank, pstart)


def _gather_rows(table, indices):
    n_idx = indices.shape[0]
    width = table.shape[1]
    mesh = plsc.VectorSubcoreMesh(core_axis_name="core", subcore_axis_name="subcore")

    @pl.kernel(out_type=jax.ShapeDtypeStruct((n_idx, width), table.dtype), mesh=mesh)
    def gather(table_hbm, idx_hbm, out_hbm):
        def body(idx_vmem, out_vmem):
            pltpu.sync_copy(table_hbm.at[idx_vmem.at[0]], out_vmem)

        pltpu.emit_pipeline(
            body,
            grid=(n_idx // SC_WINDOW,),
            in_specs=[pl.BlockSpec((1, SC_WINDOW), lambda i: (0, i))],
            out_specs=[pl.BlockSpec((SC_WINDOW, width), lambda i: (i, 0))],
            core_axis_name=("core", "subcore"),
            dimension_semantics=(pltpu.PARALLEL,),
        )(idx_hbm, out_hbm)

    return gather(table, indices.reshape(1, n_idx))


def _scatter_rows(table, n_planes, row0, n_rows, dest, n_out):
    n_idx = dest.shape[0]
    width = table.shape[1]
    plane_win = table.shape[0] // n_planes // SC_WINDOW
    win0 = row0 // SC_WINDOW
    win_per_plane = n_rows // SC_WINDOW
    win_per_rep_plane = n_idx // SC_WINDOW // n_planes
    mesh = plsc.VectorSubcoreMesh(core_axis_name="core", subcore_axis_name="subcore")

    @pl.kernel(out_type=jax.ShapeDtypeStruct((n_out, width), table.dtype), mesh=mesh)
    def scatter(table_hbm, idx_hbm, out_hbm):
        def body(rows_vmem, idx_vmem):
            pltpu.sync_copy(rows_vmem, out_hbm.at[idx_vmem.at[0]])

        pltpu.emit_pipeline(
            body,
            grid=(n_idx // SC_WINDOW,),
            in_specs=[pl.BlockSpec((SC_WINDOW, width),
                                   lambda i: ((i // win_per_rep_plane) * plane_win + win0 + i % win_per_plane, 0)),
                      pl.BlockSpec((1, SC_WINDOW), lambda i: (0, i))],
            out_specs=[],
            core_axis_name=("core", "subcore"),
            dimension_semantics=(pltpu.PARALLEL,),
        )(table_hbm, idx_hbm)

    return scatter(table, dest.reshape(1, n_idx))


def _ffn_body(be_ref, nv_ref, x_ref, wg_ref, wu_ref, wd_ref, y_ref, wgu_s, wd_s):
    r = pl.program_id(0)
    block = x_ref.shape[1]

    @pl.when((r == 0) | (be_ref[r] != be_ref[jnp.maximum(r - 1, 0)]))
    def _():
        wgu_s[:, :D_EXPERT] = wg_ref[0].astype(BF16)
        wgu_s[:, D_EXPERT:] = wu_ref[0].astype(BF16)
        wd_s[...] = wd_ref[0].astype(BF16)

    n_valid = nv_ref[r]

    def ffn(partial_block):
        x = jnp.concatenate(_unpack_rows(x_ref), axis=1)
        if partial_block:
            x = jnp.where(lax.broadcasted_iota(jnp.int32, (block, 1), 0) < n_valid, x, 0.0)
        gu = jnp.dot(x.astype(BF16), wgu_s[...], preferred_element_type=F32)
        gs = gu[:, :D_EXPERT]
        act = (gs / (1.0 + jnp.exp(-gs))) * gu[:, D_EXPERT:]
        _pack_rows(jnp.dot(act.astype(BF16), wd_s[...], preferred_element_type=F32), y_ref)

    pl.when(n_valid == block)(functools.partial(ffn, False))
    pl.when((n_valid > 0) & (n_valid < block))(functools.partial(ffn, True))

    @pl.when(n_valid == 0)
    def _():
        y_ref[...] = jnp.zeros(y_ref.shape, jnp.int32)


def _expert_ffn(xs, blk_expert, blk_valid, w_gate, w_up, w_down, block):
    n_slots = xs.shape[1]
    nb = n_slots // block
    slot_spec = pl.BlockSpec((N_PLANES, block, LANES), lambda r, be, nv: (0, r, 0))
    return pl.pallas_call(
        _ffn_body,
        grid_spec=pltpu.PrefetchScalarGridSpec(
            num_scalar_prefetch=2, grid=(nb,),
            in_specs=[slot_spec,
                      pl.BlockSpec((1, D_MODEL, D_EXPERT), lambda r, be, nv: (be[r], 0, 0)),
                      pl.BlockSpec((1, D_MODEL, D_EXPERT), lambda r, be, nv: (be[r], 0, 0)),
                      pl.BlockSpec((1, D_EXPERT, D_MODEL), lambda r, be, nv: (be[r], 0, 0))],
            out_specs=slot_spec,
            scratch_shapes=[pltpu.VMEM((D_MODEL, 2 * D_EXPERT), BF16), pltpu.VMEM((D_EXPERT, D_MODEL), BF16)]),
        out_shape=jax.ShapeDtypeStruct((N_PLANES, n_slots, LANES), jnp.int32),
        compiler_params=_params("arbitrary"),
        name="expert_ffn",
    )(blk_expert, blk_valid, xs, w_gate, w_up, w_down)


def _combine_body(xres_ref, gate_ref, *refs):
    y_refs, o_ref = refs[:TOP_K], refs[TOP_K]
    g = gate_ref[...]
    acc = [xres_ref[:, LANES * i:LANES * (i + 1)] for i in range(D_MODEL // LANES)]
    for k in range(TOP_K):
        gk = g[:, k:k + 1]
        acc = [a + gk * p for a, p in zip(acc, _unpack_rows(y_refs[k]))]
    o_ref[...] = jnp.concatenate(acc, axis=1)


def _combine(xres, row0, gates, ysg, tok0, n):
    tc = min(COMBINE_TILE, n)
    off = row0 // tc
    tok_off = tok0 // tc
    per_k = gates.shape[0] // tc
    row_map = lambda i: (i + off, 0)
    y_specs = [pl.BlockSpec((N_PLANES, tc, LANES), lambda i, k=k: (0, k * per_k + tok_off + i, 0))
               for k in range(TOP_K)]
    in_specs = [pl.BlockSpec((tc, D_MODEL), row_map), pl.BlockSpec((tc, LANES), lambda i: (tok_off + i, 0))] + y_specs
    return pl.pallas_call(
        _combine_body,
        grid=(n // tc,),
        in_specs=in_specs,
        out_specs=pl.BlockSpec((tc, D_MODEL), row_map),
        out_shape=jax.ShapeDtypeStruct(xres.shape, F32),
        input_output_aliases={0: 0},
        compiler_params=_params("parallel"),
        name="combine",
    )(xres, gates, *([ysg] * TOP_K))


def _prepare_weights(norm_mix_w, w_in, b_forget, q_norm_w, k_norm_w, lambda_re, lambda_im, log_dt, b_re, b_im,
                     c_re, c_im, d_skip, w_glu, b_glu, attn_out_norm_w, ssm_out_norm_w, w_out, norm_ffn_w,
                     w_router, router_bias, w_gate_e, w_up_e, w_down_e, w_gate_s, w_up_s, w_down_s):
    w = {}
    row = lambda a: a.reshape(1, -1).astype(F32)
    w['norm_mix_w'] = row(norm_mix_w)
    w['wqkv'] = w_in[:, :3 * D_ATTN].astype(BF16)
    w['wf'] = jnp.pad(w_in[:, 3 * D_ATTN:3 * D_ATTN + N_HEADS], ((0, 0), (0, LANES - N_HEADS))).astype(BF16)
    w['wu'] = w_in[:, 3 * D_ATTN + N_HEADS:].astype(BF16)
    w['b_forget'] = jnp.pad(row(b_forget), ((0, 0), (0, LANES - N_HEADS)))
    w['q_norm_w'] = jnp.tile(row(q_norm_w), (1, N_HEADS))
    w['k_norm_w'] = jnp.tile(row(k_norm_w), (1, N_HEADS))
    head = jnp.arange(D_ATTN) // HEAD_DIM
    w['seg'] = jnp.where(head[:, None] == head[None, :], 1.0 / HEAD_DIM, 0.0).astype(BF16)

    dt = jnp.exp(log_dt.astype(F32))[:, None]
    lre, lim = lambda_re.astype(F32), lambda_im.astype(F32)
    a, b = lre * dt, lim * dt
    ea = jnp.exp(a)
    bar_re, bar_im = ea * jnp.cos(b), ea * jnp.sin(b)
    num_re = jnp.expm1(a) * jnp.cos(b) - 2.0 * jnp.sin(0.5 * b) ** 2
    num_im = bar_im
    den = lre * lre + lim * lim
    coef_re = (num_re * lre + num_im * lim) / den
    coef_im = (num_im * lre - num_re * lim) / den
    bb_re = coef_re[:, :, None] * b_re - coef_im[:, :, None] * b_im
    bb_im = coef_re[:, :, None] * b_im + coef_im[:, :, None] * b_re
    eye = jnp.eye(N_SSM_GROUPS, dtype=F32)

    def in_block_diag(m):
        return (m.transpose(0, 2, 1)[:, :, None, :] * eye[:, None, :, None]).reshape(D_SSM, D_STATE)

    def out_block_diag(m):
        return (m.transpose(0, 2, 1)[:, :, None, :] * eye[:, None, :, None]).reshape(D_STATE, D_SSM)

    pairs = jnp.arange(N_SSM_GROUPS // 2)

    def pair_blocks(m):
        return m.reshape(4, LANES, N_SSM_GROUPS // 2, LANES).transpose(2, 0, 1, 3)[pairs, pairs // 4]

    w['ssm_bw'] = jnp.concatenate([pair_blocks(in_block_diag(bb_re)), pair_blocks(in_block_diag(bb_im))],
                                  axis=2).astype(BF16)
    tiles = jnp.arange(D_SSM // LANES)

    def tile_blocks(m):
        return m.reshape(4, 4 * LANES, 4, LANES).transpose(0, 2, 1, 3)[tiles, tiles]

    w['ssm_cre'] = tile_blocks(out_block_diag(c_re.astype(F32))).astype(BF16)
    w['ssm_cim'] = tile_blocks(out_block_diag(-c_im.astype(F32))).astype(BF16)
    w['ssm_ar'] = bar_re.reshape(1, D_STATE)
    w['ssm_ai'] = bar_im.reshape(1, D_STATE)
    w['d_skip'] = row(d_skip)
    w['w_glu'] = w_glu.astype(BF16)
    w['b_glu'] = row(b_glu)

    w['attn_out_norm_w'] = row(attn_out_norm_w)
    w['ssm_out_norm_w'] = row(ssm_out_norm_w)
    w['wo_a'] = w_out[:D_ATTN].astype(BF16)
    w['wo_s'] = w_out[D_ATTN:].astype(BF16)
    w['norm_ffn_w'] = row(norm_ffn_w)
    wr = jnp.pad(w_router.astype(F32), ((0, 0), (0, LANES - N_EXPERTS)))
    w['wr_hi'] = wr.astype(BF16)
    w['wr_lo'] = (wr - w['wr_hi'].astype(F32)).astype(BF16)
    w['router_bias'] = jnp.broadcast_to(jnp.pad(router_bias.astype(F32), (0, LANES - N_EXPERTS))[:, None],
                                        (LANES, LANES))
    w['wgu_s'] = jnp.concatenate([w_gate_s, w_up_s], axis=1).astype(BF16)
    w['wd_s'] = w_down_s.astype(BF16)
    w['w_gate_e'], w['w_up_e'], w['w_down_e'] = w_gate_e, w_up_e, w_down_e
    return w


def _mix_prompt(x, w, tables):
    bsz, length, _ = x.shape
    q, kt, vt, kb, vb, lf, u_tb = _inproj(x, w)
    ct = _cumsum_lanes(lf.reshape(bsz, length, N_HEADS).transpose(0, 2, 1))
    attn = _attn_prompt(q, kb, vb, ct, bsz, length)
    h0 = jnp.zeros((bsz, 2 * D_STATE), F32)
    ssm_lb, hlast = _ssm(u_tb, h0, w, bsz, length)
    xres, hn, logits = _outproj(attn, ssm_lb, x, w, tables, 0)
    heads_last = lambda a: a.reshape(1, bsz, N_HEADS, HEAD_DIM, length).transpose(0, 1, 4, 2, 3)
    caches = (heads_last(kt), heads_last(vt),
              lf.reshape(1, bsz, length, N_HEADS),
              hlast[:, :D_STATE].reshape(1, bsz, N_SSM_GROUPS, SSM_STATE),
              hlast[:, D_STATE:].reshape(1, bsz, N_SSM_GROUPS, SSM_STATE))
    return xres, hn, logits, caches


def _mix_sample(x, cache_k, cache_v, cache_logf, page_table, h0_re, h0_im, w, tables, row0):
    bsz, n_new, _ = x.shape
    rows = n_new * bsz
    xt = x.transpose(1, 0, 2).reshape(1, rows, D_MODEL)
    q, kt, vt, _, _, lf, u_tb = _inproj(xt, w)
    k, v = kt[0].T, vt[0].T
    to_bm = lambda a: a.reshape(n_new, bsz, -1).transpose(1, 0, 2)
    q_b, k_b, v_b, lf_b = to_bm(q), to_bm(k), to_bm(v), to_bm(lf)
    head = jnp.arange(D_ATTN) // HEAD_DIM
    hmask = (head[None, :] == jnp.arange(N_HEADS)[:, None]).astype(BF16)
    qbd = (q_b[:, None, :, :] * hmask[None, :, None, :]).reshape(bsz, N_HEADS * n_new, D_ATTN)
    pad_keys = ((0, 0), (0, PAGE_SIZE - n_new), (0, 0))
    n_pool = cache_k.shape[1]
    keys_minor = lambda c: c[0].transpose(0, 2, 3, 1).reshape(n_pool, D_ATTN, PAGE_SIZE)
    attn = _attn_sample(
        qbd, keys_minor(cache_k), keys_minor(cache_v),
        cache_logf[0].astype(F32).transpose(0, 2, 1), page_table.astype(jnp.int32),
        jnp.pad(k_b, pad_keys).transpose(0, 2, 1), jnp.pad(v_b, pad_keys).transpose(0, 2, 1),
        jnp.pad(lf_b.transpose(0, 2, 1), ((0, 0), (0, 0), (0, PAGE_SIZE - n_new))), n_new)
    attn_tm = attn.transpose(1, 0, 2).reshape(rows, D_ATTN)
    h0 = jnp.concatenate([h0_re.reshape(bsz, D_STATE), h0_im.reshape(bsz, D_STATE)], axis=1).astype(F32)
    ssm_lb, hlast = _ssm(u_tb.reshape(n_new, bsz * D_SSM), h0, w, bsz, n_new)
    xres, hn, logits = _outproj(attn_tm, ssm_lb.reshape(rows, D_SSM), xt, w, tables, row0)
    caches = (k_b.reshape(1, bsz, n_new, N_HEADS, HEAD_DIM), v_b.reshape(1, bsz, n_new, N_HEADS, HEAD_DIM),
              lf_b.reshape(1, bsz, n_new, N_HEADS),
              hlast[:, :D_STATE].reshape(1, bsz, N_SSM_GROUPS, SSM_STATE),
              hlast[:, D_STATE:].reshape(1, bsz, N_SSM_GROUPS, SSM_STATE))
    return xres, hn, logits, caches


def _moe(hn, logits, w, block, row0, n_tok):
    tri = (jnp.arange(ROUTE_TILE)[:, None] < jnp.arange(ROUTE_TILE)[None, :]).astype(BF16)
    idx, gates, rank, counts = _route(logits, w['router_bias'], tri, row0, n_tok)
    counts = counts[:, 0].astype(jnp.int32)
    padded = (counts + block - 1) // block * block
    experts = jnp.arange(N_EXPERTS, dtype=jnp.int32)
    pend = jnp.sum(jnp.where(experts[None, :] <= experts[:, None], padded[None, :], 0), axis=1)
    pstart = pend - padded
    n_blk = -(-(n_tok * TOP_K) // block) + N_EXPERTS
    n_slots = n_blk * block
    pstart_rows = jnp.broadcast_to(jnp.pad(pstart, (0, LANES - N_EXPERTS)), (TOP_K, LANES))
    pos = _slot_positions(idx, rank, pstart_rows).reshape(-1)
    blk_start = jnp.arange(n_blk, dtype=jnp.int32) * block
    blk_expert = jnp.minimum(jnp.sum(pend[None, :] <= blk_start[:, None], axis=1), N_EXPERTS - 1).astype(jnp.int32)
    seg_end = jnp.sum(jnp.where(blk_expert[:, None] == experts[None, :], (pstart + counts)[None, :], 0), axis=1)
    blk_valid = jnp.clip(seg_end - blk_start, 0, block).astype(jnp.int32)
    plane_pos = (pos[None, :] + n_slots * jnp.arange(N_PLANES, dtype=jnp.int32)[:, None]).reshape(-1)
    xs = _scatter_rows(hn.reshape(N_PLANES * hn.shape[1], LANES), N_PLANES, row0, n_tok, plane_pos,
                       N_PLANES * n_slots)
    ys = _expert_ffn(xs.reshape(N_PLANES, n_slots, LANES), blk_expert, blk_valid,
                     w['w_gate_e'], w['w_up_e'], w['w_down_e'], block)
    ysg = _gather_rows(ys.reshape(N_PLANES * n_slots, LANES), plane_pos)
    return gates, ysg.reshape(N_PLANES, TOP_K * n_tok, LANES)


def kernel(x_prompt, x_sample, cache_k, cache_v, cache_logf, page_table, state_ssm_re, state_ssm_im, norm_mix_w, w_in, b_forget, q_norm_w, k_norm_w, lambda_re, lambda_im, log_dt, b_re, b_im, c_re, c_im, d_skip, w_glu, b_glu, attn_out_norm_w, ssm_out_norm_w, w_out, norm_ffn_w, w_router, router_bias, w_gate_e, w_up_e, w_down_e, w_gate_s, w_up_s, w_down_s):
    assert norm_mix_w.shape[0] == 1, "single-layer trunk"
    w = _prepare_weights(norm_mix_w[0], w_in[0], b_forget[0], q_norm_w[0], k_norm_w[0], lambda_re[0], lambda_im[0],
                         log_dt[0], b_re[0], b_im[0], c_re[0], c_im[0], d_skip[0], w_glu[0], b_glu[0],
                         attn_out_norm_w[0], ssm_out_norm_w[0], w_out[0], norm_ffn_w[0], w_router[0], router_bias[0],
                         w_gate_e[0], w_up_e[0], w_down_e[0], w_gate_s[0], w_up_s[0], w_down_s[0])
    bp, lp, _ = x_prompt.shape
    bs, ls, _ = x_sample.shape
    n_p, n_s = bp * lp, bs * ls
    tables = (jnp.zeros((N_PLANES, n_p + n_s, LANES), jnp.int32), jnp.zeros((LANES, n_p + n_s), F32))
    y_p, hn, lg, caches_p = _mix_prompt(x_prompt, w, tables)
    y_s, hn, lg, caches_s = _mix_sample(x_sample, cache_k, cache_v, cache_logf, page_table,
                                        state_ssm_re[0], state_ssm_im[0], w, (hn, lg), n_p)
    group = n_p // MOE_GROUPS
    for g in range(MOE_GROUPS):
        last = g == MOE_GROUPS - 1
        gates, ysg = _moe(hn, lg, w, FFN_BLOCK, g * group, group + (n_s if last else 0))
        y_p = _combine(y_p, g * group, gates, ysg, 0, group)
    y_s = _combine(y_s, 0, gates, ysg, group, n_s).reshape(ls, bs, D_MODEL).transpose(1, 0, 2)
    y_p = y_p.reshape(bp, lp, D_MODEL)
    return (y_p, y_s) + caches_p + caches_s
```
